```python
import jax, jax.numpy as jnp
from jax import lax
import numpy as np

D_MODEL = 1024
BATCH = 8
SEQ = 2048
DEPTH = 4
DEC_BATCH = 128
DEC_SEQ = 8
PAST_LEN = 16384
PAGE_SIZE = 128

A_HEADS = 8
A_HEAD_DIM = 64
A_WIDTH = A_HEADS * A_HEAD_DIM
DECAY_LORA = 64
AAA_LORA = 64
GATE_LORA = 128
A_PROJ = 3 * A_WIDTH + DECAY_LORA + AAA_LORA + GATE_LORA
B_BLOCKS = 8
B_WIDTH = 512
B_BLOCK = B_WIDTH // B_BLOCKS
CONV_W = 4
LRU_C = 8.0
C_HEADS = 4
C_KDIM = 128
C_VDIM = 128
C_KWIDTH = C_HEADS * C_KDIM
C_WIDTH = C_HEADS * C_VDIM
CHUNK = 64
N_IN = A_PROJ + 2 * B_WIDTH + 2 * C_KWIDTH + 2 * C_WIDTH + 3 * D_MODEL
D_FF = 2816
RMS_EPS = 1e-6
GN_EPS = 64e-5

kernel_name = 'hybrid_rwkv7_rglru_hgrn2_decode_step'


def rmsnorm(x, g):
    xf = x.astype(jnp.float32)
    y = xf * lax.rsqrt(jnp.mean(xf * xf, axis=-1, keepdims=True) + RMS_EPS)
    return (y * g.astype(jnp.float32)).astype(x.dtype)


def swiglu(x, wg, wu, wd):
    return (jax.nn.silu(x @ wg) * (x @ wu)) @ wd


def rwkv7_mixer(p, shift_prev, s0, mu, w0, w2, a0, a2, g2, k_k, k_a, r_k, ln_g, ln_b):
    f32 = jnp.float32
    bsz, t_len, _ = p.shape
    prev = jnp.concatenate([shift_prev[:, None, :].astype(p.dtype), p[:, :-1]], axis=1)
    xs = p + mu * (prev - p)
    r, k, v, w1, a1, g1 = jnp.split(xs, [A_WIDTH, 2 * A_WIDTH, 3 * A_WIDTH, 3 * A_WIDTH + DECAY_LORA, 3 * A_WIDTH + DECAY_LORA + AAA_LORA], axis=-1)
    w = -jax.nn.softplus(-(w0 + jnp.tanh(w1) @ w2).astype(f32)) - 0.5
    decay = jnp.exp(-jnp.exp(w))
    a = jax.nn.sigmoid((a0 + a1 @ a2).astype(f32))
    g = (jax.nn.sigmoid(g1) @ g2).astype(f32)
    hs = (bsz, t_len, A_HEADS, A_HEAD_DIM)
    hp = (A_HEADS, A_HEAD_DIM)
    r = r.astype(f32).reshape(hs)
    k = k.astype(f32).reshape(hs)
    v = v.astype(f32).reshape(hs)
    decay = decay.reshape(hs)
    a = a.reshape(hs)
    kk = k * k_k.astype(f32).reshape(hp)
    kk = kk / jnp.maximum(jnp.sqrt(jnp.sum(kk * kk, axis=-1, keepdims=True)), 1e-12)
    k = k * (1.0 + (a - 1.0) * k_a.astype(f32).reshape(hp))

    def step(s, inp):
        r_t, w_t, k_t, v_t, kk_t, b_t = inp
        s_kk = jnp.einsum('bhvk,bhk->bhv', s, -kk_t)
        s = s * w_t[:, :, None, :] + s_kk[..., None] * b_t[:, :, None, :] + v_t[..., None] * k_t[:, :, None, :]
        return s, jnp.einsum('bhvk,bhk->bhv', s, r_t)

    xs_t = tuple(jnp.moveaxis(z, 1, 0) for z in (r, decay, k, v, kk, kk * a))
    s_last, y = lax.scan(step, s0.astype(f32), xs_t)
    y = jnp.moveaxis(y, 0, 1)
    mean = jnp.mean(y, axis=-1, keepdims=True)
    var = jnp.mean(jnp.square(y - mean), axis=-1, keepdims=True)
    y = (y - mean) * lax.rsqrt(var + GN_EPS) * ln_g.astype(f32).reshape(hp) + ln_b.astype(f32).reshape(hp)
    y = y + jnp.sum(r * k * r_k.astype(f32), axis=-1, keepdims=True) * v
    out = (y.reshape(bsz, t_len, A_WIDTH) * g).astype(p.dtype)
    return out, p[:, -1], s_last


def _lin_combine(earlier, later):
    a1, b1 = earlier
    a2, b2 = later
    return a1 * a2, a2 * b1 + b2


def rglru_mixer(p, conv_prev, h0, conv_w, conv_b, wa, ba, wx, bx, lam):
    f32 = jnp.float32
    bsz, t_len, _ = p.shape
    xb, gb = jnp.split(p, [B_WIDTH], axis=-1)
    xpad = jnp.concatenate([conv_prev.astype(p.dtype), xb], axis=1)
    xc = conv_b
    for j in range(CONV_W):
        xc = xc + xpad[:, j:j + t_len] * conv_w[j]
    xh = xc.reshape(bsz, t_len, B_BLOCKS, B_BLOCK)
    r = jax.nn.sigmoid((jnp.einsum('btgi,gij->btgj', xh, wa).reshape(bsz, t_len, B_WIDTH) + ba).astype(f32))
    i = jax.nn.sigmoid((jnp.einsum('btgi,gij->btgj', xh, wx).reshape(bsz, t_len, B_WIDTH) + bx).astype(f32))
    log_a = -LRU_C * r * jax.nn.softplus(-lam.astype(f32))
    a = jnp.exp(log_a)
    b = jnp.sqrt(-jnp.expm1(2.0 * log_a)) * i * xc.astype(f32)
    a_cum, b_cum = lax.associative_scan(_lin_combine, (a, b), axis=1)
    h = a_cum * h0.astype(f32)[:, None, :] + b_cum
    out = (h * jax.nn.gelu(gb.astype(f32))).astype(p.dtype)
    return out, xpad[:, t_len:], h[:, -1]


def hgrn2_mixer(p, s0, lb, norm_g):
    f32 = jnp.float32
    bsz, t_len, _ = p.shape
    q, fz, iv, gz = jnp.split(p, [C_KWIDTH, 2 * C_KWIDTH, 2 * C_KWIDTH + C_WIDTH], axis=-1)
    q = jax.nn.silu(q.astype(f32)).reshape(bsz, t_len, C_HEADS, C_KDIM)
    f = lb + (1.0 - lb) * jax.nn.sigmoid(fz.astype(f32))
    logf = jnp.log(f).reshape(bsz, t_len, C_HEADS, C_KDIM)
    k = (1.0 - f).reshape(bsz, t_len, C_HEADS, C_KDIM)
    v = iv.astype(f32).reshape(bsz, t_len, C_HEADS, C_VDIM)
    c = min(CHUNK, t_len)
    n_chunks = -(-t_len // c)
    pad = n_chunks * c - t_len

    def to_chunks(z):
        z = jnp.pad(z, ((0, 0), (0, pad), (0, 0), (0, 0)))
        return z.reshape(bsz, n_chunks, c, C_HEADS, -1).transpose(1, 0, 3, 2, 4)

    mask = jnp.tril(jnp.ones((c, c), dtype=bool))

    def step(s, inp):
        q_c, lf_c, k_c, v_c = inp
        g_cum = jnp.cumsum(lf_c, axis=2)
        o_inter = jnp.einsum('bhtk,bhkv->bhtv', q_c * jnp.exp(g_cum), s)
        diff = g_cum[:, :, :, None, :] - g_cum[:, :, None, :, :]
        dec = jnp.exp(jnp.where(mask[:, :, None], diff, -jnp.inf))
        att = jnp.einsum('bhtk,bhsk,bhtsk->bhts', q_c, k_c, dec)
        o = o_inter + jnp.einsum('bhts,bhsv->bhtv', att, v_c)
        g_last = g_cum[:, :, -1:, :]
        s = jnp.exp(g_last)[:, :, 0, :, None] * s + jnp.einsum('bhsk,bhsv->bhkv', k_c * jnp.exp(g_last - g_cum), v_c)
        return s, o

    s_last, o = lax.scan(step, s0.astype(f32), tuple(to_chunks(z) for z in (q, logf, k, v)))
    o = o.transpose(1, 0, 3, 2, 4).reshape(bsz, n_chunks * c, C_HEADS, C_VDIM)[:, :t_len]
    o = o * lax.rsqrt(jnp.mean(o * o, axis=-1, keepdims=True) + RMS_EPS) * norm_g.astype(f32).reshape(C_HEADS, C_VDIM)
    out = (o.reshape(bsz, t_len, C_WIDTH) * jax.nn.silu(gz.astype(f32))).astype(p.dtype)
    return out, s_last


def token_mix(u, l, P, lb_l, s_rwkv, s_shift, s_lru, s_conv, s_hgrn):
    proj = u @ P['w_in'][l]
    pa, pb, pc, pg = jnp.split(proj, [A_PROJ, A_PROJ + 2 * B_WIDTH, A_PROJ + 2 * B_WIDTH + 2 * C_KWIDTH + 2 * C_WIDTH], axis=-1)
    oa, n_shift, n_rwkv = rwkv7_mixer(pa, s_shift, s_rwkv, P['rwkv_mu'][l], P['rwkv_w0'][l], P['rwkv_w2'][l], P['rwkv_a0'][l], P['rwkv_a2'][l], P['rwkv_g2'][l], P['rwkv_k_k'][l], P['rwkv_k_a'][l], P['rwkv_r_k'][l], P['rwkv_ln_g'][l], P['rwkv_ln_b'][l])
    ob, n_conv, n_lru = rglru_mixer(pb, s_conv, s_lru, P['lru_conv_w'][l], P['lru_conv_b'][l], P['lru_wa'][l], P['lru_ba'][l], P['lru_wx'][l], P['lru_bx'][l], P['lru_lam'][l])
    oc, n_hgrn = hgrn2_mixer(pc, s_hgrn, lb_l, P['hgrn_norm_g'][l])
    ga, gb, gc = jnp.split(jax.nn.sigmoid(pg), 3, axis=-1)
    merged = ga * (oa @ P['proj_a'][l]) + gb * (ob @ P['proj_b'][l]) + gc * (oc @ P['proj_c'][l])
    return merged @ P['w_out'][l], (n_rwkv, n_shift, n_lru, n_conv, n_hgrn)


def run_trunk(x, s_rwkv, s_shift, s_lru, s_conv, s_hgrn, P, lb):
    collected = ([], [], [], [], [])
    for l in range(DEPTH):
        h = swiglu(rmsnorm(x, P['ffn1_pre_g'][l]), P['ffn1_wg'][l], P['ffn1_wu'][l], P['ffn1_wd'][l])
        x = x + 0.5 * rmsnorm(h, P['ffn1_post_g'][l])
        m, new = token_mix(rmsnorm(x, P['mix_pre_g'][l]), l, P, lb[l], s_rwkv[l], s_shift[l], s_lru[l], s_conv[l], s_hgrn[l])
        x = x + rmsnorm(m, P['mix_post_g'][l])
        h = swiglu(rmsnorm(x, P['ffn2_pre_g'][l]), P['ffn2_wg'][l], P['ffn2_wu'][l], P['ffn2_wd'][l])
        x = x + 0.5 * rmsnorm(h, P['ffn2_post_g'][l])
        for lst, t in zip(collected, new):
            lst.append(t)
    dt = s_rwkv.dtype
    return x, tuple(jnp.stack(lst, axis=0).astype(dt) for lst in collected)


def setup_inputs(seed: int = 0) -> dict:
    key = jax.random.key(seed)
    ks = iter(jax.random.split(key, 80))
    f32 = jnp.float32

    def nrm(shape, scale):
        return scale * jax.random.normal(next(ks), shape, f32)

    def gain(shape):
        return 1.0 + nrm(shape, 0.05)

    def unif(shape, lo, hi):
        return jax.random.uniform(next(ks), shape, f32, lo, hi)

    D, L = D_MODEL, DEPTH
    inp = {}
    inp['x_prompt'] = nrm((BATCH, SEQ, D), 1.0)
    inp['x_sample'] = nrm((DEC_BATCH, DEC_SEQ, D), 1.0)
    inp['state_rwkv'] = nrm((L, DEC_BATCH, A_HEADS, A_HEAD_DIM, A_HEAD_DIM), 0.1)
    inp['state_shift'] = nrm((L, DEC_BATCH, A_PROJ), 1.0)
    inp['state_lru'] = nrm((L, DEC_BATCH, B_WIDTH), 0.5)
    inp['state_conv'] = nrm((L, DEC_BATCH, CONV_W - 1, B_WIDTH), 1.0)
    inp['state_hgrn'] = nrm((L, DEC_BATCH, C_HEADS, C_KDIM, C_VDIM), 0.5)
    inp['ffn1_pre_g'] = gain((L, D))
    inp['ffn1_post_g'] = gain((L, D))
    inp['ffn1_wg'] = nrm((L, D, D_FF), D ** -0.5)
    inp['ffn1_wu'] = nrm((L, D, D_FF), D ** -0.5)
    inp['ffn1_wd'] = nrm((L, D_FF, D), D_FF ** -0.5)
    inp['mix_pre_g'] = gain((L, D))
    inp['mix_post_g'] = gain((L, D))
    inp['w_in'] = nrm((L, D, N_IN), D ** -0.5)
    inp['rwkv_mu'] = unif((L, A_PROJ), 0.0, 1.0)
    inp['rwkv_w0'] = nrm((L, A_WIDTH), 0.5)
    inp['rwkv_w2'] = nrm((L, DECAY_LORA, A_WIDTH), 0.1)
    inp['rwkv_a0'] = nrm((L, A_WIDTH), 0.1)
    inp['rwkv_a2'] = nrm((L, AAA_LORA, A_WIDTH), 0.1)
    inp['rwkv_g2'] = nrm((L, GATE_LORA, A_WIDTH), GATE_LORA ** -0.5)
    inp['rwkv_k_k'] = 0.85 + nrm((L, A_WIDTH), 0.05)
    inp['rwkv_k_a'] = gain((L, A_WIDTH))
    inp['rwkv_r_k'] = nrm((L, A_HEADS, A_HEAD_DIM), 0.1)
    inp['rwkv_ln_g'] = gain((L, A_WIDTH))
    inp['rwkv_ln_b'] = nrm((L, A_WIDTH), 0.02)
    inp['lru_conv_w'] = nrm((L, CONV_W, B_WIDTH), 0.5)
    inp['lru_conv_b'] = nrm((L, B_WIDTH), 0.02)
    inp['lru_wa'] = nrm((L, B_BLOCKS, B_BLOCK, B_BLOCK), B_BLOCK ** -0.5)
    inp['lru_ba'] = nrm((L, B_WIDTH), 0.02)
    inp['lru_wx'] = nrm((L, B_BLOCKS, B_BLOCK, B_BLOCK), B_BLOCK ** -0.5)
    inp['lru_bx'] = nrm((L, B_WIDTH), 0.02)
    inp['lru_lam'] = unif((L, B_WIDTH), 2.0, 6.0)
    inp['hgrn_lb_logits'] = nrm((L, C_KWIDTH), 1.0)
    inp['hgrn_norm_g'] = gain((L, C_WIDTH))
    inp['proj_a'] = nrm((L, A_WIDTH, D), A_WIDTH ** -0.5)
    inp['proj_b'] = nrm((L, B_WIDTH, D), B_WIDTH ** -0.5)
    inp['proj_c'] = nrm((L, C_WIDTH, D), C_WIDTH ** -0.5)
    inp['w_out'] = nrm((L, D, D), D ** -0.5)
    inp['ffn2_pre_g'] = gain((L, D))
    inp['ffn2_post_g'] = gain((L, D))
    inp['ffn2_wg'] = nrm((L, D, D_FF), D ** -0.5)
    inp['ffn2_wu'] = nrm((L, D, D_FF), D ** -0.5)
    inp['ffn2_wd'] = nrm((L, D_FF, D), D_FF ** -0.5)
    return inp


def reference(x_prompt, x_sample, state_rwkv, state_shift, state_lru, state_conv, state_hgrn,
              ffn1_pre_g, ffn1_post_g, ffn1_wg, ffn1_wu, ffn1_wd,
              mix_pre_g, mix_post_g, w_in,
              rwkv_mu, rwkv_w0, rwkv_w2, rwkv_a0, rwkv_a2, rwkv_g2, rwkv_k_k, rwkv_k_a, rwkv_r_k, rwkv_ln_g, rwkv_ln_b,
              lru_conv_w, lru_conv_b, lru_wa, lru_ba, lru_wx, lru_bx, lru_lam,
              hgrn_lb_logits, hgrn_norm_g,
              proj_a, proj_b, proj_c, w_out,
              ffn2_pre_g, ffn2_post_g, ffn2_wg, ffn2_wu, ffn2_wd):
    P = {
        'ffn1_pre_g': ffn1_pre_g, 'ffn1_post_g': ffn1_post_g, 'ffn1_wg': ffn1_wg, 'ffn1_wu': ffn1_wu, 'ffn1_wd': ffn1_wd,
        'mix_pre_g': mix_pre_g, 'mix_post_g': mix_post_g, 'w_in': w_in,
        'rwkv_mu': rwkv_mu, 'rwkv_w0': rwkv_w0, 'rwkv_w2': rwkv_w2, 'rwkv_a0': rwkv_a0, 'rwkv_a2': rwkv_a2,
        'rwkv_g2': rwkv_g2, 'rwkv_k_k': rwkv_k_k, 'rwkv_k_a': rwkv_k_a, 'rwkv_r_k': rwkv_r_k,
        'rwkv_ln_g': rwkv_ln_g, 'rwkv_ln_b': rwkv_ln_b,
        'lru_conv_w': lru_conv_w, 'lru_conv_b': lru_conv_b, 'lru_wa': lru_wa, 'lru_ba': lru_ba,
        'lru_wx': lru_wx, 'lru_bx': lru_bx, 'lru_lam': lru_lam,
        'hgrn_norm_g': hgrn_norm_g,
        'proj_a': proj_a, 'proj_b': proj_b, 'proj_c': proj_c, 'w_out': w_out,
        'ffn2_pre_g': ffn2_pre_g, 'ffn2_post_g': ffn2_post_g, 'ffn2_wg': ffn2_wg, 'ffn2_wu': ffn2_wu, 'ffn2_wd': ffn2_wd,
    }
    lb_cum = jnp.cumsum(jax.nn.softmax(hgrn_lb_logits.astype(jnp.float32), axis=0), axis=0)
    lb = lb_cum - lb_cum[0]
    bp = x_prompt.shape[0]
    dt = state_rwkv.dtype
    y_prompt, p_st = run_trunk(
        x_prompt,
        jnp.zeros((DEPTH, bp, A_HEADS, A_HEAD_DIM, A_HEAD_DIM), dt),
        jnp.zeros((DEPTH, bp, A_PROJ), dt),
        jnp.zeros((DEPTH, bp, B_WIDTH), dt),
        jnp.zeros((DEPTH, bp, CONV_W - 1, B_WIDTH), dt),
        jnp.zeros((DEPTH, bp, C_HEADS, C_KDIM, C_VDIM), dt),
        P, lb)
    y_sample, s_st = run_trunk(x_sample, state_rwkv, state_shift, state_lru, state_conv, state_hgrn, P, lb)
    return (y_prompt, y_sample, p_st[0], p_st[1], p_st[2], p_st[3], p_st[4], s_st[0], s_st[1], s_st[2], s_st[3], s_st[4])
```

```python
import functools

import jax
import jax.numpy as jnp
from jax import lax
from jax.experimental import pallas as pl
from jax.experimental.pallas import tpu as pltpu

F32 = jnp.float32
BF16 = jnp.bfloat16
RMS_EPS = 1e-6
GN_EPS = 64e-5
LRU_C = 8.0
KK_EPS = 1e-12

VMEM_LIMIT_BYTES = 48 * 1024 * 1024
SUBLANES = 8
HGRN_SUB = 16
RWKV_CHUNK = 64
HGRN_CHUNK = 64
LRU_TILE = 512


def _cparams(*sem):
    return pltpu.CompilerParams(dimension_semantics=sem, vmem_limit_bytes=VMEM_LIMIT_BYTES)


def _mm(a, b):
    return jnp.dot(a.astype(BF16), b.astype(BF16), preferred_element_type=F32)


def _mm_nt(a, b):
    return lax.dot_general(a.astype(BF16), b.astype(BF16), (((1,), (1,)), ((), ())),
                           preferred_element_type=F32)


def _mm_tn(a, b):
    return lax.dot_general(a.astype(BF16), b.astype(BF16), (((0,), (0,)), ((), ())),
                           preferred_element_type=F32)


def _split3(x):
    hi = x.astype(BF16)
    r1 = x - hi.astype(F32)
    mid = r1.astype(BF16)
    lo = (r1 - mid.astype(F32)).astype(BF16)
    return hi, mid, lo


def _cumsum_rows(x):
    c = x.shape[0]
    row = lax.broadcasted_iota(jnp.int32, (c, c), 0)
    col = lax.broadcasted_iota(jnp.int32, (c, c), 1)
    tri = (col <= row).astype(BF16)
    hi, mid, lo = _split3(x)
    out = jnp.dot(tri, lo, preferred_element_type=F32)
    out = out + jnp.dot(tri, mid, preferred_element_type=F32)
    return out + jnp.dot(tri, hi, preferred_element_type=F32)


def _rms(x, g):
    return x * lax.rsqrt(jnp.mean(x * x, axis=-1, keepdims=True) + RMS_EPS) * g


def _softplus(x):
    return jnp.maximum(x, 0.0) + jnp.log1p(jnp.exp(-jnp.abs(x)))


def _sigmoid(x):
    return 1.0 / (1.0 + jnp.exp(-x))


def _silu(x):
    return x * _sigmoid(x)


def _tile(n, pref):
    t = min(n, pref)
    while n % t:
        t //= 2
    return t


def _norm_matmul_kernel(x_ref, g_ref, w_ref, o_ref, xn_ref):
    @pl.when(pl.program_id(1) == 0)
    def _():
        xn_ref[...] = _rms(x_ref[...], g_ref[...]).astype(BF16)

    o_ref[...] = jnp.dot(xn_ref[...], w_ref[...], preferred_element_type=F32)


def _norm_matmul(x, g, w, l, tn):
    m, d = x.shape
    n = w.shape[-1]
    tm = _tile(m, 512)
    return pl.pallas_call(
        _norm_matmul_kernel,
        grid=(m // tm, n // tn),
        in_specs=[
            pl.BlockSpec((tm, d), lambda i, j: (i, 0)),
            pl.BlockSpec((None, 1, d), lambda i, j: (l, 0, 0)),
            pl.BlockSpec((None, d, tn), lambda i, j: (l, 0, j)),
        ],
        out_specs=pl.BlockSpec((tm, tn), lambda i, j: (i, j)),
        out_shape=jax.ShapeDtypeStruct((m, n), F32),
        scratch_shapes=[pltpu.VMEM((tm, d), BF16)],
        compiler_params=_cparams("parallel", "arbitrary"),
        name="norm_matmul",
    )(x, g, w)


def _ffn_kernel(x_ref, gpre_ref, wg_ref, wu_ref, wd_ref, gpost_ref, o_ref, xn_ref, acc_ref):
    f = pl.program_id(1)

    @pl.when(f == 0)
    def _():
        xn_ref[...] = _rms(x_ref[...], gpre_ref[...]).astype(BF16)
        acc_ref[...] = jnp.zeros_like(acc_ref)

    xn = xn_ref[...]
    hg = jnp.dot(xn, wg_ref[...], preferred_element_type=F32)
    hu = jnp.dot(xn, wu_ref[...], preferred_element_type=F32)
    h = _silu(hg) * hu
    acc_ref[...] += jnp.dot(h.astype(BF16), wd_ref[...], preferred_element_type=F32)

    @pl.when(f == pl.num_programs(1) - 1)
    def _():
        o_ref[...] = x_ref[...] + 0.5 * _rms(acc_ref[...], gpost_ref[...])


def _ffn(x, gpre, wg, wu, wd, gpost, l):
    m, d = x.shape
    ff = wg.shape[-1]
    tm = _tile(m, 512)
    tf = ff // 2 if (ff // 2) % 128 == 0 else ff
    return pl.pallas_call(
        _ffn_kernel,
        grid=(m // tm, ff // tf),
        in_specs=[
            pl.BlockSpec((tm, d), lambda i, f: (i, 0)),
            pl.BlockSpec((None, 1, d), lambda i, f: (l, 0, 0)),
            pl.BlockSpec((None, d, tf), lambda i, f: (l, 0, f)),
            pl.BlockSpec((None, d, tf), lambda i, f: (l, 0, f)),
            pl.BlockSpec((None, tf, d), lambda i, f: (l, f, 0)),
            pl.BlockSpec((None, 1, d), lambda i, f: (l, 0, 0)),
        ],
        out_specs=pl.BlockSpec((tm, d), lambda i, f: (i, 0)),
        out_shape=jax.ShapeDtypeStruct((m, d), F32),
        scratch_shapes=[pltpu.VMEM((tm, d), BF16), pltpu.VMEM((tm, d), F32)],
        compiler_params=_cparams("parallel", "arbitrary"),
        name="ffn",
    )(x, gpre, wg, wu, wd, gpost)


def _merge_kernel(x_ref, pg_ref, oa_ref, ob_ref, oc_ref, pa_ref, pb_ref, pc_ref, wo_ref, g_ref,
                  o_ref):
    d = x_ref.shape[-1]
    pg = pg_ref[...]
    merged = _sigmoid(pg[:, :d]) * _mm(oa_ref[...], pa_ref[...])
    merged += _sigmoid(pg[:, d:2 * d]) * _mm(ob_ref[...], pb_ref[...])
    merged += _sigmoid(pg[:, 2 * d:]) * _mm(oc_ref[...], pc_ref[...])
    y = _mm(merged, wo_ref[...])
    o_ref[...] = x_ref[...] + _rms(y, g_ref[...])


def _merge(x, pg, oa, ob, oc, pa, pb, pc, wo, g, l):
    m, d = x.shape
    tm = _tile(m, 512)

    def rows(w):
        return pl.BlockSpec((tm, w), lambda i: (i, 0))

    def whole(a):
        return pl.BlockSpec((None,) + a.shape[1:], lambda i: (l,) + (0,) * (a.ndim - 1))

    return pl.pallas_call(
        _merge_kernel,
        grid=(m // tm,),
        in_specs=[rows(d), rows(3 * d), rows(oa.shape[1]), rows(ob.shape[1]), rows(oc.shape[1]),
                  whole(pa), whole(pb), whole(pc), whole(wo), whole(g)],
        out_specs=rows(d),
        out_shape=jax.ShapeDtypeStruct((m, d), F32),
        compiler_params=_cparams("parallel"),
        name="merge_out",
    )(x, pg, oa, ob, oc, pa, pb, pc, wo, g)


def _rwkv_kernel(heads, hd, dw, da, p_ref, shift_ref, s0_ref, mu_ref, w0_ref, w2_ref, a0_ref,
                 a2_ref, g2_ref, kk_ref, ka_ref, rk_ref, lng_ref, lnb_ref,
                 o_ref, nshift_ref, ns_ref, st_ref, prev_ref):
    c = pl.program_id(1)
    aw = heads * hd
    clen = p_ref.shape[0]

    @pl.when(c == 0)
    def _():
        prev_ref[...] = shift_ref[...]
        st_ref[...] = s0_ref[...]

    p = p_ref[...]
    rowi = lax.broadcasted_iota(jnp.int32, (clen, 1), 0)
    prev = jnp.where(rowi == 0, prev_ref[...], pltpu.roll(p, 1, 0))
    prev_ref[...] = p[clen - 1:clen, :]
    xs = p + mu_ref[...] * (prev - p)
    r = xs[:, :aw]
    k = xs[:, aw:2 * aw]
    v = xs[:, 2 * aw:3 * aw]
    w1 = xs[:, 3 * aw:3 * aw + dw]
    a1 = xs[:, 3 * aw + dw:3 * aw + dw + da]
    g1 = xs[:, 3 * aw + dw + da:]

    w = -_softplus(-(w0_ref[...] + _mm(jnp.tanh(w1), w2_ref[...]))) - 0.5
    logw = -jnp.exp(w)
    a_icl = _sigmoid(a0_ref[...] + _mm(a1, a2_ref[...]))
    gate = _mm(_sigmoid(g1), g2_ref[...])
    kkraw = k * kk_ref[...]
    k2 = k * (1.0 + (a_icl - 1.0) * ka_ref[...])
    rkk = r * k2 * rk_ref[...]

    ginc = _cumsum_rows(logw)
    eg = jnp.exp(ginc)
    egx = jnp.exp(ginc - logw)
    einv = jnp.exp(-ginc)
    glast = ginc[clen - 1:clen, :]
    ehat = jnp.exp(glast - ginc)
    eglast = jnp.exp(glast)

    ri = lax.broadcasted_iota(jnp.int32, (clen, clen), 0)
    ci = lax.broadcasted_iota(jnp.int32, (clen, clen), 1)
    strict = ci < ri
    incl = ci <= ri
    eye = (ci == ri).astype(F32)

    for h in range(heads):
        sl = slice(h * hd, (h + 1) * hd)
        kkr = kkraw[:, sl]
        kk = kkr / jnp.maximum(jnp.sqrt(jnp.sum(kkr * kkr, axis=-1, keepdims=True)), KK_EPS)
        bvec = kk * a_icl[:, sl]
        at = -kk * egx[:, sl]
        rt = r[:, sl] * eg[:, sl]
        bt = bvec * einv[:, sl]
        kt = k2[:, sl] * einv[:, sl]
        bh = bvec * ehat[:, sl]
        kh = k2[:, sl] * ehat[:, sl]
        vh = v[:, sl]
        s0 = st_ref[h]

        a_ab = jnp.where(strict, _mm_nt(at, bt), 0.0)
        a_ak = jnp.where(strict, _mm_nt(at, kt), 0.0)
        a_rb = jnp.where(incl, _mm_nt(rt, bt), 0.0)
        a_rk = jnp.where(incl, _mm_nt(rt, kt), 0.0)

        tinv = eye + a_ab
        pw = a_ab
        span = 2
        while span < clen:
            pw = _mm(pw, pw)
            tinv = tinv + _mm(tinv, pw)
            span *= 2

        u = _mm(tinv, _mm_nt(at, s0) + _mm(a_ak, vh))
        y = _mm_nt(rt, s0) + _mm(a_rb, u) + _mm(a_rk, vh)
        s_new = s0 * eglast[:, sl] + _mm_tn(u, bh) + _mm_tn(vh, kh)
        st_ref[h] = s_new

        mean = jnp.mean(y, axis=-1, keepdims=True)
        yc = y - mean
        var = jnp.mean(yc * yc, axis=-1, keepdims=True)
        y = yc * lax.rsqrt(var + GN_EPS) * lng_ref[:, sl] + lnb_ref[:, sl]
        y = y + jnp.sum(rkk[:, sl], axis=-1, keepdims=True) * vh
        o_ref[:, sl] = y * gate[:, sl]

    @pl.when(c == pl.num_programs(1) - 1)
    def _():
        nshift_ref[...] = p[clen - 1:clen, :]
        ns_ref[...] = st_ref[...]


def _rwkv(pa, shift, s0, P, l, nseq, tlen, chunk):
    ap = pa.shape[1]
    heads, hd = s0.shape[2], s0.shape[3]
    aw = heads * hd
    dw = P['rwkv_w2'].shape[1]
    da = P['rwkv_a2'].shape[1]
    nch = tlen // chunk

    def vec(a):
        return pl.BlockSpec((None,) + a.shape[1:], lambda b, c: (l,) + (0,) * (a.ndim - 1))

    names = ['rwkv_mu', 'rwkv_w0', 'rwkv_w2', 'rwkv_a0', 'rwkv_a2', 'rwkv_g2', 'rwkv_k_k',
             'rwkv_k_a', 'rwkv_r_k', 'rwkv_ln_g', 'rwkv_ln_b']
    params = [P[n] for n in names]
    return pl.pallas_call(
        functools.partial(_rwkv_kernel, heads, hd, dw, da),
        grid=(nseq, nch),
        in_specs=[
            pl.BlockSpec((chunk, ap), lambda b, c: (b * nch + c, 0)),
            pl.BlockSpec((None, None, 1, ap), lambda b, c: (l, b, 0, 0)),
            pl.BlockSpec((None, None, heads, hd, hd), lambda b, c: (l, b, 0, 0, 0)),
        ] + [vec(a) for a in params],
        out_specs=[
            pl.BlockSpec((chunk, aw), lambda b, c: (b * nch + c, 0)),
            pl.BlockSpec((None, 1, ap), lambda b, c: (b, 0, 0)),
            pl.BlockSpec((None, heads, hd, hd), lambda b, c: (b, 0, 0, 0)),
        ],
        out_shape=[
            jax.ShapeDtypeStruct((nseq * tlen, aw), F32),
            jax.ShapeDtypeStruct((nseq, 1, ap), F32),
            jax.ShapeDtypeStruct((nseq, heads, hd, hd), F32),
        ],
        scratch_shapes=[pltpu.VMEM((heads, hd, hd), F32), pltpu.VMEM((1, ap), F32)],
        compiler_params=_cparams("parallel", "arbitrary"),
        name="rwkv7",
    )(pa, shift, s0, *params)


def _lru_kernel(bw, p_ref, cs_ref, h0_ref, cw_ref, cb_ref, wa_ref, ba_ref, wx_ref, bx_ref, lam_ref,
                o_ref, nconv_ref, nh_ref, tail_ref, h_ref):
    c = pl.program_id(1)
    tc = p_ref.shape[0]
    ncv = cs_ref.shape[0]

    @pl.when(c == 0)
    def _():
        tail_ref[...] = jnp.zeros_like(tail_ref)
        tail_ref[SUBLANES - ncv:, :] = cs_ref[...]
        h_ref[...] = h0_ref[...]

    xb = p_ref[:, :bw]
    gb = p_ref[:, bw:]
    rowi = lax.broadcasted_iota(jnp.int32, (tc, 1), 0)
    row8 = lax.broadcasted_iota(jnp.int32, (SUBLANES, 1), 0)
    tail = tail_ref[...]

    xc = cb_ref[...] + xb * cw_ref[ncv:ncv + 1, :]
    for d in range(1, ncv + 1):
        sh = pltpu.roll(xb, d, 0)
        head = jnp.where(row8 >= d, sh[:SUBLANES], pltpu.roll(tail, d, 0))
        if tc > SUBLANES:
            sh = jnp.concatenate([head, sh[SUBLANES:]], axis=0)
        else:
            sh = head
        xc = xc + sh * cw_ref[ncv - d:ncv - d + 1, :]
    tail_ref[...] = xb[tc - SUBLANES:, :]

    rg = _sigmoid(_mm(xc, wa_ref[...]) + ba_ref[...])
    ig = _sigmoid(_mm(xc, wx_ref[...]) + bx_ref[...])
    log_a = -LRU_C * rg * _softplus(-lam_ref[...])
    a = jnp.exp(log_a)
    th = jnp.tanh(log_a)
    b = jnp.sqrt(-2.0 * th / (1.0 - th)) * ig * xc

    d = 1
    while d < tc:
        keep = rowi >= d
        a_sh = jnp.where(keep, pltpu.roll(a, d, 0), 1.0)
        b_sh = jnp.where(keep, pltpu.roll(b, d, 0), 0.0)
        b = a * b_sh + b
        a = a_sh * a
        d *= 2
    h = a * h_ref[...] + b
    h_ref[...] = h[tc - 1:tc, :]
    o_ref[...] = h * jax.nn.gelu(gb)

    @pl.when(c == pl.num_programs(1) - 1)
    def _():
        nconv_ref[...] = xb[tc - ncv:, :]
        nh_ref[...] = h[tc - 1:tc, :]


def _lru(pb, conv, h0, P, l, nseq, tlen, tile):
    bw = h0.shape[-1]
    ncv = conv.shape[2]
    nt = tlen // tile

    def vec(a):
        return pl.BlockSpec((None,) + a.shape[1:], lambda b, c: (l,) + (0,) * (a.ndim - 1))

    names = ['lru_conv_w', 'lru_conv_b', 'lru_wa_bd', 'lru_ba', 'lru_wx_bd', 'lru_bx', 'lru_lam']
    params = [P[n] for n in names]
    return pl.pallas_call(
        functools.partial(_lru_kernel, bw),
        grid=(nseq, nt),
        in_specs=[
            pl.BlockSpec((tile, 2 * bw), lambda b, c: (b * nt + c, 0)),
            pl.BlockSpec((None, None, ncv, bw), lambda b, c: (l, b, 0, 0)),
            pl.BlockSpec((None, None, 1, bw), lambda b, c: (l, b, 0, 0)),
        ] + [vec(a) for a in params],
        out_specs=[
            pl.BlockSpec((tile, bw), lambda b, c: (b * nt + c, 0)),
            pl.BlockSpec((None, ncv, bw), lambda b, c: (b, 0, 0)),
            pl.BlockSpec((None, 1, bw), lambda b, c: (b, 0, 0)),
        ],
        out_shape=[
            jax.ShapeDtypeStruct((nseq * tlen, bw), F32),
            jax.ShapeDtypeStruct((nseq, ncv, bw), F32),
            jax.ShapeDtypeStruct((nseq, 1, bw), F32),
        ],
        scratch_shapes=[pltpu.VMEM((SUBLANES, bw), F32), pltpu.VMEM((1, bw), F32)],
        compiler_params=_cparams("parallel", "arbitrary"),
        name="rglru",
    )(pb, conv, h0, *params)


def _hgrn_kernel(heads, kd, vd, sub, p_ref, s0_ref, lb_ref, ng_ref, o_ref, ns_ref, st_ref):
    c = pl.program_id(1)
    clen = p_ref.shape[0]
    kw = heads * kd
    vw = heads * vd

    @pl.when(c == 0)
    def _():
        for h in range(heads):
            st_ref[h] = s0_ref[h].T

    p = p_ref[...]
    lb = lb_ref[...]
    q = _silu(p[:, :kw])
    f = lb + (1.0 - lb) * _sigmoid(p[:, kw:2 * kw])
    logf = jnp.log(f)
    kf = 1.0 - f
    v = p[:, 2 * kw:2 * kw + vw]
    gz = p[:, 2 * kw + vw:]

    g = _cumsum_rows(logf)
    eg = jnp.exp(g)
    glast = g[clen - 1:clen, :]
    khat = kf * jnp.exp(glast - g)
    eglast = jnp.exp(glast)
    qg = q * eg

    nsub = clen // sub
    srow = lax.broadcasted_iota(jnp.int32, (sub, 1), 0)
    lane = lax.broadcasted_iota(jnp.int32, (sub, 128), 1)

    for h in range(heads):
        ks = slice(h * kd, (h + 1) * kd)
        vs = slice(h * vd, (h + 1) * vd)
        st = st_ref[h]
        vh = v[:, vs]
        o_inter = _mm_nt(qg[:, ks], st)
        blocks = []
        for i in range(nsub):
            rs = slice(i * sub, (i + 1) * sub)
            g_i = g[rs, ks]
            q_i = q[rs, ks]
            k_i = kf[rs, ks]
            o_i = o_inter[rs, :]
            if i > 0:
                gref = g[i * sub - 1:i * sub, ks]
                q_rel = q_i * jnp.exp(g_i - gref)
                k_rel = kf[:i * sub, ks] * jnp.exp(gref - g[:i * sub, ks])
                o_i = o_i + _mm(_mm_nt(q_rel, k_rel), vh[:i * sub, :])
            att_t = jnp.zeros((sub, 128), F32)
            for t in range(sub):
                dlt = jnp.where(srow <= t, g_i[t:t + 1, :] - g_i, -jnp.inf)
                col = jnp.sum(q_i[t:t + 1, :] * k_i * jnp.exp(dlt), axis=-1, keepdims=True)
                att_t = jnp.where(lane == t, col, att_t)
            o_i = o_i + _mm_tn(att_t[:, :sub], vh[rs, :])
            blocks.append(o_i)
        o = blocks[0] if nsub == 1 else jnp.concatenate(blocks, axis=0)
        st_ref[h] = st * eglast[:, ks] + _mm_tn(vh, khat[:, ks])

        o = o * lax.rsqrt(jnp.mean(o * o, axis=-1, keepdims=True) + RMS_EPS) * ng_ref[:, vs]
        o_ref[:, vs] = o * _silu(gz[:, vs])

    @pl.when(c == pl.num_programs(1) - 1)
    def _():
        for h in range(heads):
            ns_ref[h] = st_ref[h].T


def _hgrn(pc, s0, lb, ng, l, nseq, tlen, chunk):
    heads, kd, vd = s0.shape[2], s0.shape[3], s0.shape[4]
    width = pc.shape[1]
    nch = tlen // chunk
    sub = min(HGRN_SUB, chunk)
    return pl.pallas_call(
        functools.partial(_hgrn_kernel, heads, kd, vd, sub),
        grid=(nseq, nch),
        in_specs=[
            pl.BlockSpec((chunk, width), lambda b, c: (b * nch + c, 0)),
            pl.BlockSpec((None, None, heads, kd, vd), lambda b, c: (l, b, 0, 0, 0)),
            pl.BlockSpec((None, 1, heads * kd), lambda b, c: (l, 0, 0)),
            pl.BlockSpec((None, 1, heads * vd), lambda b, c: (l, 0, 0)),
        ],
        out_specs=[
            pl.BlockSpec((chunk, heads * vd), lambda b, c: (b * nch + c, 0)),
            pl.BlockSpec((None, heads, kd, vd), lambda b, c: (b, 0, 0, 0)),
        ],
        out_shape=[
            jax.ShapeDtypeStruct((nseq * tlen, heads * vd), F32),
            jax.ShapeDtypeStruct((nseq, heads, kd, vd), F32),
        ],
        scratch_shapes=[pltpu.VMEM((heads, vd, kd), F32)],
        compiler_params=_cparams("parallel", "arbitrary"),
        name="hgrn2",
    )(pc, s0, lb, ng)


def _run_trunk(x, s_rwkv, s_shift, s_lru, s_conv, s_hgrn, P):
    nseq, tlen, d = x.shape
    depth = s_rwkv.shape[0]
    x = x.reshape(nseq * tlen, d)
    shift4 = s_shift[:, :, None, :]
    lru4 = s_lru[:, :, None, :]
    rw_chunk = min(RWKV_CHUNK, tlen)
    hg_chunk = min(HGRN_CHUNK, tlen)
    lru_tile = min(LRU_TILE, tlen)
    outs = ([], [], [], [], [])
    for l in range(depth):
        x = _ffn(x, P['ffn1_pre_g'], P['ffn1_wg'], P['ffn1_wu'], P['ffn1_wd'], P['ffn1_post_g'], l)
        pa = _norm_matmul(x, P['mix_pre_g'], P['w_in_a'], l, _tile(P['w_in_a'].shape[-1], 896))
        pb = _norm_matmul(x, P['mix_pre_g'], P['w_in_b'], l, _tile(P['w_in_b'].shape[-1], 1024))
        pc = _norm_matmul(x, P['mix_pre_g'], P['w_in_c'], l, _tile(P['w_in_c'].shape[-1], 1024))
        pg = _norm_matmul(x, P['mix_pre_g'], P['w_in_g'], l, _tile(P['w_in_g'].shape[-1], 1024))
        oa, n_shift, n_rwkv = _rwkv(pa, shift4, s_rwkv, P, l, nseq, tlen, rw_chunk)
        ob, n_conv, n_lru = _lru(pb, s_conv, lru4, P, l, nseq, tlen, lru_tile)
        oc, n_hgrn = _hgrn(pc, s_hgrn, P['hgrn_lb'], P['hgrn_norm_g'], l, nseq, tlen, hg_chunk)
        x = _merge(x, pg, oa, ob, oc, P['proj_a'], P['proj_b'], P['proj_c'], P['w_out'],
                   P['mix_post_g'], l)
        x = _ffn(x, P['ffn2_pre_g'], P['ffn2_wg'], P['ffn2_wu'], P['ffn2_wd'], P['ffn2_post_g'], l)
        for lst, t in zip(outs, (n_rwkv, n_shift[:, 0], n_lru[:, 0], n_conv, n_hgrn)):
            lst.append(t)
    return x.reshape(nseq, tlen, d), tuple(jnp.stack(lst, axis=0) for lst in outs)


def _block_diag(w):
    depth, g, i, j = w.shape
    eye = jnp.eye(g, dtype=w.dtype)
    return jnp.einsum('lgij,gh->lgihj', w, eye).reshape(depth, g * i, g * j)


def _prepare_params(raw, a_proj, b_width, c_kwidth, c_width):
    P = {}
    for n in ('ffn1_wg', 'ffn1_wu', 'ffn1_wd', 'ffn2_wg', 'ffn2_wu', 'ffn2_wd', 'proj_a', 'proj_b',
              'proj_c', 'w_out', 'rwkv_w2', 'rwkv_a2', 'rwkv_g2'):
        P[n] = raw[n].astype(BF16)
    w_in = raw['w_in']
    o1 = a_proj
    o2 = o1 + 2 * b_width
    o3 = o2 + 2 * c_kwidth + 2 * c_width
    P['w_in_a'] = w_in[:, :, :o1].astype(BF16)
    P['w_in_b'] = w_in[:, :, o1:o2].astype(BF16)
    P['w_in_c'] = w_in[:, :, o2:o3].astype(BF16)
    P['w_in_g'] = w_in[:, :, o3:].astype(BF16)
    for n in ('ffn1_pre_g', 'ffn1_post_g', 'mix_pre_g', 'mix_post_g', 'ffn2_pre_g', 'ffn2_post_g',
              'rwkv_mu', 'rwkv_w0', 'rwkv_a0', 'rwkv_k_k', 'rwkv_k_a', 'rwkv_ln_g', 'rwkv_ln_b',
              'lru_conv_b', 'lru_ba', 'lru_bx', 'lru_lam', 'hgrn_norm_g'):
        P[n] = raw[n][:, None, :]
    depth = raw['rwkv_r_k'].shape[0]
    P['rwkv_r_k'] = raw['rwkv_r_k'].reshape(depth, 1, -1)
    P['lru_conv_w'] = raw['lru_conv_w']
    P['lru_wa_bd'] = _block_diag(raw['lru_wa']).astype(BF16)
    P['lru_wx_bd'] = _block_diag(raw['lru_wx']).astype(BF16)
    lb_cum = jnp.cumsum(jax.nn.softmax(raw['hgrn_lb_logits'].astype(F32), axis=0), axis=0)
    P['hgrn_lb'] = (lb_cum - lb_cum[0])[:, None, :]
    return P


def kernel(x_prompt, x_sample, state_rwkv, state_shift, state_lru, state_conv, state_hgrn, ffn1_pre_g, ffn1_post_g, ffn1_wg, ffn1_wu, ffn1_wd, mix_pre_g, mix_post_g, w_in, rwkv_mu, rwkv_w0, rwkv_w2, rwkv_a0, rwkv_a2, rwkv_g2, rwkv_k_k, rwkv_k_a, rwkv_r_k, rwkv_ln_g, rwkv_ln_b, lru_conv_w, lru_conv_b, lru_wa, lru_ba, lru_wx, lru_bx, lru_lam, hgrn_lb_logits, hgrn_norm_g, proj_a, proj_b, proj_c, w_out, ffn2_pre_g, ffn2_post_g, ffn2_wg, ffn2_wu, ffn2_wd):
    raw = dict(
        ffn1_pre_g=ffn1_pre_g, ffn1_post_g=ffn1_post_g, ffn1_wg=ffn1_wg, ffn1_wu=ffn1_wu,
        ffn1_wd=ffn1_wd, mix_pre_g=mix_pre_g, mix_post_g=mix_post_g, w_in=w_in, rwkv_mu=rwkv_mu,
        rwkv_w0=rwkv_w0, rwkv_w2=rwkv_w2, rwkv_a0=rwkv_a0, rwkv_a2=rwkv_a2, rwkv_g2=rwkv_g2,
        rwkv_k_k=rwkv_k_k, rwkv_k_a=rwkv_k_a, rwkv_r_k=rwkv_r_k, rwkv_ln_g=rwkv_ln_g,
        rwkv_ln_b=rwkv_ln_b, lru_conv_w=lru_conv_w, lru_conv_b=lru_conv_b, lru_wa=lru_wa,
        lru_ba=lru_ba, lru_wx=lru_wx, lru_bx=lru_bx, lru_lam=lru_lam,
        hgrn_lb_logits=hgrn_lb_logits, hgrn_norm_g=hgrn_norm_g, proj_a=proj_a, proj_b=proj_b,
        proj_c=proj_c, w_out=w_out, ffn2_pre_g=ffn2_pre_g, ffn2_post_g=ffn2_post_g,
        ffn2_wg=ffn2_wg, ffn2_wu=ffn2_wu, ffn2_wd=ffn2_wd)
    depth, _, heads, hd, _ = state_rwkv.shape
    a_proj = state_shift.shape[-1]
    b_width = state_lru.shape[-1]
    _, _, c_heads, c_kd, c_vd = state_hgrn.shape
    P = _prepare_params(raw, a_proj, b_width, c_heads * c_kd, c_heads * c_vd)
    dt = state_rwkv.dtype
    bp = x_prompt.shape[0]
    y_prompt, p_st = _run_trunk(
        x_prompt,
        jnp.zeros((depth, bp, heads, hd, hd), dt),
        jnp.zeros((depth, bp, a_proj), dt),
        jnp.zeros((depth, bp, b_width), dt),
        jnp.zeros((depth, bp) + state_conv.shape[2:], dt),
        jnp.zeros((depth, bp, c_heads, c_kd, c_vd), dt),
        P)
    y_sample, s_st = _run_trunk(x_sample, state_rwkv, state_shift, state_lru, state_conv,
                                state_hgrn, P)
    return (y_prompt, y_sample) + tuple(t.astype(dt) for t in p_st) + tuple(t.astype(dt) for t in s_st)
```

```python
import functools

import jax
import jax.numpy as jnp
from jax import lax
from jax.experimental import pallas as pl
from jax.experimental.pallas import tpu as pltpu

F32 = jnp.float32
BF16 = jnp.bfloat16
RMS_EPS = 1e-6
GN_EPS = 64e-5
LRU_C = 8.0
KK_EPS = 1e-12

VMEM_LIMIT_BYTES = 48 * 1024 * 1024
SUBLANES = 8
HGRN_SUB = 8
RWKV_CHUNK = 64
HGRN_CHUNK = 64
LRU_TILE = 512


def _cparams(*sem):
    return pltpu.CompilerParams(dimension_semantics=sem, vmem_limit_bytes=VMEM_LIMIT_BYTES)


def _mm(a, b):
    return jnp.dot(a.astype(BF16), b.astype(BF16), preferred_element_type=F32)


def _mm_nt(a, b):
    return lax.dot_general(a.astype(BF16), b.astype(BF16), (((1,), (1,)), ((), ())),
                           preferred_element_type=F32)


def _mm_tn(a, b):
    return lax.dot_general(a.astype(BF16), b.astype(BF16), (((0,), (0,)), ((), ())),
                           preferred_element_type=F32)


def _split3(x):
    hi = x.astype(BF16)
    r1 = x - hi.astype(F32)
    mid = r1.astype(BF16)
    lo = (r1 - mid.astype(F32)).astype(BF16)
    return hi, mid, lo


def _cumsum_rows(x, seg=None):
    c = x.shape[0]
    row = lax.broadcasted_iota(jnp.int32, (c, c), 0)
    col = lax.broadcasted_iota(jnp.int32, (c, c), 1)
    tri = col <= row
    if seg is not None and seg < c:
        sh = seg.bit_length() - 1
        tri = tri & (lax.shift_right_logical(row, sh) == lax.shift_right_logical(col, sh))
    tri = tri.astype(BF16)
    hi, mid, lo = _split3(x)
    out = jnp.dot(tri, lo, preferred_element_type=F32)
    out = out + jnp.dot(tri, mid, preferred_element_type=F32)
    return out + jnp.dot(tri, hi, preferred_element_type=F32)


def _rms(x, g):
    return x * lax.rsqrt(jnp.mean(x * x, axis=-1, keepdims=True) + RMS_EPS) * g


def _softplus(x):
    return jnp.maximum(x, 0.0) + jnp.log1p(jnp.exp(-jnp.abs(x)))


def _sigmoid(x):
    return 1.0 / (1.0 + jnp.exp(-x))


def _silu(x):
    return x * _sigmoid(x)


def _tile(n, pref):
    t = min(n, pref)
    while n % t:
        t //= 2
    return t


def _norm_matmul_kernel(x_ref, g_ref, w_ref, o_ref, xn_ref):
    @pl.when(pl.program_id(1) == 0)
    def _():
        xn_ref[...] = _rms(x_ref[...], g_ref[...]).astype(BF16)

    o_ref[...] = jnp.dot(xn_ref[...], w_ref[...], preferred_element_type=F32)


def _norm_matmul(x, g, w, l, tn):
    m, d = x.shape
    n = w.shape[-1]
    tm = _tile(m, 512)
    return pl.pallas_call(
        _norm_matmul_kernel,
        grid=(m // tm, n // tn),
        in_specs=[
            pl.BlockSpec((tm, d), lambda i, j: (i, 0)),
            pl.BlockSpec((None, 1, d), lambda i, j: (l, 0, 0)),
            pl.BlockSpec((None, d, tn), lambda i, j: (l, 0, j)),
        ],
        out_specs=pl.BlockSpec((tm, tn), lambda i, j: (i, j)),
        out_shape=jax.ShapeDtypeStruct((m, n), F32),
        scratch_shapes=[pltpu.VMEM((tm, d), BF16)],
        compiler_params=_cparams("parallel", "arbitrary"),
        name="norm_matmul",
    )(x, g, w)


def _ffn_kernel(x_ref, gpre_ref, wg_ref, wu_ref, wd_ref, gpost_ref, o_ref, xn_ref, acc_ref):
    f = pl.program_id(1)

    @pl.when(f == 0)
    def _():
        xn_ref[...] = _rms(x_ref[...], gpre_ref[...]).astype(BF16)
        acc_ref[...] = jnp.zeros_like(acc_ref)

    xn = xn_ref[...]
    hg = jnp.dot(xn, wg_ref[...], preferred_element_type=F32)
    hu = jnp.dot(xn, wu_ref[...], preferred_element_type=F32)
    h = _silu(hg) * hu
    acc_ref[...] += jnp.dot(h.astype(BF16), wd_ref[...], preferred_element_type=F32)

    @pl.when(f == pl.num_programs(1) - 1)
    def _():
        o_ref[...] = x_ref[...] + 0.5 * _rms(acc_ref[...], gpost_ref[...])


def _ffn(x, gpre, wg, wu, wd, gpost, l):
    m, d = x.shape
    ff = wg.shape[-1]
    tm = _tile(m, 512)
    tf = ff // 2 if (ff // 2) % 128 == 0 else ff
    return pl.pallas_call(
        _ffn_kernel,
        grid=(m // tm, ff // tf),
        in_specs=[
            pl.BlockSpec((tm, d), lambda i, f: (i, 0)),
            pl.BlockSpec((None, 1, d), lambda i, f: (l, 0, 0)),
            pl.BlockSpec((None, d, tf), lambda i, f: (l, 0, f)),
            pl.BlockSpec((None, d, tf), lambda i, f: (l, 0, f)),
            pl.BlockSpec((None, tf, d), lambda i, f: (l, f, 0)),
            pl.BlockSpec((None, 1, d), lambda i, f: (l, 0, 0)),
        ],
        out_specs=pl.BlockSpec((tm, d), lambda i, f: (i, 0)),
        out_shape=jax.ShapeDtypeStruct((m, d), F32),
        scratch_shapes=[pltpu.VMEM((tm, d), BF16), pltpu.VMEM((tm, d), F32)],
        compiler_params=_cparams("parallel", "arbitrary"),
        name="ffn",
    )(x, gpre, wg, wu, wd, gpost)


def _merge_kernel(x_ref, pg_ref, oa_ref, ob_ref, oc_ref, pa_ref, pb_ref, pc_ref, wo_ref, g_ref,
                  o_ref):
    d = x_ref.shape[-1]
    pg = pg_ref[...]
    merged = _sigmoid(pg[:, :d]) * _mm(oa_ref[...], pa_ref[...])
    merged += _sigmoid(pg[:, d:2 * d]) * _mm(ob_ref[...], pb_ref[...])
    merged += _sigmoid(pg[:, 2 * d:]) * _mm(oc_ref[...], pc_ref[...])
    y = _mm(merged, wo_ref[...])
    o_ref[...] = x_ref[...] + _rms(y, g_ref[...])


def _merge(x, pg, oa, ob, oc, pa, pb, pc, wo, g, l):
    m, d = x.shape
    tm = _tile(m, 512)

    def rows(w):
        return pl.BlockSpec((tm, w), lambda i: (i, 0))

    def whole(a):
        return pl.BlockSpec((None,) + a.shape[1:], lambda i: (l,) + (0,) * (a.ndim - 1))

    return pl.pallas_call(
        _merge_kernel,
        grid=(m // tm,),
        in_specs=[rows(d), rows(3 * d), rows(oa.shape[1]), rows(ob.shape[1]), rows(oc.shape[1]),
                  whole(pa), whole(pb), whole(pc), whole(wo), whole(g)],
        out_specs=rows(d),
        out_shape=jax.ShapeDtypeStruct((m, d), F32),
        compiler_params=_cparams("parallel"),
        name="merge_out",
    )(x, pg, oa, ob, oc, pa, pb, pc, wo, g)


def _rwkv_kernel(heads, hd, dw, da, nb, clen, p_ref, shift_ref, s0_ref, mu_ref, w0_ref, w2_ref,
                 a0_ref, a2_ref, g2_ref, kk_ref, ka_ref, rk_ref, lng_ref, lnb_ref,
                 o_ref, nshift_ref, ns_ref, st_ref, prev_ref, pf_ref):
    c = pl.program_id(1)
    aw = heads * hd
    rows = nb * clen

    @pl.when(c == 0)
    def _():
        prev_ref[...] = shift_ref[...]
        st_ref[...] = s0_ref[...]

    p = p_ref[...]
    pf_ref[...] = pltpu.roll(p, 1, 0)
    for j in range(nb):
        pf_ref[j * clen:j * clen + 1, :] = prev_ref[j:j + 1, :]
        prev_ref[j:j + 1, :] = p_ref[(j + 1) * clen - 1:(j + 1) * clen, :]
    prev = pf_ref[...]
    xs = p + mu_ref[...] * (prev - p)
    r = xs[:, :aw]
    k = xs[:, aw:2 * aw]
    v = xs[:, 2 * aw:3 * aw]
    w1 = xs[:, 3 * aw:3 * aw + dw]
    a1 = xs[:, 3 * aw + dw:3 * aw + dw + da]
    g1 = xs[:, 3 * aw + dw + da:]

    w = -_softplus(-(w0_ref[...] + _mm(jnp.tanh(w1), w2_ref[...]))) - 0.5
    logw = -jnp.exp(w)
    a_icl = _sigmoid(a0_ref[...] + _mm(a1, a2_ref[...]))
    gate = _mm(_sigmoid(g1), g2_ref[...])
    kkraw = k * kk_ref[...]
    k2 = k * (1.0 + (a_icl - 1.0) * ka_ref[...])
    rkk = r * k2 * rk_ref[...]

    ginc = _cumsum_rows(logw, clen)
    glast = [ginc[(j + 1) * clen - 1:(j + 1) * clen, :] for j in range(nb)]
    eg = jnp.exp(ginc)
    egx = jnp.exp(ginc - logw)
    einv = jnp.exp(-ginc)
    ehat = [jnp.exp(glast[j] - ginc[j * clen:(j + 1) * clen, :]) for j in range(nb)]
    eglast = [jnp.exp(x) for x in glast]

    ri = lax.broadcasted_iota(jnp.int32, (2 * clen, clen), 0)
    ci = lax.broadcasted_iota(jnp.int32, (2 * clen, clen), 1)
    mask2 = ci < jnp.where(ri < clen, ri, ri - (clen - 1))
    eye = (lax.broadcasted_iota(jnp.int32, (clen, clen), 0)
           == lax.broadcasted_iota(jnp.int32, (clen, clen), 1)).astype(F32)

    pairs = [(j, h) for j in range(nb) for h in range(heads)]

    def sub(x):
        return [x[j * clen:(j + 1) * clen, h * hd:(h + 1) * hd] for j, h in pairs]

    kkr = sub(kkraw)
    kk = [x / jnp.maximum(jnp.sqrt(jnp.sum(x * x, axis=-1, keepdims=True)), KK_EPS) for x in kkr]
    bvec = [x * a for x, a in zip(kk, sub(a_icl))]
    k2s = sub(k2)
    vs = sub(v)
    einvs = sub(einv)
    ehats = [ehat[j][:, h * hd:(h + 1) * hd] for j, h in pairs]
    eglasts = [eglast[j][:, h * hd:(h + 1) * hd] for j, h in pairs]
    lhs2 = [jnp.concatenate([-x * e1, rr * e2], axis=0)
            for x, e1, rr, e2 in zip(kk, sub(egx), sub(r), sub(eg))]
    bt = [x * e for x, e in zip(bvec, einvs)]
    kt = [x * e for x, e in zip(k2s, einvs)]
    hat2 = [jnp.concatenate([x * e, y * e], axis=0) for x, y, e in zip(bvec, k2s, ehats)]
    s0 = [st_ref[j, h] for j, h in pairs]

    ab2 = [jnp.where(mask2, _mm_nt(x, y), 0.0) for x, y in zip(lhs2, bt)]
    ak2 = [jnp.where(mask2, _mm_nt(x, y), 0.0) for x, y in zip(lhs2, kt)]
    xy = [_mm_nt(x, s) + _mm(a, vv) for x, s, a, vv in zip(lhs2, s0, ak2, vs)]

    pw = [m[:clen] for m in ab2]
    tinv = [eye + m for m in pw]
    span = 2
    while span < clen:
        pw = [_mm(m, m) for m in pw]
        tinv = [t + _mm(t, m) for t, m in zip(tinv, pw)]
        span *= 2

    u = [_mm(t, m[:clen]) for t, m in zip(tinv, xy)]
    y = [m[clen:] + _mm(a[clen:], uu) for m, a, uu in zip(xy, ab2, u)]
    s_new = [s * e + _mm_tn(jnp.concatenate([uu, vv], axis=0), hh)
             for s, e, uu, vv, hh in zip(s0, eglasts, u, vs, hat2)]
    for (j, h), s in zip(pairs, s_new):
        st_ref[j, h] = s

    bonus = [jnp.sum(x, axis=-1, keepdims=True) for x in sub(rkk)]
    gates = sub(gate)
    for i, (j, h) in enumerate(pairs):
        sl = slice(h * hd, (h + 1) * hd)
        yy = y[i]
        yc = yy - jnp.mean(yy, axis=-1, keepdims=True)
        var = jnp.mean(yc * yc, axis=-1, keepdims=True)
        yy = yc * lax.rsqrt(var + GN_EPS) * lng_ref[:, sl] + lnb_ref[:, sl]
        yy = yy + bonus[i] * vs[i]
        o_ref[j * clen:(j + 1) * clen, sl] = yy * gates[i]

    @pl.when(c == pl.num_programs(1) - 1)
    def _():
        nshift_ref[...] = prev_ref[...]
        ns_ref[...] = st_ref[...]


def _rwkv(pa, shift, s0, P, l, nseq, tlen, chunk, nb):
    ap = pa.shape[1]
    depth, _, heads, hd, _ = s0.shape
    aw = heads * hd
    dw = P['rwkv_w2'].shape[1]
    da = P['rwkv_a2'].shape[1]
    nch = tlen // chunk
    assert nb == 1 or nch == 1
    ngrp = nseq // nb
    shift = shift.reshape(depth, ngrp, nb, ap)

    def vec(a):
        return pl.BlockSpec((None,) + a.shape[1:], lambda b, c: (l,) + (0,) * (a.ndim - 1))

    names = ['rwkv_mu', 'rwkv_w0', 'rwkv_w2', 'rwkv_a0', 'rwkv_a2', 'rwkv_g2', 'rwkv_k_k',
             'rwkv_k_a', 'rwkv_r_k', 'rwkv_ln_g', 'rwkv_ln_b']
    params = [P[n] for n in names]
    o, n_shift, n_state = pl.pallas_call(
        functools.partial(_rwkv_kernel, heads, hd, dw, da, nb, chunk),
        grid=(ngrp, nch),
        in_specs=[
            pl.BlockSpec((nb * chunk, ap), lambda b, c: (b * nch + c, 0)),
            pl.BlockSpec((None, None, nb, ap), lambda b, c: (l, b, 0, 0)),
            pl.BlockSpec((None, nb, heads, hd, hd), lambda b, c: (l, b, 0, 0, 0)),
        ] + [vec(a) for a in params],
        out_specs=[
            pl.BlockSpec((nb * chunk, aw), lambda b, c: (b * nch + c, 0)),
            pl.BlockSpec((None, nb, ap), lambda b, c: (b, 0, 0)),
            pl.BlockSpec((nb, heads, hd, hd), lambda b, c: (b, 0, 0, 0)),
        ],
        out_shape=[
            jax.ShapeDtypeStruct((nseq * tlen, aw), F32),
            jax.ShapeDtypeStruct((ngrp, nb, ap), F32),
            jax.ShapeDtypeStruct((nseq, heads, hd, hd), F32),
        ],
        scratch_shapes=[pltpu.VMEM((nb, heads, hd, hd), F32), pltpu.VMEM((nb, ap), F32),
                        pltpu.VMEM((nb * chunk, ap), F32)],
        compiler_params=_cparams("parallel", "arbitrary"),
        name="rwkv7",
    )(pa, shift, s0, *params)
    return o, n_shift.reshape(nseq, ap), n_state


def _lru_kernel(bw, p_ref, cs_ref, h0_ref, cw_ref, cb_ref, wa_ref, ba_ref, wx_ref, bx_ref, lam_ref,
                o_ref, nconv_ref, nh_ref, tail_ref, h_ref):
    c = pl.program_id(1)
    tc = p_ref.shape[0]
    ncv = cs_ref.shape[0]

    @pl.when(c == 0)
    def _():
        tail_ref[...] = jnp.zeros_like(tail_ref)
        tail_ref[SUBLANES - ncv:, :] = cs_ref[...]
        h_ref[...] = h0_ref[...]

    xb = p_ref[:, :bw]
    gb = p_ref[:, bw:]
    rowi = lax.broadcasted_iota(jnp.int32, (tc, 1), 0)
    row8 = lax.broadcasted_iota(jnp.int32, (SUBLANES, 1), 0)
    tail = tail_ref[...]

    xc = cb_ref[...] + xb * cw_ref[ncv:ncv + 1, :]
    for d in range(1, ncv + 1):
        sh = pltpu.roll(xb, d, 0)
        head = jnp.where(row8 >= d, sh[:SUBLANES], pltpu.roll(tail, d, 0))
        if tc > SUBLANES:
            sh = jnp.concatenate([head, sh[SUBLANES:]], axis=0)
        else:
            sh = head
        xc = xc + sh * cw_ref[ncv - d:ncv - d + 1, :]
    tail_ref[...] = xb[tc - SUBLANES:, :]

    rg = _sigmoid(_mm(xc, wa_ref[...]) + ba_ref[...])
    ig = _sigmoid(_mm(xc, wx_ref[...]) + bx_ref[...])
    log_a = -LRU_C * rg * _softplus(-lam_ref[...])
    a = jnp.exp(log_a)
    th = jnp.tanh(log_a)
    b = jnp.sqrt(-2.0 * th / (1.0 - th)) * ig * xc

    d = 1
    while d < tc:
        keep = rowi >= d
        a_sh = jnp.where(keep, pltpu.roll(a, d, 0), 1.0)
        b_sh = jnp.where(keep, pltpu.roll(b, d, 0), 0.0)
        b = a * b_sh + b
        a = a_sh * a
        d *= 2
    h = a * h_ref[...] + b
    h_ref[...] = h[tc - 1:tc, :]
    o_ref[...] = h * jax.nn.gelu(gb)

    @pl.when(c == pl.num_programs(1) - 1)
    def _():
        nconv_ref[...] = xb[tc - ncv:, :]
        nh_ref[...] = h[tc - 1:tc, :]


def _lru(pb, conv, h0, P, l, nseq, tlen, tile):
    bw = h0.shape[-1]
    ncv = conv.shape[2]
    nt = tlen // tile

    def vec(a):
        return pl.BlockSpec((None,) + a.shape[1:], lambda b, c: (l,) + (0,) * (a.ndim - 1))

    names = ['lru_conv_w', 'lru_conv_b', 'lru_wa_bd', 'lru_ba', 'lru_wx_bd', 'lru_bx', 'lru_lam']
    params = [P[n] for n in names]
    return pl.pallas_call(
        functools.partial(_lru_kernel, bw),
        grid=(nseq, nt),
        in_specs=[
            pl.BlockSpec((tile, 2 * bw), lambda b, c: (b * nt + c, 0)),
            pl.BlockSpec((None, None, ncv, bw), lambda b, c: (l, b, 0, 0)),
            pl.BlockSpec((None, None, 1, bw), lambda b, c: (l, b, 0, 0)),
        ] + [vec(a) for a in params],
        out_specs=[
            pl.BlockSpec((tile, bw), lambda b, c: (b * nt + c, 0)),
            pl.BlockSpec((None, ncv, bw), lambda b, c: (b, 0, 0)),
            pl.BlockSpec((None, 1, bw), lambda b, c: (b, 0, 0)),
        ],
        out_shape=[
            jax.ShapeDtypeStruct((nseq * tlen, bw), F32),
            jax.ShapeDtypeStruct((nseq, ncv, bw), F32),
            jax.ShapeDtypeStruct((nseq, 1, bw), F32),
        ],
        scratch_shapes=[pltpu.VMEM((SUBLANES, bw), F32), pltpu.VMEM((1, bw), F32)],
        compiler_params=_cparams("parallel", "arbitrary"),
        name="rglru",
    )(pb, conv, h0, *params)


def _hgrn_kernel(heads, kd, vd, sub, nb, clen, p_ref, s0_ref, lb_ref, ng_ref, o_ref, ns_ref,
                 st_ref):
    c = pl.program_id(1)
    kw = heads * kd
    vw = heads * vd
    pairs = [(j, h) for j in range(nb) for h in range(heads)]

    @pl.when(c == 0)
    def _():
        for j, h in pairs:
            st_ref[j, h] = s0_ref[j, h].T

    p = p_ref[...]
    lb = lb_ref[...]
    q = _silu(p[:, :kw])
    f = lb + (1.0 - lb) * _sigmoid(p[:, kw:2 * kw])
    logf = jnp.log(f)
    kf = 1.0 - f
    v = p[:, 2 * kw:2 * kw + vw]
    gz = p[:, 2 * kw + vw:]

    g = _cumsum_rows(logf, clen)
    qg = q * jnp.exp(g)
    glast = [g[(j + 1) * clen - 1:(j + 1) * clen, :] for j in range(nb)]
    khat = [kf[j * clen:(j + 1) * clen, :] * jnp.exp(glast[j] - g[j * clen:(j + 1) * clen, :])
            for j in range(nb)]
    eglast = [jnp.exp(x) for x in glast]

    nsub = clen // sub
    srow = lax.broadcasted_iota(jnp.int32, (sub, 1), 0)
    lane = lax.broadcasted_iota(jnp.int32, (sub, 128), 1)

    st = [st_ref[j, h] for j, h in pairs]
    o_inter = [_mm_nt(qg[j * clen:(j + 1) * clen, h * kd:(h + 1) * kd], s)
               for (j, h), s in zip(pairs, st)]
    st_new = [s * eglast[j][:, h * kd:(h + 1) * kd]
              + _mm_tn(v[j * clen:(j + 1) * clen, h * vd:(h + 1) * vd],
                       khat[j][:, h * kd:(h + 1) * kd])
              for (j, h), s in zip(pairs, st)]
    for (j, h), s in zip(pairs, st_new):
        st_ref[j, h] = s

    blocks = [(pi, j, h, i) for pi, (j, h) in enumerate(pairs) for i in range(nsub)]
    rel = []
    for pi, j, h, i in blocks:
        if i == 0:
            continue
        ks = slice(h * kd, (h + 1) * kd)
        r0 = j * clen + i * sub
        gref = g[r0 - 1:r0, ks]
        q_rel = q[r0:r0 + sub, ks] * jnp.exp(g[r0:r0 + sub, ks] - gref)
        k_rel = kf[j * clen:r0, ks] * jnp.exp(gref - g[j * clen:r0, ks])
        rel.append(((pi, i), q_rel, k_rel, v[j * clen:r0, h * vd:(h + 1) * vd]))
    att_off = [(key, _mm_nt(q_rel, k_rel), vv) for key, q_rel, k_rel, vv in rel]
    o_off = {key: _mm(att, vv) for key, att, vv in att_off}

    gqk = []
    for pi, j, h, i in blocks:
        ks = slice(h * kd, (h + 1) * kd)
        r0 = j * clen + i * sub
        gqk.append((g[r0:r0 + sub, ks], q[r0:r0 + sub, ks], kf[r0:r0 + sub, ks]))
    att_t = [jnp.zeros((sub, 128), F32) for _ in blocks]
    for t in range(sub):
        cols = []
        for g_i, q_i, k_i in gqk:
            dlt = jnp.where(srow <= t, g_i[t:t + 1, :] - g_i, -jnp.inf)
            cols.append(jnp.sum(q_i[t:t + 1, :] * k_i * jnp.exp(dlt), axis=-1, keepdims=True))
        att_t = [jnp.where(lane == t, col, a) for col, a in zip(cols, att_t)]
    o_diag = [_mm_tn(a[:, :sub],
                     v[j * clen + i * sub:j * clen + (i + 1) * sub, h * vd:(h + 1) * vd])
              for a, (pi, j, h, i) in zip(att_t, blocks)]

    for pi, (j, h) in enumerate(pairs):
        vs = slice(h * vd, (h + 1) * vd)
        parts = []
        for i in range(nsub):
            o_i = o_diag[pi * nsub + i]
            if i > 0:
                o_i = o_i + o_off[(pi, i)]
            parts.append(o_i)
        o = o_inter[pi] + (parts[0] if nsub == 1 else jnp.concatenate(parts, axis=0))
        o = o * lax.rsqrt(jnp.mean(o * o, axis=-1, keepdims=True) + RMS_EPS) * ng_ref[:, vs]
        o_ref[j * clen:(j + 1) * clen, vs] = o * _silu(gz[j * clen:(j + 1) * clen, vs])

    @pl.when(c == pl.num_programs(1) - 1)
    def _():
        for j, h in pairs:
            ns_ref[j, h] = st_ref[j, h].T


def _hgrn(pc, s0, lb, ng, l, nseq, tlen, chunk, nb):
    heads, kd, vd = s0.shape[2], s0.shape[3], s0.shape[4]
    width = pc.shape[1]
    nch = tlen // chunk
    assert nb == 1 or nch == 1
    sub = min(HGRN_SUB, chunk)
    return pl.pallas_call(
        functools.partial(_hgrn_kernel, heads, kd, vd, sub, nb, chunk),
        grid=(nseq // nb, nch),
        in_specs=[
            pl.BlockSpec((nb * chunk, width), lambda b, c: (b * nch + c, 0)),
            pl.BlockSpec((None, nb, heads, kd, vd), lambda b, c: (l, b, 0, 0, 0)),
            pl.BlockSpec((None, 1, heads * kd), lambda b, c: (l, 0, 0)),
            pl.BlockSpec((None, 1, heads * vd), lambda b, c: (l, 0, 0)),
        ],
        out_specs=[
            pl.BlockSpec((nb * chunk, heads * vd), lambda b, c: (b * nch + c, 0)),
            pl.BlockSpec((nb, heads, kd, vd), lambda b, c: (b, 0, 0, 0)),
        ],
        out_shape=[
            jax.ShapeDtypeStruct((nseq * tlen, heads * vd), F32),
            jax.ShapeDtypeStruct((nseq, heads, kd, vd), F32),
        ],
        scratch_shapes=[pltpu.VMEM((nb, heads, vd, kd), F32)],
        compiler_params=_cparams("parallel", "arbitrary"),
        name="hgrn2",
    )(pc, s0, lb, ng)


def _run_trunk(x, s_rwkv, s_shift, s_lru, s_conv, s_hgrn, P):
    nseq, tlen, d = x.shape
    depth = s_rwkv.shape[0]
    x = x.reshape(nseq * tlen, d)
    lru4 = s_lru[:, :, None, :]
    rw_chunk = min(RWKV_CHUNK, tlen)
    rw_nb = _tile(nseq, RWKV_CHUNK // rw_chunk) if rw_chunk == tlen else 1
    hg_chunk = min(HGRN_CHUNK, tlen)
    hg_nb = _tile(nseq, HGRN_CHUNK // hg_chunk) if hg_chunk == tlen else 1
    lru_tile = min(LRU_TILE, tlen)
    outs = ([], [], [], [], [])
    for l in range(depth):
        x = _ffn(x, P['ffn1_pre_g'], P['ffn1_wg'], P['ffn1_wu'], P['ffn1_wd'], P['ffn1_post_g'], l)
        pa = _norm_matmul(x, P['mix_pre_g'], P['w_in_a'], l, _tile(P['w_in_a'].shape[-1], 896))
        pb = _norm_matmul(x, P['mix_pre_g'], P['w_in_b'], l, _tile(P['w_in_b'].shape[-1], 1024))
        pc = _norm_matmul(x, P['mix_pre_g'], P['w_in_c'], l, _tile(P['w_in_c'].shape[-1], 1024))
        pg = _norm_matmul(x, P['mix_pre_g'], P['w_in_g'], l, _tile(P['w_in_g'].shape[-1], 1024))
        oa, n_shift, n_rwkv = _rwkv(pa, s_shift, s_rwkv, P, l, nseq, tlen, rw_chunk, rw_nb)
        ob, n_conv, n_lru = _lru(pb, s_conv, lru4, P, l, nseq, tlen, lru_tile)
        oc, n_hgrn = _hgrn(pc, s_hgrn, P['hgrn_lb'], P['hgrn_norm_g'], l, nseq, tlen, hg_chunk,
                           hg_nb)
        x = _merge(x, pg, oa, ob, oc, P['proj_a'], P['proj_b'], P['proj_c'], P['w_out'],
                   P['mix_post_g'], l)
        x = _ffn(x, P['ffn2_pre_g'], P['ffn2_wg'], P['ffn2_wu'], P['ffn2_wd'], P['ffn2_post_g'], l)
        for lst, t in zip(outs, (n_rwkv, n_shift, n_lru[:, 0], n_conv, n_hgrn)):
            lst.append(t)
    return x.reshape(nseq, tlen, d), tuple(jnp.stack(lst, axis=0) for lst in outs)


def _block_diag(w):
    depth, g, i, j = w.shape
    eye = jnp.eye(g, dtype=w.dtype)
    return jnp.einsum('lgij,gh->lgihj', w, eye).reshape(depth, g * i, g * j)


def _prepare_params(raw, a_proj, b_width, c_kwidth, c_width):
    P = {}
    for n in ('ffn1_wg', 'ffn1_wu', 'ffn1_wd', 'ffn2_wg', 'ffn2_wu', 'ffn2_wd', 'proj_a', 'proj_b',
              'proj_c', 'w_out', 'rwkv_w2', 'rwkv_a2', 'rwkv_g2'):
        P[n] = raw[n].astype(BF16)
    w_in = raw['w_in']
    o1 = a_proj
    o2 = o1 + 2 * b_width
    o3 = o2 + 2 * c_kwidth + 2 * c_width
    P['w_in_a'] = w_in[:, :, :o1].astype(BF16)
    P['w_in_b'] = w_in[:, :, o1:o2].astype(BF16)
    P['w_in_c'] = w_in[:, :, o2:o3].astype(BF16)
    P['w_in_g'] = w_in[:, :, o3:].astype(BF16)
    for n in ('ffn1_pre_g', 'ffn1_post_g', 'mix_pre_g', 'mix_post_g', 'ffn2_pre_g', 'ffn2_post_g',
              'rwkv_mu', 'rwkv_w0', 'rwkv_a0', 'rwkv_k_k', 'rwkv_k_a', 'rwkv_ln_g', 'rwkv_ln_b',
              'lru_conv_b', 'lru_ba', 'lru_bx', 'lru_lam', 'hgrn_norm_g'):
        P[n] = raw[n][:, None, :]
    depth = raw['rwkv_r_k'].shape[0]
    P['rwkv_r_k'] = raw['rwkv_r_k'].reshape(depth, 1, -1)
    P['lru_conv_w'] = raw['lru_conv_w']
    P['lru_wa_bd'] = _block_diag(raw['lru_wa']).astype(BF16)
    P['lru_wx_bd'] = _block_diag(raw['lru_wx']).astype(BF16)
    lb_cum = jnp.cumsum(jax.nn.softmax(raw['hgrn_lb_logits'].astype(F32), axis=0), axis=0)
    P['hgrn_lb'] = (lb_cum - lb_cum[0])[:, None, :]
    return P


def kernel(x_prompt, x_sample, state_rwkv, state_shift, state_lru, state_conv, state_hgrn, ffn1_pre_g, ffn1_post_g, ffn1_wg, ffn1_wu, ffn1_wd, mix_pre_g, mix_post_g, w_in, rwkv_mu, rwkv_w0, rwkv_w2, rwkv_a0, rwkv_a2, rwkv_g2, rwkv_k_k, rwkv_k_a, rwkv_r_k, rwkv_ln_g, rwkv_ln_b, lru_conv_w, lru_conv_b, lru_wa, lru_ba, lru_wx, lru_bx, lru_lam, hgrn_lb_logits, hgrn_norm_g, proj_a, proj_b, proj_c, w_out, ffn2_pre_g, ffn2_post_g, ffn2_wg, ffn2_wu, ffn2_wd):
    raw = dict(
        ffn1_pre_g=ffn1_pre_g, ffn1_post_g=ffn1_post_g, ffn1_wg=ffn1_wg, ffn1_wu=ffn1_wu,
        ffn1_wd=ffn1_wd, mix_pre_g=mix_pre_g, mix_post_g=mix_post_g, w_in=w_in, rwkv_mu=rwkv_mu,
        rwkv_w0=rwkv_w0, rwkv_w2=rwkv_w2, rwkv_a0=rwkv_a0, rwkv_a2=rwkv_a2, rwkv_g2=rwkv_g2,
        rwkv_k_k=rwkv_k_k, rwkv_k_a=rwkv_k_a, rwkv_r_k=rwkv_r_k, rwkv_ln_g=rwkv_ln_g,
        rwkv_ln_b=rwkv_ln_b, lru_conv_w=lru_conv_w, lru_conv_b=lru_conv_b, lru_wa=lru_wa,
        lru_ba=lru_ba, lru_wx=lru_wx, lru_bx=lru_bx, lru_lam=lru_lam,
        hgrn_lb_logits=hgrn_lb_logits, hgrn_norm_g=hgrn_norm_g, proj_a=proj_a, proj_b=proj_b,
        proj_c=proj_c, w_out=w_out, ffn2_pre_g=ffn2_pre_g, ffn2_post_g=ffn2_post_g,
        ffn2_wg=ffn2_wg, ffn2_wu=ffn2_wu, ffn2_wd=ffn2_wd)
    depth, _, heads, hd, _ = state_rwkv.shape
    a_proj = state_shift.shape[-1]
    b_width = state_lru.shape[-1]
    _, _, c_heads, c_kd, c_vd = state_hgrn.shape
    P = _prepare_params(raw, a_proj, b_width, c_heads * c_kd, c_heads * c_vd)
    dt = state_rwkv.dtype
    bp = x_prompt.shape[0]
    y_prompt, p_st = _run_trunk(
        x_prompt,
        jnp.zeros((depth, bp, heads, hd, hd), dt),
        jnp.zeros((depth, bp, a_proj), dt),
        jnp.zeros((depth, bp, b_width), dt),
        jnp.zeros((depth, bp) + state_conv.shape[2:], dt),
        jnp.zeros((depth, bp, c_heads, c_kd, c_vd), dt),
        P)
    y_sample, s_st = _run_trunk(x_sample, state_rwkv, state_shift, state_lru, state_conv,
                                state_hgrn, P)
    return (y_prompt, y_sample) + tuple(t.astype(dt) for t in p_st) + tuple(t.astype(dt) for t in s_st)
```

```python
import functools

import jax
import jax.numpy as jnp
from jax import lax
from jax.experimental import pallas as pl
from jax.experimental.pallas import tpu as pltpu

F32 = jnp.float32
BF16 = jnp.bfloat16
RMS_EPS = 1e-6
GN_EPS = 64e-5
LRU_C = 8.0
KK_EPS = 1e-12

VMEM_LIMIT_BYTES = 48 * 1024 * 1024
SUBLANES = 8
HGRN_SUB = 8
RWKV_CHUNK = 64
HGRN_CHUNK = 64
RWKV_SEQS = 2
HGRN_SEQS = 2
LRU_TILE = 512
IN_PROJ_ROWS = 256
FFN_ROWS = 512
FFN_CHUNK = 256


def _cparams(*sem):
    return pltpu.CompilerParams(dimension_semantics=sem, vmem_limit_bytes=VMEM_LIMIT_BYTES)


def _mm(a, b):
    return jnp.dot(a.astype(BF16), b.astype(BF16), preferred_element_type=F32)


def _mm_nt(a, b):
    return lax.dot_general(a.astype(BF16), b.astype(BF16), (((1,), (1,)), ((), ())),
                           preferred_element_type=F32)


def _mm_tn(a, b):
    return lax.dot_general(a.astype(BF16), b.astype(BF16), (((0,), (0,)), ((), ())),
                           preferred_element_type=F32)


def _split3(x):
    hi = x.astype(BF16)
    r1 = x - hi.astype(F32)
    mid = r1.astype(BF16)
    lo = (r1 - mid.astype(F32)).astype(BF16)
    return hi, mid, lo


def _cumsum_rows(x, seg=None):
    c = x.shape[0]
    row = lax.broadcasted_iota(jnp.int32, (c, c), 0)
    col = lax.broadcasted_iota(jnp.int32, (c, c), 1)
    tri = col <= row
    if seg is not None and seg < c:
        sh = seg.bit_length() - 1
        tri = tri & (lax.shift_right_logical(row, sh) == lax.shift_right_logical(col, sh))
    tri = tri.astype(BF16)
    hi, mid, lo = _split3(x)
    out = jnp.dot(tri, lo, preferred_element_type=F32)
    out = out + jnp.dot(tri, mid, preferred_element_type=F32)
    return out + jnp.dot(tri, hi, preferred_element_type=F32)


def _rms(x, g):
    return x * lax.rsqrt(jnp.mean(x * x, axis=-1, keepdims=True) + RMS_EPS) * g


def _softplus(x):
    return jnp.maximum(x, 0.0) + jnp.log1p(jnp.exp(-jnp.abs(x)))


def _sigmoid(x):
    return 1.0 / (1.0 + jnp.exp(-x))


def _silu(x):
    return x * _sigmoid(x)


def _tile(n, pref):
    t = min(n, pref)
    while n % t:
        t //= 2
    return t


def _in_proj_kernel(sections, x_ref, g_ref, w_ref, *o_refs):
    xn = _rms(x_ref[...], g_ref[...]).astype(BF16)
    start = 0
    for width, o_ref in zip(sections, o_refs):
        o_ref[...] = jnp.dot(xn, w_ref[:, start:start + width], preferred_element_type=F32)
        start += width


def _in_proj(x, g, w, sections, l):
    m, d = x.shape
    tm = _tile(m, IN_PROJ_ROWS)

    def resident(a):
        return pl.BlockSpec((None,) + a.shape[1:], lambda i: (l,) + (0,) * (a.ndim - 1),
                            pipeline_mode=pl.Buffered(1))

    return pl.pallas_call(
        functools.partial(_in_proj_kernel, sections),
        grid=(m // tm,),
        in_specs=[pl.BlockSpec((tm, d), lambda i: (i, 0)), resident(g), resident(w)],
        out_specs=[pl.BlockSpec((tm, n), lambda i: (i, 0)) for n in sections],
        out_shape=[jax.ShapeDtypeStruct((m, n), F32) for n in sections],
        compiler_params=_cparams("parallel"),
        name="in_proj",
    )(x, g, w)


def _ffn_kernel(cw, x_ref, gpre_ref, wg_ref, wu_ref, wd_ref, gpost_ref, o_ref, h_ref):
    ff = wg_ref.shape[-1]
    xn = _rms(x_ref[...], gpre_ref[...]).astype(BF16)

    def gate_up(i):
        cs = slice(i * cw, (i + 1) * cw)
        return (jnp.dot(xn, wg_ref[:, cs], preferred_element_type=F32),
                jnp.dot(xn, wu_ref[:, cs], preferred_element_type=F32))

    nchunk = ff // cw
    cur = gate_up(0)
    for i in range(nchunk):
        nxt = gate_up(i + 1) if i + 1 < nchunk else None
        h_ref[:, i * cw:(i + 1) * cw] = (_silu(cur[0]) * cur[1]).astype(BF16)
        cur = nxt
    y = jnp.dot(h_ref[...], wd_ref[...], preferred_element_type=F32)
    o_ref[...] = x_ref[...] + 0.5 * _rms(y, gpost_ref[...])


def _ffn(x, gpre, wg, wu, wd, gpost, l):
    m, d = x.shape
    ff = wg.shape[-1]
    tm = _tile(m, FFN_ROWS)
    cw = FFN_CHUNK if ff % FFN_CHUNK == 0 else ff

    def resident(a):
        return pl.BlockSpec((None,) + a.shape[1:], lambda i: (l,) + (0,) * (a.ndim - 1),
                            pipeline_mode=pl.Buffered(1))

    return pl.pallas_call(
        functools.partial(_ffn_kernel, cw),
        grid=(m // tm,),
        in_specs=[pl.BlockSpec((tm, d), lambda i: (i, 0)), resident(gpre), resident(wg),
                  resident(wu), resident(wd), resident(gpost)],
        out_specs=pl.BlockSpec((tm, d), lambda i: (i, 0)),
        out_shape=jax.ShapeDtypeStruct((m, d), F32),
        scratch_shapes=[pltpu.VMEM((tm, ff), BF16)],
        compiler_params=_cparams("parallel"),
        name="ffn",
    )(x, gpre, wg, wu, wd, gpost)


def _merge_kernel(x_ref, pg_ref, oa_ref, ob_ref, oc_ref, pa_ref, pb_ref, pc_ref, wo_ref, g_ref,
                  o_ref):
    d = x_ref.shape[-1]
    pg = pg_ref[...]
    merged = _sigmoid(pg[:, :d]) * _mm(oa_ref[...], pa_ref[...])
    merged += _sigmoid(pg[:, d:2 * d]) * _mm(ob_ref[...], pb_ref[...])
    merged += _sigmoid(pg[:, 2 * d:]) * _mm(oc_ref[...], pc_ref[...])
    y = _mm(merged, wo_ref[...])
    o_ref[...] = x_ref[...] + _rms(y, g_ref[...])


def _merge(x, pg, oa, ob, oc, pa, pb, pc, wo, g, l):
    m, d = x.shape
    tm = _tile(m, 512)

    def rows(w):
        return pl.BlockSpec((tm, w), lambda i: (i, 0))

    def whole(a):
        return pl.BlockSpec((None,) + a.shape[1:], lambda i: (l,) + (0,) * (a.ndim - 1),
                            pipeline_mode=pl.Buffered(1))

    return pl.pallas_call(
        _merge_kernel,
        grid=(m // tm,),
        in_specs=[rows(d), rows(3 * d), rows(oa.shape[1]), rows(ob.shape[1]), rows(oc.shape[1]),
                  whole(pa), whole(pb), whole(pc), whole(wo), whole(g)],
        out_specs=rows(d),
        out_shape=jax.ShapeDtypeStruct((m, d), F32),
        compiler_params=_cparams("parallel"),
        name="merge_out",
    )(x, pg, oa, ob, oc, pa, pb, pc, wo, g)


def _rwkv_kernel(heads, hd, dw, da, nb, clen, p_ref, shift_ref, s0_ref, mu_ref, w0_ref, w2_ref,
                 a0_ref, a2_ref, g2_ref, kk_ref, ka_ref, rk_ref, lng_ref, lnb_ref, *rest):
    o_ref, nshift_ref, ns_ref, st_ref, prev_ref, pf_ref = rest[-6:]
    c = pl.program_id(1)
    aw = heads * hd

    @pl.when(c == 0)
    def _():
        prev_ref[...] = shift_ref[...]
        st_ref[...] = s0_ref[...]

    p = p_ref[0] if nb == 1 else jnp.concatenate([p_ref[j] for j in range(nb)], axis=0)
    pf_ref[...] = pltpu.roll(p, 1, 0)
    for j in range(nb):
        pf_ref[j * clen:j * clen + 1, :] = prev_ref[j:j + 1, :]
        prev_ref[j:j + 1, :] = p_ref[j, clen - 1:clen, :]
    prev = pf_ref[...]
    xs = p + mu_ref[...] * (prev - p)
    r = xs[:, :aw]
    k = xs[:, aw:2 * aw]
    v = xs[:, 2 * aw:3 * aw]
    w1 = xs[:, 3 * aw:3 * aw + dw]
    a1 = xs[:, 3 * aw + dw:3 * aw + dw + da]
    g1 = xs[:, 3 * aw + dw + da:]

    w = -_softplus(-(w0_ref[...] + _mm(jnp.tanh(w1), w2_ref[...]))) - 0.5
    logw = -jnp.exp(w)
    a_icl = _sigmoid(a0_ref[...] + _mm(a1, a2_ref[...]))
    gate = _mm(_sigmoid(g1), g2_ref[...])
    kkraw = k * kk_ref[...]
    k2 = k * (1.0 + (a_icl - 1.0) * ka_ref[...])
    rkk = r * k2 * rk_ref[...]

    ginc = _cumsum_rows(logw, clen)
    glast = [ginc[(j + 1) * clen - 1:(j + 1) * clen, :] for j in range(nb)]
    eg = jnp.exp(ginc)
    egx = jnp.exp(ginc - logw)
    einv = jnp.exp(-ginc)
    ehat = [jnp.exp(glast[j] - ginc[j * clen:(j + 1) * clen, :]) for j in range(nb)]
    eglast = [jnp.exp(x) for x in glast]

    ri = lax.broadcasted_iota(jnp.int32, (2 * clen, clen), 0)
    ci = lax.broadcasted_iota(jnp.int32, (2 * clen, clen), 1)
    mask2 = ci < jnp.where(ri < clen, ri, ri - (clen - 1))
    eye = (lax.broadcasted_iota(jnp.int32, (clen, clen), 0)
           == lax.broadcasted_iota(jnp.int32, (clen, clen), 1)).astype(F32)

    pairs = [(j, h) for j in range(nb) for h in range(heads)]

    def sub(x):
        return [x[j * clen:(j + 1) * clen, h * hd:(h + 1) * hd] for j, h in pairs]

    kkr = sub(kkraw)
    kk = [x / jnp.maximum(jnp.sqrt(jnp.sum(x * x, axis=-1, keepdims=True)), KK_EPS) for x in kkr]
    bvec = [x * a for x, a in zip(kk, sub(a_icl))]
    k2s = sub(k2)
    vs = sub(v)
    einvs = sub(einv)
    ehats = [ehat[j][:, h * hd:(h + 1) * hd] for j, h in pairs]
    eglasts = [eglast[j][:, h * hd:(h + 1) * hd] for j, h in pairs]
    lhs2 = [jnp.concatenate([-x * e1, rr * e2], axis=0)
            for x, e1, rr, e2 in zip(kk, sub(egx), sub(r), sub(eg))]
    bt = [x * e for x, e in zip(bvec, einvs)]
    kt = [x * e for x, e in zip(k2s, einvs)]
    hat2 = [jnp.concatenate([x * e, y * e], axis=0) for x, y, e in zip(bvec, k2s, ehats)]
    s0 = [st_ref[j, h] for j, h in pairs]

    ab2 = [jnp.where(mask2, _mm_nt(x, y), 0.0) for x, y in zip(lhs2, bt)]
    ak2 = [jnp.where(mask2, _mm_nt(x, y), 0.0) for x, y in zip(lhs2, kt)]
    xy = [_mm_nt(x, s) + _mm(a, vv) for x, s, a, vv in zip(lhs2, s0, ak2, vs)]

    pw = [m[:clen] for m in ab2]
    tinv = [eye + m for m in pw]
    span = 2
    while span < clen:
        pw = [_mm(m, m) for m in pw]
        tinv = [t + _mm(t, m) for t, m in zip(tinv, pw)]
        span *= 2

    u = [_mm(t, m[:clen]) for t, m in zip(tinv, xy)]
    y = [m[clen:] + _mm(a[clen:], uu) for m, a, uu in zip(xy, ab2, u)]
    s_new = [s * e + _mm_tn(jnp.concatenate([uu, vv], axis=0), hh)
             for s, e, uu, vv, hh in zip(s0, eglasts, u, vs, hat2)]
    for (j, h), s in zip(pairs, s_new):
        st_ref[j, h] = s

    bonus = [jnp.sum(x, axis=-1, keepdims=True) for x in sub(rkk)]
    gates = sub(gate)
    for i, (j, h) in enumerate(pairs):
        sl = slice(h * hd, (h + 1) * hd)
        yy = y[i]
        yc = yy - jnp.mean(yy, axis=-1, keepdims=True)
        var = jnp.mean(yc * yc, axis=-1, keepdims=True)
        yy = yc * lax.rsqrt(var + GN_EPS) * lng_ref[:, sl] + lnb_ref[:, sl]
        yy = yy + bonus[i] * vs[i]
        o_ref[j, :, sl] = yy * gates[i]

    @pl.when(c == pl.num_programs(1) - 1)
    def _():
        nshift_ref[...] = prev_ref[...]
        ns_ref[...] = st_ref[...]


def _rwkv(pa, shift, s0, stacked, P, l, nseq, tlen, chunk, nb):
    ap = pa.shape[1]
    depth, _, heads, hd, _ = s0.shape
    aw = heads * hd
    dw = P['rwkv_w2'].shape[1]
    da = P['rwkv_a2'].shape[1]
    nch = tlen // chunk
    ngrp = nseq // nb
    shift = shift.reshape(depth, ngrp, nb, ap)
    pa = pa.reshape(nseq, tlen, ap)

    def vec(a):
        return pl.BlockSpec((None,) + a.shape[1:], lambda b, c: (l,) + (0,) * (a.ndim - 1))

    names = ['rwkv_mu', 'rwkv_w0', 'rwkv_w2', 'rwkv_a0', 'rwkv_a2', 'rwkv_g2', 'rwkv_k_k',
             'rwkv_k_a', 'rwkv_r_k', 'rwkv_ln_g', 'rwkv_ln_b']
    params = [P[n] for n in names]
    args = [pa, shift, s0] + params
    in_specs = [
        pl.BlockSpec((nb, chunk, ap), lambda b, c: (b, c, 0)),
        pl.BlockSpec((None, None, nb, ap), lambda b, c: (l, b, 0, 0)),
        pl.BlockSpec((None, nb, heads, hd, hd), lambda b, c: (l, b, 0, 0, 0)),
    ] + [vec(a) for a in params]
    aliases = {}
    if stacked is not None:
        aliases = {len(args): 2}
        args.append(stacked)
        in_specs.append(pl.BlockSpec(memory_space=pl.ANY))
    o, n_shift, stacked = pl.pallas_call(
        functools.partial(_rwkv_kernel, heads, hd, dw, da, nb, chunk),
        grid=(ngrp, nch),
        in_specs=in_specs,
        out_specs=[
            pl.BlockSpec((nb, chunk, aw), lambda b, c: (b, c, 0)),
            pl.BlockSpec((None, nb, ap), lambda b, c: (b, 0, 0)),
            pl.BlockSpec((None, nb, heads, hd, hd), lambda b, c: (l, b, 0, 0, 0)),
        ],
        out_shape=[
            jax.ShapeDtypeStruct((nseq, tlen, aw), F32),
            jax.ShapeDtypeStruct((ngrp, nb, ap), F32),
            jax.ShapeDtypeStruct((depth, nseq, heads, hd, hd), F32),
        ],
        scratch_shapes=[pltpu.VMEM((nb, heads, hd, hd), F32), pltpu.VMEM((nb, ap), F32),
                        pltpu.VMEM((nb * chunk, ap), F32)],
        input_output_aliases=aliases,
        compiler_params=_cparams("parallel", "arbitrary"),
        name="rwkv7",
    )(*args)
    return o.reshape(nseq * tlen, aw), n_shift.reshape(nseq, ap), stacked


def _lru_kernel(bw, p_ref, cs_ref, h0_ref, cw_ref, cb_ref, wa_ref, ba_ref, wx_ref, bx_ref, lam_ref,
                o_ref, nconv_ref, nh_ref, tail_ref, h_ref):
    c = pl.program_id(1)
    tc = p_ref.shape[0]
    ncv = cs_ref.shape[0]

    @pl.when(c == 0)
    def _():
        tail_ref[...] = jnp.zeros_like(tail_ref)
        tail_ref[SUBLANES - ncv:, :] = cs_ref[...]
        h_ref[...] = h0_ref[...]

    xb = p_ref[:, :bw]
    gb = p_ref[:, bw:]
    rowi = lax.broadcasted_iota(jnp.int32, (tc, 1), 0)
    row8 = lax.broadcasted_iota(jnp.int32, (SUBLANES, 1), 0)
    tail = tail_ref[...]

    xc = cb_ref[...] + xb * cw_ref[ncv:ncv + 1, :]
    for d in range(1, ncv + 1):
        sh = pltpu.roll(xb, d, 0)
        head = jnp.where(row8 >= d, sh[:SUBLANES], pltpu.roll(tail, d, 0))
        if tc > SUBLANES:
            sh = jnp.concatenate([head, sh[SUBLANES:]], axis=0)
        else:
            sh = head
        xc = xc + sh * cw_ref[ncv - d:ncv - d + 1, :]
    tail_ref[...] = xb[tc - SUBLANES:, :]

    rg = _sigmoid(_mm(xc, wa_ref[...]) + ba_ref[...])
    ig = _sigmoid(_mm(xc, wx_ref[...]) + bx_ref[...])
    log_a = -LRU_C * rg * _softplus(-lam_ref[...])
    a = jnp.exp(log_a)
    th = jnp.tanh(log_a)
    b = jnp.sqrt(-2.0 * th / (1.0 - th)) * ig * xc

    d = 1
    while d < tc:
        keep = rowi >= d
        a_sh = jnp.where(keep, pltpu.roll(a, d, 0), 1.0)
        b_sh = jnp.where(keep, pltpu.roll(b, d, 0), 0.0)
        b = a * b_sh + b
        a = a_sh * a
        d *= 2
    h = a * h_ref[...] + b
    h_ref[...] = h[tc - 1:tc, :]
    o_ref[...] = h * jax.nn.gelu(gb)

    @pl.when(c == pl.num_programs(1) - 1)
    def _():
        nconv_ref[...] = xb[tc - ncv:, :]
        nh_ref[...] = h[tc - 1:tc, :]


def _lru(pb, conv, h0, P, l, nseq, tlen, tile):
    bw = h0.shape[-1]
    ncv = conv.shape[2]
    nt = tlen // tile

    def vec(a):
        return pl.BlockSpec((None,) + a.shape[1:], lambda b, c: (l,) + (0,) * (a.ndim - 1))

    names = ['lru_conv_w', 'lru_conv_b', 'lru_wa_bd', 'lru_ba', 'lru_wx_bd', 'lru_bx', 'lru_lam']
    params = [P[n] for n in names]
    return pl.pallas_call(
        functools.partial(_lru_kernel, bw),
        grid=(nseq, nt),
        in_specs=[
            pl.BlockSpec((tile, 2 * bw), lambda b, c: (b * nt + c, 0)),
            pl.BlockSpec((None, None, ncv, bw), lambda b, c: (l, b, 0, 0)),
            pl.BlockSpec((None, None, 1, bw), lambda b, c: (l, b, 0, 0)),
        ] + [vec(a) for a in params],
        out_specs=[
            pl.BlockSpec((tile, bw), lambda b, c: (b * nt + c, 0)),
            pl.BlockSpec((None, ncv, bw), lambda b, c: (b, 0, 0)),
            pl.BlockSpec((None, 1, bw), lambda b, c: (b, 0, 0)),
        ],
        out_shape=[
            jax.ShapeDtypeStruct((nseq * tlen, bw), F32),
            jax.ShapeDtypeStruct((nseq, ncv, bw), F32),
            jax.ShapeDtypeStruct((nseq, 1, bw), F32),
        ],
        scratch_shapes=[pltpu.VMEM((SUBLANES, bw), F32), pltpu.VMEM((1, bw), F32)],
        compiler_params=_cparams("parallel", "arbitrary"),
        name="rglru",
    )(pb, conv, h0, *params)


def _hgrn_kernel(heads, kd, vd, sub, nb, clen, p_ref, s0_ref, lb_ref, ng_ref, *rest):
    o_ref, ns_ref, st_ref = rest[-3:]
    c = pl.program_id(1)
    kw = heads * kd
    vw = heads * vd
    pairs = [(j, h) for j in range(nb) for h in range(heads)]

    @pl.when(c == 0)
    def _():
        for j, h in pairs:
            st_ref[j, h] = s0_ref[j, h].T

    p = p_ref[0] if nb == 1 else jnp.concatenate([p_ref[j] for j in range(nb)], axis=0)
    lb = lb_ref[...]
    q = _silu(p[:, :kw])
    f = lb + (1.0 - lb) * _sigmoid(p[:, kw:2 * kw])
    logf = jnp.log(f)
    kf = 1.0 - f
    v = p[:, 2 * kw:2 * kw + vw]
    gz = p[:, 2 * kw + vw:]

    g = _cumsum_rows(logf, clen)
    qg = q * jnp.exp(g)
    glast = [g[(j + 1) * clen - 1:(j + 1) * clen, :] for j in range(nb)]
    khat = [kf[j * clen:(j + 1) * clen, :] * jnp.exp(glast[j] - g[j * clen:(j + 1) * clen, :])
            for j in range(nb)]
    eglast = [jnp.exp(x) for x in glast]

    nsub = clen // sub
    srow = lax.broadcasted_iota(jnp.int32, (sub, 1), 0)
    lane = lax.broadcasted_iota(jnp.int32, (sub, 128), 1)

    st = [st_ref[j, h] for j, h in pairs]
    o_inter = [_mm_nt(qg[j * clen:(j + 1) * clen, h * kd:(h + 1) * kd], s)
               for (j, h), s in zip(pairs, st)]
    st_new = [s * eglast[j][:, h * kd:(h + 1) * kd]
              + _mm_tn(v[j * clen:(j + 1) * clen, h * vd:(h + 1) * vd],
                       khat[j][:, h * kd:(h + 1) * kd])
              for (j, h), s in zip(pairs, st)]
    for (j, h), s in zip(pairs, st_new):
        st_ref[j, h] = s

    blocks = [(pi, j, h, i) for pi, (j, h) in enumerate(pairs) for i in range(nsub)]
    rel = []
    for pi, j, h, i in blocks:
        if i == 0:
            continue
        ks = slice(h * kd, (h + 1) * kd)
        r0 = j * clen + i * sub
        gref = g[r0 - 1:r0, ks]
        q_rel = q[r0:r0 + sub, ks] * jnp.exp(g[r0:r0 + sub, ks] - gref)
        k_rel = kf[j * clen:r0, ks] * jnp.exp(gref - g[j * clen:r0, ks])
        rel.append(((pi, i), q_rel, k_rel, v[j * clen:r0, h * vd:(h + 1) * vd]))
    att_off = [(key, _mm_nt(q_rel, k_rel), vv) for key, q_rel, k_rel, vv in rel]
    o_off = {key: _mm(att, vv) for key, att, vv in att_off}

    gqk = []
    for pi, j, h, i in blocks:
        ks = slice(h * kd, (h + 1) * kd)
        r0 = j * clen + i * sub
        gqk.append((g[r0:r0 + sub, ks], q[r0:r0 + sub, ks], kf[r0:r0 + sub, ks]))
    att_t = [jnp.zeros((sub, 128), F32) for _ in blocks]
    for t in range(sub):
        cols = []
        for g_i, q_i, k_i in gqk:
            dlt = jnp.where(srow <= t, g_i[t:t + 1, :] - g_i, -jnp.inf)
            cols.append(jnp.sum(q_i[t:t + 1, :] * k_i * jnp.exp(dlt), axis=-1, keepdims=True))
        att_t = [jnp.where(lane == t, col, a) for col, a in zip(cols, att_t)]
    o_diag = [_mm_tn(a[:, :sub],
                     v[j * clen + i * sub:j * clen + (i + 1) * sub, h * vd:(h + 1) * vd])
              for a, (pi, j, h, i) in zip(att_t, blocks)]

    for pi, (j, h) in enumerate(pairs):
        vs = slice(h * vd, (h + 1) * vd)
        parts = []
        for i in range(nsub):
            o_i = o_diag[pi * nsub + i]
            if i > 0:
                o_i = o_i + o_off[(pi, i)]
            parts.append(o_i)
        o = o_inter[pi] + (parts[0] if nsub == 1 else jnp.concatenate(parts, axis=0))
        o = o * lax.rsqrt(jnp.mean(o * o, axis=-1, keepdims=True) + RMS_EPS) * ng_ref[:, vs]
        o_ref[j, :, vs] = o * _silu(gz[j * clen:(j + 1) * clen, vs])

    @pl.when(c == pl.num_programs(1) - 1)
    def _():
        for j, h in pairs:
            ns_ref[j, h] = st_ref[j, h].T


def _hgrn(pc, s0, stacked, lb, ng, l, nseq, tlen, chunk, nb):
    depth, _, heads, kd, vd = s0.shape
    width = pc.shape[1]
    nch = tlen // chunk
    sub = min(HGRN_SUB, chunk)
    args = [pc.reshape(nseq, tlen, width), s0, lb, ng]
    in_specs = [
        pl.BlockSpec((nb, chunk, width), lambda b, c: (b, c, 0)),
        pl.BlockSpec((None, nb, heads, kd, vd), lambda b, c: (l, b, 0, 0, 0)),
        pl.BlockSpec((None, 1, heads * kd), lambda b, c: (l, 0, 0)),
        pl.BlockSpec((None, 1, heads * vd), lambda b, c: (l, 0, 0)),
    ]
    aliases = {}
    if stacked is not None:
        aliases = {len(args): 1}
        args.append(stacked)
        in_specs.append(pl.BlockSpec(memory_space=pl.ANY))
    o, stacked = pl.pallas_call(
        functools.partial(_hgrn_kernel, heads, kd, vd, sub, nb, chunk),
        grid=(nseq // nb, nch),
        in_specs=in_specs,
        out_specs=[
            pl.BlockSpec((nb, chunk, heads * vd), lambda b, c: (b, c, 0)),
            pl.BlockSpec((None, nb, heads, kd, vd), lambda b, c: (l, b, 0, 0, 0)),
        ],
        out_shape=[
            jax.ShapeDtypeStruct((nseq, tlen, heads * vd), F32),
            jax.ShapeDtypeStruct((depth, nseq, heads, kd, vd), F32),
        ],
        scratch_shapes=[pltpu.VMEM((nb, heads, vd, kd), F32)],
        input_output_aliases=aliases,
        compiler_params=_cparams("parallel", "arbitrary"),
        name="hgrn2",
    )(*args)
    return o.reshape(nseq * tlen, heads * vd), stacked


def _run_trunk(x, s_rwkv, s_shift, s_lru, s_conv, s_hgrn, P):
    nseq, tlen, d = x.shape
    depth = s_rwkv.shape[0]
    x = x.reshape(nseq * tlen, d)
    lru4 = s_lru[:, :, None, :]
    rw_chunk = min(RWKV_CHUNK, tlen)
    rw_nb = _tile(nseq, max(RWKV_SEQS, RWKV_CHUNK // rw_chunk))
    hg_chunk = min(HGRN_CHUNK, tlen)
    hg_nb = _tile(nseq, max(HGRN_SEQS, HGRN_CHUNK // hg_chunk))
    lru_tile = min(LRU_TILE, tlen)
    outs = ([], [], [])
    n_rwkv = n_hgrn = None
    for l in range(depth):
        x = _ffn(x, P['ffn1_pre_g'], P['ffn1_wg'], P['ffn1_wu'], P['ffn1_wd'], P['ffn1_post_g'], l)
        pa, pb, pc, pg = _in_proj(x, P['mix_pre_g'], P['w_in'], P['w_in_sections'], l)
        oa, n_shift, n_rwkv = _rwkv(pa, s_shift, s_rwkv, n_rwkv, P, l, nseq, tlen, rw_chunk, rw_nb)
        ob, n_conv, n_lru = _lru(pb, s_conv, lru4, P, l, nseq, tlen, lru_tile)
        oc, n_hgrn = _hgrn(pc, s_hgrn, n_hgrn, P['hgrn_lb'], P['hgrn_norm_g'], l, nseq, tlen,
                           hg_chunk, hg_nb)
        x = _merge(x, pg, oa, ob, oc, P['proj_a'], P['proj_b'], P['proj_c'], P['w_out'],
                   P['mix_post_g'], l)
        x = _ffn(x, P['ffn2_pre_g'], P['ffn2_wg'], P['ffn2_wu'], P['ffn2_wd'], P['ffn2_post_g'], l)
        for lst, t in zip(outs, (n_shift, n_lru[:, 0], n_conv)):
            lst.append(t)
    n_shift, n_lru, n_conv = (jnp.stack(lst, axis=0) for lst in outs)
    return x.reshape(nseq, tlen, d), (n_rwkv, n_shift, n_lru, n_conv, n_hgrn)


def _block_diag(w):
    depth, g, i, j = w.shape
    eye = jnp.eye(g, dtype=w.dtype)
    return jnp.einsum('lgij,gh->lgihj', w, eye).reshape(depth, g * i, g * j)


def _prepare_params(raw, a_proj, b_width, c_kwidth, c_width):
    P = {}
    for n in ('ffn1_wg', 'ffn1_wu', 'ffn1_wd', 'ffn2_wg', 'ffn2_wu', 'ffn2_wd', 'proj_a', 'proj_b',
              'proj_c', 'w_out', 'w_in', 'rwkv_w2', 'rwkv_a2', 'rwkv_g2'):
        P[n] = raw[n].astype(BF16)
    mixers = (a_proj, 2 * b_width, 2 * c_kwidth + 2 * c_width)
    P['w_in_sections'] = mixers + (raw['w_in'].shape[-1] - sum(mixers),)
    for n in ('ffn1_pre_g', 'ffn1_post_g', 'mix_pre_g', 'mix_post_g', 'ffn2_pre_g', 'ffn2_post_g',
              'rwkv_mu', 'rwkv_w0', 'rwkv_a0', 'rwkv_k_k', 'rwkv_k_a', 'rwkv_ln_g', 'rwkv_ln_b',
              'lru_conv_b', 'lru_ba', 'lru_bx', 'lru_lam', 'hgrn_norm_g'):
        P[n] = raw[n][:, None, :]
    depth = raw['rwkv_r_k'].shape[0]
    P['rwkv_r_k'] = raw['rwkv_r_k'].reshape(depth, 1, -1)
    P['lru_conv_w'] = raw['lru_conv_w']
    P['lru_wa_bd'] = _block_diag(raw['lru_wa']).astype(BF16)
    P['lru_wx_bd'] = _block_diag(raw['lru_wx']).astype(BF16)
    lb_cum = jnp.cumsum(jax.nn.softmax(raw['hgrn_lb_logits'].astype(F32), axis=0), axis=0)
    P['hgrn_lb'] = (lb_cum - lb_cum[0])[:, None, :]
    return P


def kernel(x_prompt, x_sample, state_rwkv, state_shift, state_lru, state_conv, state_hgrn, ffn1_pre_g, ffn1_post_g, ffn1_wg, ffn1_wu, ffn1_wd, mix_pre_g, mix_post_g, w_in, rwkv_mu, rwkv_w0, rwkv_w2, rwkv_a0, rwkv_a2, rwkv_g2, rwkv_k_k, rwkv_k_a, rwkv_r_k, rwkv_ln_g, rwkv_ln_b, lru_conv_w, lru_conv_b, lru_wa, lru_ba, lru_wx, lru_bx, lru_lam, hgrn_lb_logits, hgrn_norm_g, proj_a, proj_b, proj_c, w_out, ffn2_pre_g, ffn2_post_g, ffn2_wg, ffn2_wu, ffn2_wd):
    raw = dict(
        ffn1_pre_g=ffn1_pre_g, ffn1_post_g=ffn1_post_g, ffn1_wg=ffn1_wg, ffn1_wu=ffn1_wu,
        ffn1_wd=ffn1_wd, mix_pre_g=mix_pre_g, mix_post_g=mix_post_g, w_in=w_in, rwkv_mu=rwkv_mu,
        rwkv_w0=rwkv_w0, rwkv_w2=rwkv_w2, rwkv_a0=rwkv_a0, rwkv_a2=rwkv_a2, rwkv_g2=rwkv_g2,
        rwkv_k_k=rwkv_k_k, rwkv_k_a=rwkv_k_a, rwkv_r_k=rwkv_r_k, rwkv_ln_g=rwkv_ln_g,
        rwkv_ln_b=rwkv_ln_b, lru_conv_w=lru_conv_w, lru_conv_b=lru_conv_b, lru_wa=lru_wa,
        lru_ba=lru_ba, lru_wx=lru_wx, lru_bx=lru_bx, lru_lam=lru_lam,
        hgrn_lb_logits=hgrn_lb_logits, hgrn_norm_g=hgrn_norm_g, proj_a=proj_a, proj_b=proj_b,
        proj_c=proj_c, w_out=w_out, ffn2_pre_g=ffn2_pre_g, ffn2_post_g=ffn2_post_g,
        ffn2_wg=ffn2_wg, ffn2_wu=ffn2_wu, ffn2_wd=ffn2_wd)
    depth, _, heads, hd, _ = state_rwkv.shape
    a_proj = state_shift.shape[-1]
    b_width = state_lru.shape[-1]
    _, _, c_heads, c_kd, c_vd = state_hgrn.shape
    P = _prepare_params(raw, a_proj, b_width, c_heads * c_kd, c_heads * c_vd)
    dt = state_rwkv.dtype
    bp = x_prompt.shape[0]
    y_prompt, p_st = _run_trunk(
        x_prompt,
        jnp.zeros((depth, bp, heads, hd, hd), dt),
        jnp.zeros((depth, bp, a_proj), dt),
        jnp.zeros((depth, bp, b_width), dt),
        jnp.zeros((depth, bp) + state_conv.shape[2:], dt),
        jnp.zeros((depth, bp, c_heads, c_kd, c_vd), dt),
        P)
    y_sample, s_st = _run_trunk(x_sample, state_rwkv, state_shift, state_lru, state_conv,
                                state_hgrn, P)
    return (y_prompt, y_sample) + tuple(t.astype(dt) for t in p_st) + tuple(t.astype(dt) for t in s_st)
```

```python
import functools

import jax
import jax.numpy as jnp
from jax import lax
from jax.experimental import pallas as pl
from jax.experimental.pallas import tpu as pltpu

F32 = jnp.float32
BF16 = jnp.bfloat16
RMS_EPS = 1e-6
GN_EPS = 64e-5
LRU_C = 8.0
KK_EPS = 1e-12

VMEM_LIMIT_BYTES = 48 * 1024 * 1024
SUBLANES = 8
HGRN_SUB = 8
RWKV_CHUNK = 64
HGRN_CHUNK = 64
RWKV_SEQS = 2
HGRN_SEQS = 2
LRU_TILE = 512
LRU_SEQS = 16
IN_PROJ_ROWS = 256
FFN_ROWS = 512
FFN_CHUNK = 256


def _cparams(*sem):
    return pltpu.CompilerParams(dimension_semantics=sem, vmem_limit_bytes=VMEM_LIMIT_BYTES)


def _mm(a, b):
    return jnp.dot(a.astype(BF16), b.astype(BF16), preferred_element_type=F32)


def _mm_nt(a, b):
    return lax.dot_general(a.astype(BF16), b.astype(BF16), (((1,), (1,)), ((), ())),
                           preferred_element_type=F32)


def _mm_tn(a, b):
    return lax.dot_general(a.astype(BF16), b.astype(BF16), (((0,), (0,)), ((), ())),
                           preferred_element_type=F32)


def _split3(x):
    hi = x.astype(BF16)
    r1 = x - hi.astype(F32)
    mid = r1.astype(BF16)
    lo = (r1 - mid.astype(F32)).astype(BF16)
    return hi, mid, lo


def _cumsum_rows(x, seg=None):
    c = x.shape[0]
    row = lax.broadcasted_iota(jnp.int32, (c, c), 0)
    col = lax.broadcasted_iota(jnp.int32, (c, c), 1)
    tri = col <= row
    if seg is not None and seg < c:
        sh = seg.bit_length() - 1
        tri = tri & (lax.shift_right_logical(row, sh) == lax.shift_right_logical(col, sh))
    tri = tri.astype(BF16)
    hi, mid, lo = _split3(x)
    out = jnp.dot(tri, lo, preferred_element_type=F32)
    out = out + jnp.dot(tri, mid, preferred_element_type=F32)
    return out + jnp.dot(tri, hi, preferred_element_type=F32)


def _rms(x, g):
    return x * lax.rsqrt(jnp.mean(x * x, axis=-1, keepdims=True) + RMS_EPS) * g


def _softplus(x):
    return jnp.maximum(x, 0.0) + jnp.log1p(jnp.exp(-jnp.abs(x)))


def _sigmoid(x):
    return 1.0 / (1.0 + jnp.exp(-x))


def _silu(x):
    return x * _sigmoid(x)


def _tile(n, pref):
    t = min(n, pref)
    while n % t:
        t //= 2
    return t


def _in_proj_kernel(sections, x_ref, g_ref, w_ref, *o_refs):
    xn = _rms(x_ref[...], g_ref[...]).astype(BF16)
    start = 0
    for width, o_ref in zip(sections, o_refs):
        o_ref[...] = jnp.dot(xn, w_ref[:, start:start + width], preferred_element_type=F32)
        start += width


def _in_proj(x, g, w, sections, l):
    m, d = x.shape
    tm = _tile(m, IN_PROJ_ROWS)

    def resident(a, cols):
        return pl.BlockSpec((None, a.shape[1], cols), lambda i: (l, 0, 0),
                            pipeline_mode=pl.Buffered(1))

    return pl.pallas_call(
        functools.partial(_in_proj_kernel, sections),
        grid=(m // tm,),
        in_specs=[pl.BlockSpec((tm, d), lambda i: (i, 0)), resident(g, d),
                  resident(w, sum(sections))],
        out_specs=[pl.BlockSpec((tm, n), lambda i: (i, 0)) for n in sections],
        out_shape=[jax.ShapeDtypeStruct((m, n), F32) for n in sections],
        compiler_params=_cparams("parallel"),
        name="in_proj",
    )(x, g, w)


def _ffn_kernel(cw, x_ref, gpre_ref, wg_ref, wu_ref, wd_ref, gpost_ref, o_ref, h_ref):
    ff = wg_ref.shape[-1]
    xn = _rms(x_ref[...], gpre_ref[...]).astype(BF16)

    def gate_up(i):
        cs = slice(i * cw, (i + 1) * cw)
        return (jnp.dot(xn, wg_ref[:, cs], preferred_element_type=F32),
                jnp.dot(xn, wu_ref[:, cs], preferred_element_type=F32))

    nchunk = ff // cw
    cur = gate_up(0)
    for i in range(nchunk):
        nxt = gate_up(i + 1) if i + 1 < nchunk else None
        h_ref[:, i * cw:(i + 1) * cw] = (_silu(cur[0]) * cur[1]).astype(BF16)
        cur = nxt
    y = jnp.dot(h_ref[...], wd_ref[...], preferred_element_type=F32)
    o_ref[...] = x_ref[...] + 0.5 * _rms(y, gpost_ref[...])


def _ffn(x, gpre, wg, wu, wd, gpost, l):
    m, d = x.shape
    ff = wg.shape[-1]
    tm = _tile(m, FFN_ROWS)
    cw = FFN_CHUNK if ff % FFN_CHUNK == 0 else ff

    def resident(a):
        return pl.BlockSpec((None,) + a.shape[1:], lambda i: (l,) + (0,) * (a.ndim - 1),
                            pipeline_mode=pl.Buffered(1))

    return pl.pallas_call(
        functools.partial(_ffn_kernel, cw),
        grid=(m // tm,),
        in_specs=[pl.BlockSpec((tm, d), lambda i: (i, 0)), resident(gpre), resident(wg),
                  resident(wu), resident(wd), resident(gpost)],
        out_specs=pl.BlockSpec((tm, d), lambda i: (i, 0)),
        out_shape=jax.ShapeDtypeStruct((m, d), F32),
        scratch_shapes=[pltpu.VMEM((tm, ff), BF16)],
        compiler_params=_cparams("parallel"),
        name="ffn",
    )(x, gpre, wg, wu, wd, gpost)


def _merge_kernel(goff, x_ref, gpre_ref, win_ref, oa_ref, ob_ref, oc_ref, pa_ref, pb_ref, pc_ref,
                  wo_ref, g_ref, o_ref):
    d = x_ref.shape[-1]
    x = x_ref[...]
    xn = _rms(x, gpre_ref[...]).astype(BF16)
    merged = None
    for i, (o_b, p_b) in enumerate(((oa_ref, pa_ref), (ob_ref, pb_ref), (oc_ref, pc_ref))):
        gate = jnp.dot(xn, win_ref[:, goff + i * d:goff + (i + 1) * d], preferred_element_type=F32)
        term = _sigmoid(gate) * _mm(o_b[...], p_b[...])
        merged = term if merged is None else merged + term
    y = _mm(merged, wo_ref[...])
    o_ref[...] = x + _rms(y, g_ref[...])


def _merge(x, gpre, w_in, goff, oa, ob, oc, pa, pb, pc, wo, g, l):
    m, d = x.shape
    tm = _tile(m, 512)

    def rows(w):
        return pl.BlockSpec((tm, w), lambda i: (i, 0))

    def whole(a):
        return pl.BlockSpec((None,) + a.shape[1:], lambda i: (l,) + (0,) * (a.ndim - 1),
                            pipeline_mode=pl.Buffered(1))

    return pl.pallas_call(
        functools.partial(_merge_kernel, goff),
        grid=(m // tm,),
        in_specs=[rows(d), whole(gpre), whole(w_in), rows(oa.shape[1]), rows(ob.shape[1]),
                  rows(oc.shape[1]), whole(pa), whole(pb), whole(pc), whole(wo), whole(g)],
        out_specs=rows(d),
        out_shape=jax.ShapeDtypeStruct((m, d), F32),
        compiler_params=_cparams("parallel"),
        name="merge_out",
    )(x, gpre, w_in, oa, ob, oc, pa, pb, pc, wo, g)


def _rwkv_kernel(heads, hd, dw, da, nb, clen, p_ref, shift_ref, s0_ref, mu_ref, w0_ref, w2_ref,
                 a0_ref, a2_ref, g2_ref, kk_ref, ka_ref, rk_ref, lng_ref, lnb_ref, *rest):
    o_ref, nshift_ref, ns_ref, st_ref, prev_ref, pf_ref = rest[-6:]
    c = pl.program_id(1)
    aw = heads * hd

    @pl.when(c == 0)
    def _():
        prev_ref[...] = shift_ref[...]
        st_ref[...] = s0_ref[...]

    p = p_ref[0] if nb == 1 else jnp.concatenate([p_ref[j] for j in range(nb)], axis=0)
    pf_ref[...] = pltpu.roll(p, 1, 0)
    for j in range(nb):
        pf_ref[j * clen:j * clen + 1, :] = prev_ref[j:j + 1, :]
        prev_ref[j:j + 1, :] = p_ref[j, clen - 1:clen, :]
    prev = pf_ref[...]
    xs = p + mu_ref[...] * (prev - p)
    r = xs[:, :aw]
    k = xs[:, aw:2 * aw]
    v = xs[:, 2 * aw:3 * aw]
    w1 = xs[:, 3 * aw:3 * aw + dw]
    a1 = xs[:, 3 * aw + dw:3 * aw + dw + da]
    g1 = xs[:, 3 * aw + dw + da:]

    w = -_softplus(-(w0_ref[...] + _mm(jnp.tanh(w1), w2_ref[...]))) - 0.5
    logw = -jnp.exp(w)
    a_icl = _sigmoid(a0_ref[...] + _mm(a1, a2_ref[...]))
    gate = _mm(_sigmoid(g1), g2_ref[...])
    kkraw = k * kk_ref[...]
    k2 = k * (1.0 + (a_icl - 1.0) * ka_ref[...])
    rkk = r * k2 * rk_ref[...]

    ginc = _cumsum_rows(logw, clen)
    glast = [ginc[(j + 1) * clen - 1:(j + 1) * clen, :] for j in range(nb)]
    eg = jnp.exp(ginc)
    egx = jnp.exp(ginc - logw)
    einv = jnp.exp(-ginc)
    ehat = [jnp.exp(glast[j] - ginc[j * clen:(j + 1) * clen, :]) for j in range(nb)]
    eglast = [jnp.exp(x) for x in glast]

    ri = lax.broadcasted_iota(jnp.int32, (2 * clen, clen), 0)
    ci = lax.broadcasted_iota(jnp.int32, (2 * clen, clen), 1)
    mask2 = ci < jnp.where(ri < clen, ri, ri - (clen - 1))
    eye = (lax.broadcasted_iota(jnp.int32, (clen, clen), 0)
           == lax.broadcasted_iota(jnp.int32, (clen, clen), 1)).astype(F32)

    pairs = [(j, h) for j in range(nb) for h in range(heads)]

    def sub(x):
        return [x[j * clen:(j + 1) * clen, h * hd:(h + 1) * hd] for j, h in pairs]

    kkr = sub(kkraw)
    kk = [x * lax.rsqrt(jnp.maximum(jnp.sum(x * x, axis=-1, keepdims=True), KK_EPS * KK_EPS))
          for x in kkr]
    bvec = [x * a for x, a in zip(kk, sub(a_icl))]
    k2s = sub(k2)
    vs = sub(v)
    einvs = sub(einv)
    ehats = [ehat[j][:, h * hd:(h + 1) * hd] for j, h in pairs]
    eglasts = [eglast[j][:, h * hd:(h + 1) * hd] for j, h in pairs]
    lhs2 = [jnp.concatenate([-x * e1, rr * e2], axis=0)
            for x, e1, rr, e2 in zip(kk, sub(egx), sub(r), sub(eg))]
    bt = [x * e for x, e in zip(bvec, einvs)]
    kt = [x * e for x, e in zip(k2s, einvs)]
    hat2 = [jnp.concatenate([x * e, y * e], axis=0) for x, y, e in zip(bvec, k2s, ehats)]
    s0 = [st_ref[j, h] for j, h in pairs]

    ab2 = [jnp.where(mask2, _mm_nt(x, y), 0.0) for x, y in zip(lhs2, bt)]
    ak2 = [jnp.where(mask2, _mm_nt(x, y), 0.0) for x, y in zip(lhs2, kt)]
    xy = [_mm_nt(x, s) + _mm(a, vv) for x, s, a, vv in zip(lhs2, s0, ak2, vs)]

    pw = [m[:clen] for m in ab2]
    tinv = [eye + m for m in pw]
    span = 2
    while span < clen:
        pw = [_mm(m, m) for m in pw]
        tinv = [t + _mm(t, m) for t, m in zip(tinv, pw)]
        span *= 2

    u = [_mm(t, m[:clen]) for t, m in zip(tinv, xy)]
    y = [m[clen:] + _mm(a[clen:], uu) for m, a, uu in zip(xy, ab2, u)]
    s_new = [s * e + _mm_tn(jnp.concatenate([uu, vv], axis=0), hh)
             for s, e, uu, vv, hh in zip(s0, eglasts, u, vs, hat2)]
    for (j, h), s in zip(pairs, s_new):
        st_ref[j, h] = s

    bonus = [jnp.sum(x, axis=-1, keepdims=True) for x in sub(rkk)]
    gates = sub(gate)
    for i, (j, h) in enumerate(pairs):
        sl = slice(h * hd, (h + 1) * hd)
        yy = y[i]
        yc = yy - jnp.mean(yy, axis=-1, keepdims=True)
        var = jnp.mean(yc * yc, axis=-1, keepdims=True)
        yy = yc * lax.rsqrt(var + GN_EPS) * lng_ref[:, sl] + lnb_ref[:, sl]
        yy = yy + bonus[i] * vs[i]
        o_ref[j, :, sl] = yy * gates[i]

    @pl.when(c == pl.num_programs(1) - 1)
    def _():
        nshift_ref[...] = prev_ref[...]
        ns_ref[...] = st_ref[...]


def _rwkv(pa, shift, s0, stacked, P, l, nseq, tlen, chunk, nb):
    ap = pa.shape[1]
    depth, _, heads, hd, _ = s0.shape
    aw = heads * hd
    dw = P['rwkv_w2'].shape[1]
    da = P['rwkv_a2'].shape[1]
    nch = tlen // chunk
    ngrp = nseq // nb
    shift = shift.reshape(depth, ngrp, nb, ap)
    pa = pa.reshape(nseq, tlen, ap)

    def vec(a):
        return pl.BlockSpec((None,) + a.shape[1:], lambda b, c: (l,) + (0,) * (a.ndim - 1))

    names = ['rwkv_mu', 'rwkv_w0', 'rwkv_w2', 'rwkv_a0', 'rwkv_a2', 'rwkv_g2', 'rwkv_k_k',
             'rwkv_k_a', 'rwkv_r_k', 'rwkv_ln_g', 'rwkv_ln_b']
    params = [P[n] for n in names]
    args = [pa, shift, s0] + params
    in_specs = [
        pl.BlockSpec((nb, chunk, ap), lambda b, c: (b, c, 0)),
        pl.BlockSpec((None, None, nb, ap), lambda b, c: (l, b, 0, 0)),
        pl.BlockSpec((None, nb, heads, hd, hd), lambda b, c: (l, b, 0, 0, 0)),
    ] + [vec(a) for a in params]
    aliases = {}
    if stacked is not None:
        aliases = {len(args): 2}
        args.append(stacked)
        in_specs.append(pl.BlockSpec(memory_space=pl.ANY))
    o, n_shift, stacked = pl.pallas_call(
        functools.partial(_rwkv_kernel, heads, hd, dw, da, nb, chunk),
        grid=(ngrp, nch),
        in_specs=in_specs,
        out_specs=[
            pl.BlockSpec((nb, chunk, aw), lambda b, c: (b, c, 0)),
            pl.BlockSpec((None, nb, ap), lambda b, c: (b, 0, 0)),
            pl.BlockSpec((None, nb, heads, hd, hd), lambda b, c: (l, b, 0, 0, 0)),
        ],
        out_shape=[
            jax.ShapeDtypeStruct((nseq, tlen, aw), F32),
            jax.ShapeDtypeStruct((ngrp, nb, ap), F32),
            jax.ShapeDtypeStruct((depth, nseq, heads, hd, hd), F32),
        ],
        scratch_shapes=[pltpu.VMEM((nb, heads, hd, hd), F32), pltpu.VMEM((nb, ap), F32),
                        pltpu.VMEM((nb * chunk, ap), F32)],
        input_output_aliases=aliases,
        compiler_params=_cparams("parallel", "arbitrary"),
        name="rwkv7",
    )(*args)
    return o.reshape(nseq * tlen, aw), n_shift.reshape(nseq, ap), stacked


def _lru_kernel(bw, seqs, p_ref, cs_ref, h0_ref, cw_ref, cb_ref, wa_ref, ba_ref, wx_ref, bx_ref,
                lam_ref, o_ref, nconv_ref, nh_ref, tail_ref, h_ref):
    c = pl.program_id(1)
    rows = p_ref.shape[0]
    ncv = cs_ref.shape[1]
    nseg = rows // SUBLANES

    if seqs == 1:
        @pl.when(c == 0)
        def _():
            tail_ref[...] = jnp.zeros_like(tail_ref)
            tail_ref[SUBLANES - ncv:, :] = cs_ref[0]
            h_ref[...] = h0_ref[0]
    else:
        tail_ref[...] = jnp.zeros_like(tail_ref)
        for j in range(seqs):
            tail_ref[(j + 1) * SUBLANES - ncv:(j + 1) * SUBLANES, :] = cs_ref[j]

    xb = p_ref[:, :bw]
    gb = p_ref[:, bw:]
    rowi = lax.broadcasted_iota(jnp.int32, (rows, 1), 0)
    pos = rowi & (SUBLANES - 1)

    def seg_roll(x, d):
        return pltpu.roll(x.reshape(nseg, SUBLANES, bw), d, 1).reshape(rows, bw)

    if seqs == 1 and rows > SUBLANES:
        before = jnp.concatenate([tail_ref[...], xb[:rows - SUBLANES, :]], axis=0)
    else:
        before = tail_ref[...]
    xc = cb_ref[...] + xb * cw_ref[ncv:ncv + 1, :]
    for d in range(1, ncv + 1):
        sh = jnp.where(pos >= d, seg_roll(xb, d), seg_roll(before, d))
        xc = xc + sh * cw_ref[ncv - d:ncv - d + 1, :]
    if seqs == 1:
        tail_ref[...] = xb[rows - SUBLANES:, :]

    rg = _sigmoid(_mm(xc, wa_ref[...]) + ba_ref[...])
    ig = _sigmoid(_mm(xc, wx_ref[...]) + bx_ref[...])
    log_a = -LRU_C * rg * _softplus(-lam_ref[...])
    a = jnp.exp(log_a)
    b = jnp.sqrt(-jnp.tanh(log_a) * (1.0 + a * a)) * ig * xc

    d = 1
    while d < SUBLANES:
        keep = pos >= d
        a_sh = jnp.where(keep, seg_roll(a, d), 1.0)
        b_sh = jnp.where(keep, seg_roll(b, d), 0.0)
        b = a * b_sh + b
        a = a_sh * a
        d *= 2
    pieces = []
    hc = h_ref[...] if seqs == 1 else None
    for s in range(nseg):
        rs = slice(s * SUBLANES, (s + 1) * SUBLANES)
        h_s = a[rs] * (hc if seqs == 1 else h0_ref[s]) + b[rs]
        hc = h_s[SUBLANES - 1:, :]
        pieces.append(h_s)
    h = pieces[0] if nseg == 1 else jnp.concatenate(pieces, axis=0)
    o_ref[...] = h * jax.nn.gelu(gb)

    if seqs == 1:
        h_ref[...] = hc

        @pl.when(c == pl.num_programs(1) - 1)
        def _():
            nconv_ref[0] = xb[rows - ncv:, :]
            nh_ref[0] = hc
    else:
        for s in range(seqs):
            nconv_ref[s] = xb[(s + 1) * SUBLANES - ncv:(s + 1) * SUBLANES, :]
            nh_ref[s] = pieces[s][SUBLANES - 1:, :]


def _lru(pb, conv, h0, P, l, nseq, tlen, tile, seqs):
    bw = h0.shape[-1]
    ncv = conv.shape[2]
    nt = tlen // tile
    assert seqs == 1 or (nt == 1 and tile == SUBLANES)
    rows = seqs * tile

    def vec(a):
        return pl.BlockSpec((None,) + a.shape[1:], lambda b, c: (l,) + (0,) * (a.ndim - 1))

    names = ['lru_conv_w', 'lru_conv_b', 'lru_wa_bd', 'lru_ba', 'lru_wx_bd', 'lru_bx', 'lru_lam']
    params = [P[n] for n in names]
    return pl.pallas_call(
        functools.partial(_lru_kernel, bw, seqs),
        grid=(nseq // seqs, nt),
        in_specs=[
            pl.BlockSpec((rows, 2 * bw), lambda b, c: (b * nt + c, 0)),
            pl.BlockSpec((None, seqs, ncv, bw), lambda b, c: (l, b, 0, 0)),
            pl.BlockSpec((None, seqs, 1, bw), lambda b, c: (l, b, 0, 0)),
        ] + [vec(a) for a in params],
        out_specs=[
            pl.BlockSpec((rows, bw), lambda b, c: (b * nt + c, 0)),
            pl.BlockSpec((seqs, ncv, bw), lambda b, c: (b, 0, 0)),
            pl.BlockSpec((seqs, 1, bw), lambda b, c: (b, 0, 0)),
        ],
        out_shape=[
            jax.ShapeDtypeStruct((nseq * tlen, bw), F32),
            jax.ShapeDtypeStruct((nseq, ncv, bw), F32),
            jax.ShapeDtypeStruct((nseq, 1, bw), F32),
        ],
        scratch_shapes=[pltpu.VMEM((SUBLANES if seqs == 1 else rows, bw), F32),
                        pltpu.VMEM((1, bw), F32)],
        compiler_params=_cparams("parallel", "arbitrary"),
        name="rglru",
    )(pb, conv, h0, *params)


def _hgrn_kernel(heads, kd, vd, sub, nb, clen, p_ref, s0_ref, lb_ref, ng_ref, *rest):
    o_ref, ns_ref, st_ref = rest[-3:]
    c = pl.program_id(1)
    kw = heads * kd
    vw = heads * vd
    pairs = [(j, h) for j in range(nb) for h in range(heads)]

    @pl.when(c == 0)
    def _():
        for j, h in pairs:
            st_ref[j, h] = s0_ref[j, h].T

    p = p_ref[0] if nb == 1 else jnp.concatenate([p_ref[j] for j in range(nb)], axis=0)
    lb = lb_ref[...]
    q = _silu(p[:, :kw])
    f = lb + (1.0 - lb) * _sigmoid(p[:, kw:2 * kw])
    logf = jnp.log(f)
    kf = 1.0 - f
    v = p[:, 2 * kw:2 * kw + vw]
    gz = p[:, 2 * kw + vw:]

    g = _cumsum_rows(logf, clen)
    qg = q * jnp.exp(g)
    glast = [g[(j + 1) * clen - 1:(j + 1) * clen, :] for j in range(nb)]
    khat = [kf[j * clen:(j + 1) * clen, :] * jnp.exp(glast[j] - g[j * clen:(j + 1) * clen, :])
            for j in range(nb)]
    eglast = [jnp.exp(x) for x in glast]

    nsub = clen // sub
    srow = lax.broadcasted_iota(jnp.int32, (sub, 1), 0)
    lane = lax.broadcasted_iota(jnp.int32, (sub, 128), 1)

    st = [st_ref[j, h] for j, h in pairs]
    o_inter = [_mm_nt(qg[j * clen:(j + 1) * clen, h * kd:(h + 1) * kd], s)
               for (j, h), s in zip(pairs, st)]
    st_new = [s * eglast[j][:, h * kd:(h + 1) * kd]
              + _mm_tn(v[j * clen:(j + 1) * clen, h * vd:(h + 1) * vd],
                       khat[j][:, h * kd:(h + 1) * kd])
              for (j, h), s in zip(pairs, st)]
    for (j, h), s in zip(pairs, st_new):
        st_ref[j, h] = s

    blocks = [(pi, j, h, i) for pi, (j, h) in enumerate(pairs) for i in range(nsub)]
    rel = []
    for pi, j, h, i in blocks:
        if i == 0:
            continue
        ks = slice(h * kd, (h + 1) * kd)
        r0 = j * clen + i * sub
        gref = g[r0 - 1:r0, ks]
        q_rel = q[r0:r0 + sub, ks] * jnp.exp(g[r0:r0 + sub, ks] - gref)
        k_rel = kf[j * clen:r0, ks] * jnp.exp(gref - g[j * clen:r0, ks])
        rel.append(((pi, i), q_rel, k_rel, v[j * clen:r0, h * vd:(h + 1) * vd]))
    att_off = [(key, _mm_nt(q_rel, k_rel), vv) for key, q_rel, k_rel, vv in rel]
    o_off = {key: _mm(att, vv) for key, att, vv in att_off}

    fqk = []
    for pi, j, h, i in blocks:
        ks = slice(h * kd, (h + 1) * kd)
        r0 = j * clen + i * sub
        fqk.append((f[r0:r0 + sub, ks], q[r0:r0 + sub, ks], kf[r0:r0 + sub, ks]))
    att_t = [jnp.zeros((sub, 128), F32) for _ in blocks]
    dec = [jnp.zeros((sub, kd), F32) for _ in blocks]
    for t in range(sub):
        dec = [jnp.where(srow == t, k_i, d * f_i[t:t + 1, :]) for d, (f_i, q_i, k_i) in zip(dec, fqk)]
        cols = [jnp.sum(q_i[t:t + 1, :] * d, axis=-1, keepdims=True)
                for d, (f_i, q_i, k_i) in zip(dec, fqk)]
        att_t = [jnp.where(lane == t, col, a) for col, a in zip(cols, att_t)]
    o_diag = [_mm_tn(a[:, :sub],
                     v[j * clen + i * sub:j * clen + (i + 1) * sub, h * vd:(h + 1) * vd])
              for a, (pi, j, h, i) in zip(att_t, blocks)]

    for pi, (j, h) in enumerate(pairs):
        vs = slice(h * vd, (h + 1) * vd)
        parts = []
        for i in range(nsub):
            o_i = o_diag[pi * nsub + i]
            if i > 0:
                o_i = o_i + o_off[(pi, i)]
            parts.append(o_i)
        o = o_inter[pi] + (parts[0] if nsub == 1 else jnp.concatenate(parts, axis=0))
        o = o * lax.rsqrt(jnp.mean(o * o, axis=-1, keepdims=True) + RMS_EPS) * ng_ref[:, vs]
        o_ref[j, :, vs] = o * _silu(gz[j * clen:(j + 1) * clen, vs])

    @pl.when(c == pl.num_programs(1) - 1)
    def _():
        for j, h in pairs:
            ns_ref[j, h] = st_ref[j, h].T


def _hgrn(pc, s0, stacked, lb, ng, l, nseq, tlen, chunk, nb):
    depth, _, heads, kd, vd = s0.shape
    width = pc.shape[1]
    nch = tlen // chunk
    sub = min(HGRN_SUB, chunk)
    args = [pc.reshape(nseq, tlen, width), s0, lb, ng]
    in_specs = [
        pl.BlockSpec((nb, chunk, width), lambda b, c: (b, c, 0)),
        pl.BlockSpec((None, nb, heads, kd, vd), lambda b, c: (l, b, 0, 0, 0)),
        pl.BlockSpec((None, 1, heads * kd), lambda b, c: (l, 0, 0)),
        pl.BlockSpec((None, 1, heads * vd), lambda b, c: (l, 0, 0)),
    ]
    aliases = {}
    if stacked is not None:
        aliases = {len(args): 1}
        args.append(stacked)
        in_specs.append(pl.BlockSpec(memory_space=pl.ANY))
    o, stacked = pl.pallas_call(
        functools.partial(_hgrn_kernel, heads, kd, vd, sub, nb, chunk),
        grid=(nseq // nb, nch),
        in_specs=in_specs,
        out_specs=[
            pl.BlockSpec((nb, chunk, heads * vd), lambda b, c: (b, c, 0)),
            pl.BlockSpec((None, nb, heads, kd, vd), lambda b, c: (l, b, 0, 0, 0)),
        ],
        out_shape=[
            jax.ShapeDtypeStruct((nseq, tlen, heads * vd), F32),
            jax.ShapeDtypeStruct((depth, nseq, heads, kd, vd), F32),
        ],
        scratch_shapes=[pltpu.VMEM((nb, heads, vd, kd), F32)],
        input_output_aliases=aliases,
        compiler_params=_cparams("parallel", "arbitrary"),
        name="hgrn2",
    )(*args)
    return o.reshape(nseq * tlen, heads * vd), stacked


def _run_trunk(x, s_rwkv, s_shift, s_lru, s_conv, s_hgrn, P):
    nseq, tlen, d = x.shape
    depth = s_rwkv.shape[0]
    x = x.reshape(nseq * tlen, d)
    lru4 = s_lru[:, :, None, :]
    rw_chunk = min(RWKV_CHUNK, tlen)
    rw_nb = _tile(nseq, max(RWKV_SEQS, RWKV_CHUNK // rw_chunk))
    hg_chunk = min(HGRN_CHUNK, tlen)
    hg_nb = _tile(nseq, max(HGRN_SEQS, HGRN_CHUNK // hg_chunk))
    lru_tile = min(LRU_TILE, tlen)
    lru_seqs = _tile(nseq, LRU_SEQS) if lru_tile == tlen == SUBLANES else 1
    outs = ([], [], [])
    n_rwkv = n_hgrn = None
    for l in range(depth):
        x = _ffn(x, P['ffn1_pre_g'], P['ffn1_wg'], P['ffn1_wu'], P['ffn1_wd'], P['ffn1_post_g'], l)
        pa, pb, pc = _in_proj(x, P['mix_pre_g'], P['w_in'], P['w_in_sections'], l)
        oa, n_shift, n_rwkv = _rwkv(pa, s_shift, s_rwkv, n_rwkv, P, l, nseq, tlen, rw_chunk, rw_nb)
        ob, n_conv, n_lru = _lru(pb, s_conv, lru4, P, l, nseq, tlen, lru_tile, lru_seqs)
        oc, n_hgrn = _hgrn(pc, s_hgrn, n_hgrn, P['hgrn_lb'], P['hgrn_norm_g'], l, nseq, tlen,
                           hg_chunk, hg_nb)
        x = _merge(x, P['mix_pre_g'], P['w_in'], sum(P['w_in_sections']), oa, ob, oc, P['proj_a'],
                   P['proj_b'], P['proj_c'], P['w_out'], P['mix_post_g'], l)
        x = _ffn(x, P['ffn2_pre_g'], P['ffn2_wg'], P['ffn2_wu'], P['ffn2_wd'], P['ffn2_post_g'], l)
        for lst, t in zip(outs, (n_shift, n_lru[:, 0], n_conv)):
            lst.append(t)
    n_shift, n_lru, n_conv = (jnp.stack(lst, axis=0) for lst in outs)
    return x.reshape(nseq, tlen, d), (n_rwkv, n_shift, n_lru, n_conv, n_hgrn)


def _block_diag(w):
    depth, g, i, j = w.shape
    eye = jnp.eye(g, dtype=w.dtype)
    return jnp.einsum('lgij,gh->lgihj', w, eye).reshape(depth, g * i, g * j)


def _prepare_params(raw, a_proj, b_width, c_kwidth, c_width):
    P = {}
    for n in ('ffn1_wg', 'ffn1_wu', 'ffn1_wd', 'ffn2_wg', 'ffn2_wu', 'ffn2_wd', 'proj_a', 'proj_b',
              'proj_c', 'w_out', 'w_in', 'rwkv_w2', 'rwkv_a2', 'rwkv_g2'):
        P[n] = raw[n].astype(BF16)
    P['w_in_sections'] = (a_proj, 2 * b_width, 2 * c_kwidth + 2 * c_width)
    for n in ('ffn1_pre_g', 'ffn1_post_g', 'mix_pre_g', 'mix_post_g', 'ffn2_pre_g', 'ffn2_post_g',
              'rwkv_mu', 'rwkv_w0', 'rwkv_a0', 'rwkv_k_k', 'rwkv_k_a', 'rwkv_ln_g', 'rwkv_ln_b',
              'lru_conv_b', 'lru_ba', 'lru_bx', 'lru_lam', 'hgrn_norm_g'):
        P[n] = raw[n][:, None, :]
    depth = raw['rwkv_r_k'].shape[0]
    P['rwkv_r_k'] = raw['rwkv_r_k'].reshape(depth, 1, -1)
    P['lru_conv_w'] = raw['lru_conv_w']
    P['lru_wa_bd'] = _block_diag(raw['lru_wa']).astype(BF16)
    P['lru_wx_bd'] = _block_diag(raw['lru_wx']).astype(BF16)
    lb_cum = jnp.cumsum(jax.nn.softmax(raw['hgrn_lb_logits'].astype(F32), axis=0), axis=0)
    P['hgrn_lb'] = (lb_cum - lb_cum[0])[:, None, :]
    return P


def kernel(x_prompt, x_sample, state_rwkv, state_shift, state_lru, state_conv, state_hgrn, ffn1_pre_g, ffn1_post_g, ffn1_wg, ffn1_wu, ffn1_wd, mix_pre_g, mix_post_g, w_in, rwkv_mu, rwkv_w0, rwkv_w2, rwkv_a0, rwkv_a2, rwkv_g2, rwkv_k_k, rwkv_k_a, rwkv_r_k, rwkv_ln_g, rwkv_ln_b, lru_conv_w, lru_conv_b, lru_wa, lru_ba, lru_wx, lru_bx, lru_lam, hgrn_lb_logits, hgrn_norm_g, proj_a, proj_b, proj_c, w_out, ffn2_pre_g, ffn2_post_g, ffn2_wg, ffn2_wu, ffn2_wd):
    raw = dict(
        ffn1_pre_g=ffn1_pre_g, ffn1_post_g=ffn1_post_g, ffn1_wg=ffn1_wg, ffn1_wu=ffn1_wu,
        ffn1_wd=ffn1_wd, mix_pre_g=mix_pre_g, mix_post_g=mix_post_g, w_in=w_in, rwkv_mu=rwkv_mu,
        rwkv_w0=rwkv_w0, rwkv_w2=rwkv_w2, rwkv_a0=rwkv_a0, rwkv_a2=rwkv_a2, rwkv_g2=rwkv_g2,
        rwkv_k_k=rwkv_k_k, rwkv_k_a=rwkv_k_a, rwkv_r_k=rwkv_r_k, rwkv_ln_g=rwkv_ln_g,
        rwkv_ln_b=rwkv_ln_b, lru_conv_w=lru_conv_w, lru_conv_b=lru_conv_b, lru_wa=lru_wa,
        lru_ba=lru_ba, lru_wx=lru_wx, lru_bx=lru_bx, lru_lam=lru_lam,
        hgrn_lb_logits=hgrn_lb_logits, hgrn_norm_g=hgrn_norm_g, proj_a=proj_a, proj_b=proj_b,
        proj_c=proj_c, w_out=w_out, ffn2_pre_g=ffn2_pre_g, ffn2_post_g=ffn2_post_g,
        ffn2_wg=ffn2_wg, ffn2_wu=ffn2_wu, ffn2_wd=ffn2_wd)
    depth, _, heads, hd, _ = state_rwkv.shape
    a_proj = state_shift.shape[-1]
    b_width = state_lru.shape[-1]
    _, _, c_heads, c_kd, c_vd = state_hgrn.shape
    P = _prepare_params(raw, a_proj, b_width, c_heads * c_kd, c_heads * c_vd)
    dt = state_rwkv.dtype
    bp = x_prompt.shape[0]
    y_prompt, p_st = _run_trunk(
        x_prompt,
        jnp.zeros((depth, bp, heads, hd, hd), dt),
        jnp.zeros((depth, bp, a_proj), dt),
        jnp.zeros((depth, bp, b_width), dt),
        jnp.zeros((depth, bp) + state_conv.shape[2:], dt),
        jnp.zeros((depth, bp, c_heads, c_kd, c_vd), dt),
        P)
    y_sample, s_st = _run_trunk(x_sample, state_rwkv, state_shift, state_lru, state_conv,
                                state_hgrn, P)
    return (y_prompt, y_sample) + tuple(t.astype(dt) for t in p_st) + tuple(t.astype(dt) for t in s_st)
```

```python
import functools

import jax
import jax.numpy as jnp
from jax import lax
from jax.experimental import pallas as pl
from jax.experimental.pallas import tpu as pltpu

F32 = jnp.float32
BF16 = jnp.bfloat16
RMS_EPS = 1e-6
GN_EPS = 64e-5
LRU_C = 8.0
KK_EPS = 1e-12

VMEM_LIMIT_BYTES = 48 * 1024 * 1024
SUBLANES = 8
HGRN_SUB = 8
RWKV_CHUNK = 64
HGRN_CHUNK = 64
RWKV_SEQS = 2
HGRN_SEQS = 2
LRU_TILE = 512
LRU_SEQS = 16
IN_PROJ_ROWS = 256
FFN_ROWS = 512
FFN_CHUNK = 256


def _cparams(*sem):
    return pltpu.CompilerParams(dimension_semantics=sem, vmem_limit_bytes=VMEM_LIMIT_BYTES)


def _mm(a, b):
    return jnp.dot(a.astype(BF16), b.astype(BF16), preferred_element_type=F32)


def _mm_nt(a, b):
    return lax.dot_general(a.astype(BF16), b.astype(BF16), (((1,), (1,)), ((), ())),
                           preferred_element_type=F32)


def _mm_tn(a, b):
    return lax.dot_general(a.astype(BF16), b.astype(BF16), (((0,), (0,)), ((), ())),
                           preferred_element_type=F32)


def _split3(x):
    hi = x.astype(BF16)
    r1 = x - hi.astype(F32)
    mid = r1.astype(BF16)
    lo = (r1 - mid.astype(F32)).astype(BF16)
    return hi, mid, lo


def _cumsum_rows(x, seg=None):
    c = x.shape[0]
    row = lax.broadcasted_iota(jnp.int32, (c, c), 0)
    col = lax.broadcasted_iota(jnp.int32, (c, c), 1)
    tri = col <= row
    if seg is not None and seg < c:
        sh = seg.bit_length() - 1
        tri = tri & (lax.shift_right_logical(row, sh) == lax.shift_right_logical(col, sh))
    tri = tri.astype(BF16)
    hi, mid, lo = _split3(x)
    out = jnp.dot(tri, lo, preferred_element_type=F32)
    out = out + jnp.dot(tri, mid, preferred_element_type=F32)
    return out + jnp.dot(tri, hi, preferred_element_type=F32)


def _rms(x, g):
    return x * lax.rsqrt(jnp.mean(x * x, axis=-1, keepdims=True) + RMS_EPS) * g


def _softplus(x):
    return jnp.maximum(x, 0.0) + jnp.log1p(jnp.exp(-jnp.abs(x)))


def _sigmoid(x):
    return 1.0 / (1.0 + jnp.exp(-x))


def _silu(x):
    return x * _sigmoid(x)


def _tile(n, pref):
    t = min(n, pref)
    while n % t:
        t //= 2
    return t


def _in_proj_kernel(sections, x_ref, g_ref, w_ref, *o_refs):
    xn = _rms(x_ref[...], g_ref[...]).astype(BF16)
    start = 0
    for width, o_ref in zip(sections, o_refs):
        o_ref[...] = jnp.dot(xn, w_ref[:, start:start + width], preferred_element_type=F32)
        start += width


def _in_proj(x, g, w, sections, l):
    m, d = x.shape
    tm = _tile(m, IN_PROJ_ROWS)

    def resident(a, cols):
        return pl.BlockSpec((None, a.shape[1], cols), lambda i: (l, 0, 0),
                            pipeline_mode=pl.Buffered(1))

    return pl.pallas_call(
        functools.partial(_in_proj_kernel, sections),
        grid=(m // tm,),
        in_specs=[pl.BlockSpec((tm, d), lambda i: (i, 0)), resident(g, d),
                  resident(w, sum(sections))],
        out_specs=[pl.BlockSpec((tm, n), lambda i: (i, 0)) for n in sections],
        out_shape=[jax.ShapeDtypeStruct((m, n), F32) for n in sections],
        compiler_params=_cparams("parallel"),
        name="in_proj",
    )(x, g, w)


def _ffn_kernel(cw, x_ref, gpre_ref, wg_ref, wu_ref, wd_ref, gpost_ref, o_ref, h_ref):
    ff = wg_ref.shape[-1]
    xn = _rms(x_ref[...], gpre_ref[...]).astype(BF16)

    def gate_up(i):
        cs = slice(i * cw, (i + 1) * cw)
        return (jnp.dot(xn, wg_ref[:, cs], preferred_element_type=F32),
                jnp.dot(xn, wu_ref[:, cs], preferred_element_type=F32))

    nchunk = ff // cw
    cur = gate_up(0)
    for i in range(nchunk):
        nxt = gate_up(i + 1) if i + 1 < nchunk else None
        h_ref[:, i * cw:(i + 1) * cw] = (_silu(cur[0]) * cur[1]).astype(BF16)
        cur = nxt
    y = jnp.dot(h_ref[...], wd_ref[...], preferred_element_type=F32)
    o_ref[...] = x_ref[...] + 0.5 * _rms(y, gpost_ref[...])


def _ffn(x, gpre, wg, wu, wd, gpost, l):
    m, d = x.shape
    ff = wg.shape[-1]
    tm = _tile(m, FFN_ROWS)
    cw = FFN_CHUNK if ff % FFN_CHUNK == 0 else ff

    def resident(a):
        return pl.BlockSpec((None,) + a.shape[1:], lambda i: (l,) + (0,) * (a.ndim - 1),
                            pipeline_mode=pl.Buffered(1))

    return pl.pallas_call(
        functools.partial(_ffn_kernel, cw),
        grid=(m // tm,),
        in_specs=[pl.BlockSpec((tm, d), lambda i: (i, 0)), resident(gpre), resident(wg),
                  resident(wu), resident(wd), resident(gpost)],
        out_specs=pl.BlockSpec((tm, d), lambda i: (i, 0)),
        out_shape=jax.ShapeDtypeStruct((m, d), F32),
        scratch_shapes=[pltpu.VMEM((tm, ff), BF16)],
        compiler_params=_cparams("parallel"),
        name="ffn",
    )(x, gpre, wg, wu, wd, gpost)


def _merge_kernel(goff, x_ref, gpre_ref, win_ref, oa_ref, ob_ref, oc_ref, pa_ref, pb_ref, pc_ref,
                  wo_ref, g_ref, o_ref):
    d = x_ref.shape[-1]
    x = x_ref[...]
    xn = _rms(x, gpre_ref[...]).astype(BF16)
    merged = None
    for i, (o_b, p_b) in enumerate(((oa_ref, pa_ref), (ob_ref, pb_ref), (oc_ref, pc_ref))):
        gate = jnp.dot(xn, win_ref[:, goff + i * d:goff + (i + 1) * d], preferred_element_type=F32)
        term = _sigmoid(gate) * _mm(o_b[...], p_b[...])
        merged = term if merged is None else merged + term
    y = _mm(merged, wo_ref[...])
    o_ref[...] = x + _rms(y, g_ref[...])


def _merge(x, gpre, w_in, goff, oa, ob, oc, pa, pb, pc, wo, g, l):
    m, d = x.shape
    tm = _tile(m, 512)

    def rows(w):
        return pl.BlockSpec((tm, w), lambda i: (i, 0))

    def whole(a):
        return pl.BlockSpec((None,) + a.shape[1:], lambda i: (l,) + (0,) * (a.ndim - 1),
                            pipeline_mode=pl.Buffered(1))

    return pl.pallas_call(
        functools.partial(_merge_kernel, goff),
        grid=(m // tm,),
        in_specs=[rows(d), whole(gpre), whole(w_in), rows(oa.shape[1]), rows(ob.shape[1]),
                  rows(oc.shape[1]), whole(pa), whole(pb), whole(pc), whole(wo), whole(g)],
        out_specs=rows(d),
        out_shape=jax.ShapeDtypeStruct((m, d), F32),
        compiler_params=_cparams("parallel"),
        name="merge_out",
    )(x, gpre, w_in, oa, ob, oc, pa, pb, pc, wo, g)


def _rwkv_kernel(heads, hd, dw, da, nb, clen, p_ref, shift_ref, s0_ref, mu_ref, w0_ref, w2_ref,
                 a0_ref, a2_ref, g2_ref, kk_ref, ka_ref, rk_ref, lng_ref, lnb_ref, *rest):
    o_ref, nshift_ref, ns_ref, st_ref, prev_ref, pf_ref = rest[-6:]
    c = pl.program_id(1)
    aw = heads * hd

    @pl.when(c == 0)
    def _():
        prev_ref[...] = shift_ref[...]
        st_ref[...] = s0_ref[...]

    p = p_ref[0] if nb == 1 else jnp.concatenate([p_ref[j] for j in range(nb)], axis=0)
    pf_ref[...] = pltpu.roll(p, 1, 0)
    for j in range(nb):
        pf_ref[j * clen:j * clen + 1, :] = prev_ref[j:j + 1, :]
        prev_ref[j:j + 1, :] = p_ref[j, clen - 1:clen, :]
    prev = pf_ref[...]
    xs = p + mu_ref[...] * (prev - p)
    r = xs[:, :aw]
    k = xs[:, aw:2 * aw]
    v = xs[:, 2 * aw:3 * aw]
    w1 = xs[:, 3 * aw:3 * aw + dw]
    a1 = xs[:, 3 * aw + dw:3 * aw + dw + da]
    g1 = xs[:, 3 * aw + dw + da:]

    w = -_softplus(-(w0_ref[...] + _mm(jnp.tanh(w1), w2_ref[...]))) - 0.5
    logw = -jnp.exp(w)
    a_icl = _sigmoid(a0_ref[...] + _mm(a1, a2_ref[...]))
    gate = _mm(_sigmoid(g1), g2_ref[...])
    kkraw = k * kk_ref[...]
    k2 = k * (1.0 + (a_icl - 1.0) * ka_ref[...])
    rkk = r * k2 * rk_ref[...]

    ginc = _cumsum_rows(logw, clen)
    glast = [ginc[(j + 1) * clen - 1:(j + 1) * clen, :] for j in range(nb)]
    eg = jnp.exp(ginc)
    egx = jnp.exp(ginc - logw)
    einv = jnp.exp(-ginc)
    ehat = [jnp.exp(glast[j] - ginc[j * clen:(j + 1) * clen, :]) for j in range(nb)]
    eglast = [jnp.exp(x) for x in glast]

    ri = lax.broadcasted_iota(jnp.int32, (2 * clen, clen), 0)
    ci = lax.broadcasted_iota(jnp.int32, (2 * clen, clen), 1)
    mask2 = ci < jnp.where(ri < clen, ri, ri - (clen - 1))
    eye = (lax.broadcasted_iota(jnp.int32, (clen, clen), 0)
           == lax.broadcasted_iota(jnp.int32, (clen, clen), 1)).astype(F32)

    pairs = [(j, h) for j in range(nb) for h in range(heads)]

    def sub(x):
        return [x[j * clen:(j + 1) * clen, h * hd:(h + 1) * hd] for j, h in pairs]

    kkr = sub(kkraw)
    kk = [x * lax.rsqrt(jnp.maximum(jnp.sum(x * x, axis=-1, keepdims=True), KK_EPS * KK_EPS))
          for x in kkr]
    bvec = [x * a for x, a in zip(kk, sub(a_icl))]
    k2s = sub(k2)
    vs = sub(v)
    einvs = sub(einv)
    ehats = [ehat[j][:, h * hd:(h + 1) * hd] for j, h in pairs]
    eglasts = [eglast[j][:, h * hd:(h + 1) * hd] for j, h in pairs]
    lhs2 = [jnp.concatenate([-x * e1, rr * e2], axis=0)
            for x, e1, rr, e2 in zip(kk, sub(egx), sub(r), sub(eg))]
    bt = [x * e for x, e in zip(bvec, einvs)]
    kt = [x * e for x, e in zip(k2s, einvs)]
    hat2 = [jnp.concatenate([x * e, y * e], axis=0) for x, y, e in zip(bvec, k2s, ehats)]
    s0 = [st_ref[j, h] for j, h in pairs]

    ab2 = [jnp.where(mask2, _mm_nt(x, y), 0.0) for x, y in zip(lhs2, bt)]
    ak2 = [jnp.where(mask2, _mm_nt(x, y), 0.0) for x, y in zip(lhs2, kt)]
    xy = [_mm_nt(x, s) + _mm(a, vv) for x, s, a, vv in zip(lhs2, s0, ak2, vs)]

    pw = [m[:clen] for m in ab2]
    tinv = [eye + m for m in pw]
    span = 2
    while span < clen:
        pw = [_mm(m, m) for m in pw]
        tinv = [t + _mm(t, m) for t, m in zip(tinv, pw)]
        span *= 2

    u = [_mm(t, m[:clen]) for t, m in zip(tinv, xy)]
    y = [m[clen:] + _mm(a[clen:], uu) for m, a, uu in zip(xy, ab2, u)]
    s_new = [s * e + _mm_tn(jnp.concatenate([uu, vv], axis=0), hh)
             for s, e, uu, vv, hh in zip(s0, eglasts, u, vs, hat2)]
    for (j, h), s in zip(pairs, s_new):
        st_ref[j, h] = s

    bonus = [jnp.sum(x, axis=-1, keepdims=True) for x in sub(rkk)]
    gates = sub(gate)
    for i, (j, h) in enumerate(pairs):
        sl = slice(h * hd, (h + 1) * hd)
        yy = y[i]
        yc = yy - jnp.mean(yy, axis=-1, keepdims=True)
        var = jnp.mean(yc * yc, axis=-1, keepdims=True)
        yy = yc * lax.rsqrt(var + GN_EPS) * lng_ref[:, sl] + lnb_ref[:, sl]
        yy = yy + bonus[i] * vs[i]
        o_ref[j, :, sl] = yy * gates[i]

    @pl.when(c == pl.num_programs(1) - 1)
    def _():
        nshift_ref[...] = prev_ref[...]
        ns_ref[...] = st_ref[...]


def _rwkv(pa, shift, s0, stacked, P, l, nseq, tlen, chunk, nb):
    ap = pa.shape[1]
    depth, _, heads, hd, _ = s0.shape
    aw = heads * hd
    dw = P['rwkv_w2'].shape[1]
    da = P['rwkv_a2'].shape[1]
    nch = tlen // chunk
    ngrp = nseq // nb
    shift = shift.reshape(depth, ngrp, nb, ap)
    pa = pa.reshape(nseq, tlen, ap)

    def vec(a):
        return pl.BlockSpec((None,) + a.shape[1:], lambda b, c: (l,) + (0,) * (a.ndim - 1))

    names = ['rwkv_mu', 'rwkv_w0', 'rwkv_w2', 'rwkv_a0', 'rwkv_a2', 'rwkv_g2', 'rwkv_k_k',
             'rwkv_k_a', 'rwkv_r_k', 'rwkv_ln_g', 'rwkv_ln_b']
    params = [P[n] for n in names]
    args = [pa, shift, s0] + params
    in_specs = [
        pl.BlockSpec((nb, chunk, ap), lambda b, c: (b, c, 0)),
        pl.BlockSpec((None, None, nb, ap), lambda b, c: (l, b, 0, 0)),
        pl.BlockSpec((None, nb, heads, hd, hd), lambda b, c: (l, b, 0, 0, 0)),
    ] + [vec(a) for a in params]
    aliases = {}
    if stacked is not None:
        aliases = {len(args): 2}
        args.append(stacked)
        in_specs.append(pl.BlockSpec(memory_space=pl.ANY))
    o, n_shift, stacked = pl.pallas_call(
        functools.partial(_rwkv_kernel, heads, hd, dw, da, nb, chunk),
        grid=(ngrp, nch),
        in_specs=in_specs,
        out_specs=[
            pl.BlockSpec((nb, chunk, aw), lambda b, c: (b, c, 0)),
            pl.BlockSpec((None, nb, ap), lambda b, c: (b, 0, 0)),
            pl.BlockSpec((None, nb, heads, hd, hd), lambda b, c: (l, b, 0, 0, 0)),
        ],
        out_shape=[
            jax.ShapeDtypeStruct((nseq, tlen, aw), F32),
            jax.ShapeDtypeStruct((ngrp, nb, ap), F32),
            jax.ShapeDtypeStruct((depth, nseq, heads, hd, hd), F32),
        ],
        scratch_shapes=[pltpu.VMEM((nb, heads, hd, hd), F32), pltpu.VMEM((nb, ap), F32),
                        pltpu.VMEM((nb * chunk, ap), F32)],
        input_output_aliases=aliases,
        compiler_params=_cparams("parallel", "arbitrary"),
        name="rwkv7",
    )(*args)
    return o.reshape(nseq * tlen, aw), n_shift.reshape(nseq, ap), stacked


def _rwkv_prep(heads, hd, dw, da, nb, clen, p_ref, r0, prm, prev_ref, pf_ref, ops):
    (mu_ref, w0_ref, w2_ref, a0_ref, a2_ref, g2_ref, kk_ref, ka_ref, rk_ref) = prm
    l2_ref, bt_ref, kt_ref, h2_ref, v_ref, bv_ref, g_ref, egl_ref = ops
    aw = heads * hd
    rows = nb * clen
    seq = [slice(j * clen, (j + 1) * clen) for j in range(nb)]

    p = jnp.concatenate([p_ref[j, r0:r0 + clen, :] for j in range(nb)], axis=0)
    pf_ref[...] = pltpu.roll(p, 1, 0)
    for j in range(nb):
        pf_ref[j * clen:j * clen + 1, :] = prev_ref[j:j + 1, :]
        prev_ref[j:j + 1, :] = p_ref[j, r0 + clen - 1:r0 + clen, :]
    xs = p + mu_ref[...] * (pf_ref[...] - p)
    yield
    r = xs[:, :aw]
    k = xs[:, aw:2 * aw]
    v = xs[:, 2 * aw:3 * aw]
    w1 = xs[:, 3 * aw:3 * aw + dw]
    a1 = xs[:, 3 * aw + dw:3 * aw + dw + da]
    g1 = xs[:, 3 * aw + dw + da:]
    w = -_softplus(-(w0_ref[...] + _mm(jnp.tanh(w1), w2_ref[...]))) - 0.5
    logw = -jnp.exp(w)
    yield
    a_icl = _sigmoid(a0_ref[...] + _mm(a1, a2_ref[...]))
    gate = _mm(_sigmoid(g1), g2_ref[...])
    for j in range(nb):
        g_ref[j] = gate[seq[j]]
    yield
    kkraw = k * kk_ref[...]
    k2 = k * (1.0 + (a_icl - 1.0) * ka_ref[...])
    rkk = r * k2 * rk_ref[...]
    ginc = _cumsum_rows(logw, clen)
    yield
    eg = jnp.exp(ginc)
    egx = jnp.exp(ginc - logw)
    einv = jnp.exp(-ginc)
    yield
    lane = lax.broadcasted_iota(jnp.int32, (rows, 128), 1)
    per = 128 // hd

    def spread(cols):
        out = []
        for g0 in range(0, heads, per):
            acc = cols[g0 + per - 1]
            for i in range(per - 2, -1, -1):
                acc = jnp.where(lane < (i + 1) * hd, cols[g0 + i], acc)
            out.append(jnp.broadcast_to(acc, (rows, 128)))
        return jnp.concatenate(out, axis=1)

    heads_sl = [slice(h * hd, (h + 1) * hd) for h in range(heads)]
    scale = spread([lax.rsqrt(jnp.maximum(
        jnp.sum(kkraw[:, sl] * kkraw[:, sl], axis=-1, keepdims=True), KK_EPS * KK_EPS))
        for sl in heads_sl])
    yield
    bonus = spread([jnp.sum(rkk[:, sl], axis=-1, keepdims=True) for sl in heads_sl])
    kk = kkraw * scale
    bvec = kk * a_icl
    yield
    for j in range(nb):
        l2_ref[j, :clen, :] = -kk[seq[j]] * egx[seq[j]]
        l2_ref[j, clen:, :] = r[seq[j]] * eg[seq[j]]
    yield
    for j in range(nb):
        bt_ref[j] = bvec[seq[j]] * einv[seq[j]]
        kt_ref[j] = k2[seq[j]] * einv[seq[j]]
    yield
    for j in range(nb):
        glast = ginc[(j + 1) * clen - 1:(j + 1) * clen, :]
        ehat = jnp.exp(glast - ginc[seq[j]])
        h2_ref[j, :clen, :] = bvec[seq[j]] * ehat
        h2_ref[j, clen:, :] = k2[seq[j]] * ehat
        egl_ref[j] = jnp.exp(glast)
    yield
    for j in range(nb):
        v_ref[j] = v[seq[j]]
        bv_ref[j] = bonus[seq[j]] * v[seq[j]]
    yield


def _rwkv_chain(heads, hd, nb, clen, ops, st_ref, lng_ref, lnb_ref, o_ref, o_r0):
    l2_ref, bt_ref, kt_ref, h2_ref, v_ref, bv_ref, g_ref, egl_ref = ops
    pairs = [(j, h) for j in range(nb) for h in range(heads)]

    def ld(ref):
        return [ref[j, :, h * hd:(h + 1) * hd] for j, h in pairs]

    ri = lax.broadcasted_iota(jnp.int32, (2 * clen, clen), 0)
    ci = lax.broadcasted_iota(jnp.int32, (2 * clen, clen), 1)
    mask2 = ci < jnp.where(ri < clen, ri, ri - (clen - 1))
    eye = (lax.broadcasted_iota(jnp.int32, (clen, clen), 0)
           == lax.broadcasted_iota(jnp.int32, (clen, clen), 1)).astype(F32)

    lhs2 = ld(l2_ref)
    ab2 = [jnp.where(mask2, _mm_nt(x, y), 0.0) for x, y in zip(lhs2, ld(bt_ref))]
    yield
    ak2 = [jnp.where(mask2, _mm_nt(x, y), 0.0) for x, y in zip(lhs2, ld(kt_ref))]
    yield
    s0 = [st_ref[j, h] for j, h in pairs]
    vs = ld(v_ref)
    xy = [_mm_nt(x, s) + _mm(a, vv) for x, s, a, vv in zip(lhs2, s0, ak2, vs)]
    yield
    pw = [m[:clen] for m in ab2]
    tinv = [eye + m for m in pw]
    span = 2
    while span < clen:
        pw = [_mm(m, m) for m in pw]
        yield
        tinv = [t + _mm(t, m) for t, m in zip(tinv, pw)]
        yield
        span *= 2
    u = [_mm(t, m[:clen]) for t, m in zip(tinv, xy)]
    yield
    y = [m[clen:] + _mm(a[clen:], uu) for m, a, uu in zip(xy, ab2, u)]
    yield
    s_new = [s * egl_ref[j, :, h * hd:(h + 1) * hd]
             + _mm_tn(jnp.concatenate([uu, vv], axis=0), hh)
             for (j, h), s, uu, vv, hh in zip(pairs, s0, u, vs, ld(h2_ref))]
    for (j, h), s in zip(pairs, s_new):
        st_ref[j, h] = s
    yield
    bvs = ld(bv_ref)
    gates = ld(g_ref)
    for i, (j, h) in enumerate(pairs):
        sl = slice(h * hd, (h + 1) * hd)
        yy = y[i]
        yc = yy - jnp.mean(yy, axis=-1, keepdims=True)
        var = jnp.mean(yc * yc, axis=-1, keepdims=True)
        yy = yc * lax.rsqrt(var + GN_EPS) * lng_ref[:, sl] + lnb_ref[:, sl]
        o_ref[j, o_r0:o_r0 + clen, sl] = (yy + bvs[i]) * gates[i]
        if i % heads == heads - 1:
            yield


def _run_streams(*gens):
    live = list(gens)
    while live:
        for g in list(live):
            try:
                next(g)
            except StopIteration:
                live.remove(g)


def _rwkv_step_kernel(heads, hd, dw, da, nb, clen, nsub, has_stack, *refs):
    refs = list(refs)
    p_ref = refs.pop(0)
    pn_ref = refs.pop(0) if nsub == 2 else None
    shift_ref, s0_ref = refs[:2]
    prm = refs[2:11]
    lng_ref, lnb_ref = refs[11:13]
    rest = refs[13 + (1 if has_stack else 0):]
    o_ref, nshift_ref, ns_ref, st_ref, prev_ref, pf_ref = rest[:6]
    ops_a = rest[6:14]
    ops_b = rest[14:22]
    c = pl.program_id(1)

    @pl.when(c == 0)
    def _():
        prev_ref[...] = shift_ref[...]
        st_ref[...] = s0_ref[...]

    def prep(src, r0, ops):
        return _rwkv_prep(heads, hd, dw, da, nb, clen, src, r0, prm, prev_ref, pf_ref, ops)

    def chain(ops, o_r0):
        return _rwkv_chain(heads, hd, nb, clen, ops, st_ref, lng_ref, lnb_ref, o_ref, o_r0)

    if nsub == 1:
        _run_streams(prep(p_ref, 0, ops_a))
        _run_streams(chain(ops_a, 0))
    else:
        @pl.when(c == 0)
        def _():
            _run_streams(prep(p_ref, 0, ops_a))

        _run_streams(chain(ops_a, 0), prep(p_ref, clen, ops_b))
        _run_streams(chain(ops_b, clen), prep(pn_ref, 0, ops_a))

    @pl.when(c == pl.num_programs(1) - 1)
    def _():
        nshift_ref[...] = prev_ref[...]
        ns_ref[...] = st_ref[...]


def _rwkv_mix(pa, shift, s0, stacked, P, l, nseq, tlen, chunk, nb):
    ap = pa.shape[1]
    depth, _, heads, hd, _ = s0.shape
    aw = heads * hd
    dw = P['rwkv_w2'].shape[1]
    da = P['rwkv_a2'].shape[1]
    nch = tlen // chunk
    nsub = 2 if nch % 2 == 0 else 1
    assert nsub == 2 or nch == 1
    nstep = nch // nsub
    ngrp = nseq // nb
    shift = shift.reshape(depth, ngrp, nb, ap)
    pa = pa.reshape(nseq, tlen, ap)

    def vec(a):
        return pl.BlockSpec((None,) + a.shape[1:], lambda b, c: (l,) + (0,) * (a.ndim - 1))

    names = ['rwkv_mu', 'rwkv_w0', 'rwkv_w2', 'rwkv_a0', 'rwkv_a2', 'rwkv_g2', 'rwkv_k_k',
             'rwkv_k_a', 'rwkv_r_k', 'rwkv_ln_g', 'rwkv_ln_b']
    params = [P[n] for n in names]
    args = [pa]
    in_specs = [pl.BlockSpec((nb, nsub * chunk, ap), lambda b, c: (b, c, 0))]
    if nsub == 2:
        args.append(pa)
        in_specs.append(pl.BlockSpec(
            (nb, chunk, ap), lambda b, c: (b, jnp.minimum(2 * c + 2, nch - 1), 0)))
    args += [shift, s0] + params
    in_specs += [
        pl.BlockSpec((None, None, nb, ap), lambda b, c: (l, b, 0, 0)),
        pl.BlockSpec((None, nb, heads, hd, hd), lambda b, c: (l, b, 0, 0, 0)),
    ] + [vec(a) for a in params]
    aliases = {}
    if stacked is not None:
        aliases = {len(args): 2}
        args.append(stacked)
        in_specs.append(pl.BlockSpec(memory_space=pl.ANY))

    def operand_set():
        f = lambda r: pltpu.VMEM((nb, r, aw), F32)
        return [f(2 * chunk), f(chunk), f(chunk), f(2 * chunk), f(chunk), f(chunk), f(chunk), f(1)]

    o, n_shift, stacked = pl.pallas_call(
        functools.partial(_rwkv_step_kernel, heads, hd, dw, da, nb, chunk, nsub,
                          stacked is not None),
        grid=(ngrp, nstep),
        in_specs=in_specs,
        out_specs=[
            pl.BlockSpec((nb, nsub * chunk, aw), lambda b, c: (b, c, 0)),
            pl.BlockSpec((None, nb, ap), lambda b, c: (b, 0, 0)),
            pl.BlockSpec((None, nb, heads, hd, hd), lambda b, c: (l, b, 0, 0, 0)),
        ],
        out_shape=[
            jax.ShapeDtypeStruct((nseq, tlen, aw), F32),
            jax.ShapeDtypeStruct((ngrp, nb, ap), F32),
            jax.ShapeDtypeStruct((depth, nseq, heads, hd, hd), F32),
        ],
        scratch_shapes=[pltpu.VMEM((nb, heads, hd, hd), F32), pltpu.VMEM((nb, ap), F32),
                        pltpu.VMEM((nb * chunk, ap), F32)] + operand_set() + operand_set(),
        input_output_aliases=aliases,
        compiler_params=_cparams("parallel", "arbitrary"),
        name="rwkv7",
    )(*args)
    return o.reshape(nseq * tlen, aw), n_shift.reshape(nseq, ap), stacked


def _lru_kernel(bw, seqs, p_ref, cs_ref, h0_ref, cw_ref, cb_ref, wa_ref, ba_ref, wx_ref, bx_ref,
                lam_ref, o_ref, nconv_ref, nh_ref, tail_ref, h_ref):
    c = pl.program_id(1)
    rows = p_ref.shape[0]
    ncv = cs_ref.shape[1]
    nseg = rows // SUBLANES

    if seqs == 1:
        @pl.when(c == 0)
        def _():
            tail_ref[...] = jnp.zeros_like(tail_ref)
            tail_ref[SUBLANES - ncv:, :] = cs_ref[0]
            h_ref[...] = h0_ref[0]
    else:
        tail_ref[...] = jnp.zeros_like(tail_ref)
        for j in range(seqs):
            tail_ref[(j + 1) * SUBLANES - ncv:(j + 1) * SUBLANES, :] = cs_ref[j]

    xb = p_ref[:, :bw]
    gb = p_ref[:, bw:]
    rowi = lax.broadcasted_iota(jnp.int32, (rows, 1), 0)
    pos = rowi & (SUBLANES - 1)

    def seg_roll(x, d):
        return pltpu.roll(x.reshape(nseg, SUBLANES, bw), d, 1).reshape(rows, bw)

    if seqs == 1 and rows > SUBLANES:
        before = jnp.concatenate([tail_ref[...], xb[:rows - SUBLANES, :]], axis=0)
    else:
        before = tail_ref[...]
    xc = cb_ref[...] + xb * cw_ref[ncv:ncv + 1, :]
    for d in range(1, ncv + 1):
        sh = jnp.where(pos >= d, seg_roll(xb, d), seg_roll(before, d))
        xc = xc + sh * cw_ref[ncv - d:ncv - d + 1, :]
    if seqs == 1:
        tail_ref[...] = xb[rows - SUBLANES:, :]

    rg = _sigmoid(_mm(xc, wa_ref[...]) + ba_ref[...])
    ig = _sigmoid(_mm(xc, wx_ref[...]) + bx_ref[...])
    log_a = -LRU_C * rg * _softplus(-lam_ref[...])
    a = jnp.exp(log_a)
    b = jnp.sqrt(-jnp.tanh(log_a) * (1.0 + a * a)) * ig * xc

    d = 1
    while d < SUBLANES:
        keep = pos >= d
        a_sh = jnp.where(keep, seg_roll(a, d), 1.0)
        b_sh = jnp.where(keep, seg_roll(b, d), 0.0)
        b = a * b_sh + b
        a = a_sh * a
        d *= 2
    pieces = []
    hc = h_ref[...] if seqs == 1 else None
    for s in range(nseg):
        rs = slice(s * SUBLANES, (s + 1) * SUBLANES)
        h_s = a[rs] * (hc if seqs == 1 else h0_ref[s]) + b[rs]
        hc = h_s[SUBLANES - 1:, :]
        pieces.append(h_s)
    h = pieces[0] if nseg == 1 else jnp.concatenate(pieces, axis=0)
    o_ref[...] = h * jax.nn.gelu(gb)

    if seqs == 1:
        h_ref[...] = hc

        @pl.when(c == pl.num_programs(1) - 1)
        def _():
            nconv_ref[0] = xb[rows - ncv:, :]
            nh_ref[0] = hc
    else:
        for s in range(seqs):
            nconv_ref[s] = xb[(s + 1) * SUBLANES - ncv:(s + 1) * SUBLANES, :]
            nh_ref[s] = pieces[s][SUBLANES - 1:, :]


def _lru(pb, conv, h0, P, l, nseq, tlen, tile, seqs):
    bw = h0.shape[-1]
    ncv = conv.shape[2]
    nt = tlen // tile
    assert seqs == 1 or (nt == 1 and tile == SUBLANES)
    rows = seqs * tile

    def vec(a):
        return pl.BlockSpec((None,) + a.shape[1:], lambda b, c: (l,) + (0,) * (a.ndim - 1))

    names = ['lru_conv_w', 'lru_conv_b', 'lru_wa_bd', 'lru_ba', 'lru_wx_bd', 'lru_bx', 'lru_lam']
    params = [P[n] for n in names]
    return pl.pallas_call(
        functools.partial(_lru_kernel, bw, seqs),
        grid=(nseq // seqs, nt),
        in_specs=[
            pl.BlockSpec((rows, 2 * bw), lambda b, c: (b * nt + c, 0)),
            pl.BlockSpec((None, seqs, ncv, bw), lambda b, c: (l, b, 0, 0)),
            pl.BlockSpec((None, seqs, 1, bw), lambda b, c: (l, b, 0, 0)),
        ] + [vec(a) for a in params],
        out_specs=[
            pl.BlockSpec((rows, bw), lambda b, c: (b * nt + c, 0)),
            pl.BlockSpec((seqs, ncv, bw), lambda b, c: (b, 0, 0)),
            pl.BlockSpec((seqs, 1, bw), lambda b, c: (b, 0, 0)),
        ],
        out_shape=[
            jax.ShapeDtypeStruct((nseq * tlen, bw), F32),
            jax.ShapeDtypeStruct((nseq, ncv, bw), F32),
            jax.ShapeDtypeStruct((nseq, 1, bw), F32),
        ],
        scratch_shapes=[pltpu.VMEM((SUBLANES if seqs == 1 else rows, bw), F32),
                        pltpu.VMEM((1, bw), F32)],
        compiler_params=_cparams("parallel", "arbitrary"),
        name="rglru",
    )(pb, conv, h0, *params)


def _hgrn_kernel(heads, kd, vd, sub, nb, clen, p_ref, s0_ref, lb_ref, ng_ref, *rest):
    o_ref, ns_ref, st_ref = rest[-3:]
    c = pl.program_id(1)
    kw = heads * kd
    vw = heads * vd
    pairs = [(j, h) for j in range(nb) for h in range(heads)]

    @pl.when(c == 0)
    def _():
        for j, h in pairs:
            st_ref[j, h] = s0_ref[j, h].T

    p = p_ref[0] if nb == 1 else jnp.concatenate([p_ref[j] for j in range(nb)], axis=0)
    lb = lb_ref[...]
    q = _silu(p[:, :kw])
    f = lb + (1.0 - lb) * _sigmoid(p[:, kw:2 * kw])
    logf = jnp.log(f)
    kf = 1.0 - f
    v = p[:, 2 * kw:2 * kw + vw]
    gz = p[:, 2 * kw + vw:]

    g = _cumsum_rows(logf, clen)
    qg = q * jnp.exp(g)
    glast = [g[(j + 1) * clen - 1:(j + 1) * clen, :] for j in range(nb)]
    khat = [kf[j * clen:(j + 1) * clen, :] * jnp.exp(glast[j] - g[j * clen:(j + 1) * clen, :])
            for j in range(nb)]
    eglast = [jnp.exp(x) for x in glast]

    nsub = clen // sub
    srow = lax.broadcasted_iota(jnp.int32, (sub, 1), 0)
    lane = lax.broadcasted_iota(jnp.int32, (sub, 128), 1)

    st = [st_ref[j, h] for j, h in pairs]
    o_inter = [_mm_nt(qg[j * clen:(j + 1) * clen, h * kd:(h + 1) * kd], s)
               for (j, h), s in zip(pairs, st)]
    st_new = [s * eglast[j][:, h * kd:(h + 1) * kd]
              + _mm_tn(v[j * clen:(j + 1) * clen, h * vd:(h + 1) * vd],
                       khat[j][:, h * kd:(h + 1) * kd])
              for (j, h), s in zip(pairs, st)]
    for (j, h), s in zip(pairs, st_new):
        st_ref[j, h] = s

    blocks = [(pi, j, h, i) for pi, (j, h) in enumerate(pairs) for i in range(nsub)]
    rel = []
    for pi, j, h, i in blocks:
        if i == 0:
            continue
        ks = slice(h * kd, (h + 1) * kd)
        r0 = j * clen + i * sub
        gref = g[r0 - 1:r0, ks]
        q_rel = q[r0:r0 + sub, ks] * jnp.exp(g[r0:r0 + sub, ks] - gref)
        k_rel = kf[j * clen:r0, ks] * jnp.exp(gref - g[j * clen:r0, ks])
        rel.append(((pi, i), q_rel, k_rel, v[j * clen:r0, h * vd:(h + 1) * vd]))
    att_off = [(key, _mm_nt(q_rel, k_rel), vv) for key, q_rel, k_rel, vv in rel]
    o_off = {key: _mm(att, vv) for key, att, vv in att_off}

    fqk = []
    for pi, j, h, i in blocks:
        ks = slice(h * kd, (h + 1) * kd)
        r0 = j * clen + i * sub
        fqk.append((f[r0:r0 + sub, ks], q[r0:r0 + sub, ks], kf[r0:r0 + sub, ks]))
    att_t = [jnp.zeros((sub, 128), F32) for _ in blocks]
    dec = [jnp.zeros((sub, kd), F32) for _ in blocks]
    for t in range(sub):
        dec = [jnp.where(srow == t, k_i, d * f_i[t:t + 1, :]) for d, (f_i, q_i, k_i) in zip(dec, fqk)]
        cols = [jnp.sum(q_i[t:t + 1, :] * d, axis=-1, keepdims=True)
                for d, (f_i, q_i, k_i) in zip(dec, fqk)]
        att_t = [jnp.where(lane == t, col, a) for col, a in zip(cols, att_t)]
    o_diag = [_mm_tn(a[:, :sub],
                     v[j * clen + i * sub:j * clen + (i + 1) * sub, h * vd:(h + 1) * vd])
              for a, (pi, j, h, i) in zip(att_t, blocks)]

    for pi, (j, h) in enumerate(pairs):
        vs = slice(h * vd, (h + 1) * vd)
        parts = []
        for i in range(nsub):
            o_i = o_diag[pi * nsub + i]
            if i > 0:
                o_i = o_i + o_off[(pi, i)]
            parts.append(o_i)
        o = o_inter[pi] + (parts[0] if nsub == 1 else jnp.concatenate(parts, axis=0))
        o = o * lax.rsqrt(jnp.mean(o * o, axis=-1, keepdims=True) + RMS_EPS) * ng_ref[:, vs]
        o_ref[j, :, vs] = o * _silu(gz[j * clen:(j + 1) * clen, vs])

    @pl.when(c == pl.num_programs(1) - 1)
    def _():
        for j, h in pairs:
            ns_ref[j, h] = st_ref[j, h].T


def _hgrn(pc, s0, stacked, lb, ng, l, nseq, tlen, chunk, nb):
    depth, _, heads, kd, vd = s0.shape
    width = pc.shape[1]
    nch = tlen // chunk
    sub = min(HGRN_SUB, chunk)
    args = [pc.reshape(nseq, tlen, width), s0, lb, ng]
    in_specs = [
        pl.BlockSpec((nb, chunk, width), lambda b, c: (b, c, 0)),
        pl.BlockSpec((None, nb, heads, kd, vd), lambda b, c: (l, b, 0, 0, 0)),
        pl.BlockSpec((None, 1, heads * kd), lambda b, c: (l, 0, 0)),
        pl.BlockSpec((None, 1, heads * vd), lambda b, c: (l, 0, 0)),
    ]
    aliases = {}
    if stacked is not None:
        aliases = {len(args): 1}
        args.append(stacked)
        in_specs.append(pl.BlockSpec(memory_space=pl.ANY))
    o, stacked = pl.pallas_call(
        functools.partial(_hgrn_kernel, heads, kd, vd, sub, nb, chunk),
        grid=(nseq // nb, nch),
        in_specs=in_specs,
        out_specs=[
            pl.BlockSpec((nb, chunk, heads * vd), lambda b, c: (b, c, 0)),
            pl.BlockSpec((None, nb, heads, kd, vd), lambda b, c: (l, b, 0, 0, 0)),
        ],
        out_shape=[
            jax.ShapeDtypeStruct((nseq, tlen, heads * vd), F32),
            jax.ShapeDtypeStruct((depth, nseq, heads, kd, vd), F32),
        ],
        scratch_shapes=[pltpu.VMEM((nb, heads, vd, kd), F32)],
        input_output_aliases=aliases,
        compiler_params=_cparams("parallel", "arbitrary"),
        name="hgrn2",
    )(*args)
    return o.reshape(nseq * tlen, heads * vd), stacked


def _run_trunk(x, s_rwkv, s_shift, s_lru, s_conv, s_hgrn, P):
    nseq, tlen, d = x.shape
    depth = s_rwkv.shape[0]
    x = x.reshape(nseq * tlen, d)
    lru4 = s_lru[:, :, None, :]
    rw_chunk = min(RWKV_CHUNK, tlen)
    rw_nb = _tile(nseq, max(RWKV_SEQS, RWKV_CHUNK // rw_chunk))
    hg_chunk = min(HGRN_CHUNK, tlen)
    hg_nb = _tile(nseq, max(HGRN_SEQS, HGRN_CHUNK // hg_chunk))
    lru_tile = min(LRU_TILE, tlen)
    lru_seqs = _tile(nseq, LRU_SEQS) if lru_tile == tlen == SUBLANES else 1
    outs = ([], [], [])
    n_rwkv = n_hgrn = None
    for l in range(depth):
        x = _ffn(x, P['ffn1_pre_g'], P['ffn1_wg'], P['ffn1_wu'], P['ffn1_wd'], P['ffn1_post_g'], l)
        pa, pb, pc = _in_proj(x, P['mix_pre_g'], P['w_in'], P['w_in_sections'], l)
        oa, n_shift, n_rwkv = _rwkv_mix(pa, s_shift, s_rwkv, n_rwkv, P, l, nseq, tlen, rw_chunk,
                                        rw_nb)
        ob, n_conv, n_lru = _lru(pb, s_conv, lru4, P, l, nseq, tlen, lru_tile, lru_seqs)
        oc, n_hgrn = _hgrn(pc, s_hgrn, n_hgrn, P['hgrn_lb'], P['hgrn_norm_g'], l, nseq, tlen,
                           hg_chunk, hg_nb)
        x = _merge(x, P['mix_pre_g'], P['w_in'], sum(P['w_in_sections']), oa, ob, oc, P['proj_a'],
                   P['proj_b'], P['proj_c'], P['w_out'], P['mix_post_g'], l)
        x = _ffn(x, P['ffn2_pre_g'], P['ffn2_wg'], P['ffn2_wu'], P['ffn2_wd'], P['ffn2_post_g'], l)
        for lst, t in zip(outs, (n_shift, n_lru[:, 0], n_conv)):
            lst.append(t)
    n_shift, n_lru, n_conv = (jnp.stack(lst, axis=0) for lst in outs)
    return x.reshape(nseq, tlen, d), (n_rwkv, n_shift, n_lru, n_conv, n_hgrn)


def _block_diag(w):
    depth, g, i, j = w.shape
    eye = jnp.eye(g, dtype=w.dtype)
    return jnp.einsum('lgij,gh->lgihj', w, eye).reshape(depth, g * i, g * j)


def _prepare_params(raw, a_proj, b_width, c_kwidth, c_width):
    P = {}
    for n in ('ffn1_wg', 'ffn1_wu', 'ffn1_wd', 'ffn2_wg', 'ffn2_wu', 'ffn2_wd', 'proj_a', 'proj_b',
              'proj_c', 'w_out', 'w_in', 'rwkv_w2', 'rwkv_a2', 'rwkv_g2'):
        P[n] = raw[n].astype(BF16)
    P['w_in_sections'] = (a_proj, 2 * b_width, 2 * c_kwidth + 2 * c_width)
    for n in ('ffn1_pre_g', 'ffn1_post_g', 'mix_pre_g', 'mix_post_g', 'ffn2_pre_g', 'ffn2_post_g',
              'rwkv_mu', 'rwkv_w0', 'rwkv_a0', 'rwkv_k_k', 'rwkv_k_a', 'rwkv_ln_g', 'rwkv_ln_b',
              'lru_conv_b', 'lru_ba', 'lru_bx', 'lru_lam', 'hgrn_norm_g'):
        P[n] = raw[n][:, None, :]
    depth = raw['rwkv_r_k'].shape[0]
    P['rwkv_r_k'] = raw['rwkv_r_k'].reshape(depth, 1, -1)
    P['lru_conv_w'] = raw['lru_conv_w']
    P['lru_wa_bd'] = _block_diag(raw['lru_wa']).astype(BF16)
    P['lru_wx_bd'] = _block_diag(raw['lru_wx']).astype(BF16)
    lb_cum = jnp.cumsum(jax.nn.softmax(raw['hgrn_lb_logits'].astype(F32), axis=0), axis=0)
    P['hgrn_lb'] = (lb_cum - lb_cum[0])[:, None, :]
    return P


def kernel(x_prompt, x_sample, state_rwkv, state_shift, state_lru, state_conv, state_hgrn, ffn1_pre_g, ffn1_post_g, ffn1_wg, ffn1_wu, ffn1_wd, mix_pre_g, mix_post_g, w_in, rwkv_mu, rwkv_w0, rwkv_w2, rwkv_a0, rwkv_a2, rwkv_g2, rwkv_k_k, rwkv_k_a, rwkv_r_k, rwkv_ln_g, rwkv_ln_b, lru_conv_w, lru_conv_b, lru_wa, lru_ba, lru_wx, lru_bx, lru_lam, hgrn_lb_logits, hgrn_norm_g, proj_a, proj_b, proj_c, w_out, ffn2_pre_g, ffn2_post_g, ffn2_wg, ffn2_wu, ffn2_wd):
    raw = dict(
        ffn1_pre_g=ffn1_pre_g, ffn1_post_g=ffn1_post_g, ffn1_wg=ffn1_wg, ffn1_wu=ffn1_wu,
        ffn1_wd=ffn1_wd, mix_pre_g=mix_pre_g, mix_post_g=mix_post_g, w_in=w_in, rwkv_mu=rwkv_mu,
        rwkv_w0=rwkv_w0, rwkv_w2=rwkv_w2, rwkv_a0=rwkv_a0, rwkv_a2=rwkv_a2, rwkv_g2=rwkv_g2,
        rwkv_k_k=rwkv_k_k, rwkv_k_a=rwkv_k_a, rwkv_r_k=rwkv_r_k, rwkv_ln_g=rwkv_ln_g,
        rwkv_ln_b=rwkv_ln_b, lru_conv_w=lru_conv_w, lru_conv_b=lru_conv_b, lru_wa=lru_wa,
        lru_ba=lru_ba, lru_wx=lru_wx, lru_bx=lru_bx, lru_lam=lru_lam,
        hgrn_lb_logits=hgrn_lb_logits, hgrn_norm_g=hgrn_norm_g, proj_a=proj_a, proj_b=proj_b,
        proj_c=proj_c, w_out=w_out, ffn2_pre_g=ffn2_pre_g, ffn2_post_g=ffn2_post_g,
        ffn2_wg=ffn2_wg, ffn2_wu=ffn2_wu, ffn2_wd=ffn2_wd)
    depth, _, heads, hd, _ = state_rwkv.shape
    a_proj = state_shift.shape[-1]
    b_width = state_lru.shape[-1]
    _, _, c_heads, c_kd, c_vd = state_hgrn.shape
    P = _prepare_params(raw, a_proj, b_width, c_heads * c_kd, c_heads * c_vd)
    dt = state_rwkv.dtype
    bp = x_prompt.shape[0]
    y_prompt, p_st = _run_trunk(
        x_prompt,
        jnp.zeros((depth, bp, heads, hd, hd), dt),
        jnp.zeros((depth, bp, a_proj), dt),
        jnp.zeros((depth, bp, b_width), dt),
        jnp.zeros((depth, bp) + state_conv.shape[2:], dt),
        jnp.zeros((depth, bp, c_heads, c_kd, c_vd), dt),
        P)
    y_sample, s_st = _run_trunk(x_sample, state_rwkv, state_shift, state_lru, state_conv,
                                state_hgrn, P)
    return (y_prompt, y_sample) + tuple(t.astype(dt) for t in p_st) + tuple(t.astype(dt) for t in s_st)
```

```python
import functools

import jax
import jax.numpy as jnp
from jax import lax
from jax.experimental import pallas as pl
from jax.experimental.pallas import tpu as pltpu

F32 = jnp.float32
BF16 = jnp.bfloat16
RMS_EPS = 1e-6
GN_EPS = 64e-5
LRU_C = 8.0
KK_EPS = 1e-12

VMEM_LIMIT_BYTES = 48 * 1024 * 1024
SUBLANES = 8
HGRN_SUB = 8
RWKV_CHUNK = 64
HGRN_CHUNK = 64
RWKV_SEQS = 4
HGRN_SEQS = 2
LRU_TILE = 512
LRU_SEQS = 16
IN_PROJ_ROWS = 256
FFN_ROWS = 512
FFN_CHUNK = 256


def _cparams(*sem):
    return pltpu.CompilerParams(dimension_semantics=sem, vmem_limit_bytes=VMEM_LIMIT_BYTES)


def _mm(a, b):
    return jnp.dot(a.astype(BF16), b.astype(BF16), preferred_element_type=F32)


def _mm_nt(a, b):
    return lax.dot_general(a.astype(BF16), b.astype(BF16), (((1,), (1,)), ((), ())),
                           preferred_element_type=F32)


def _mm_tn(a, b):
    return lax.dot_general(a.astype(BF16), b.astype(BF16), (((0,), (0,)), ((), ())),
                           preferred_element_type=F32)


def _split3(x):
    hi = x.astype(BF16)
    r1 = x - hi.astype(F32)
    mid = r1.astype(BF16)
    lo = (r1 - mid.astype(F32)).astype(BF16)
    return hi, mid, lo


def _cumsum_rows(x, seg=None):
    c = x.shape[0]
    row = lax.broadcasted_iota(jnp.int32, (c, c), 0)
    col = lax.broadcasted_iota(jnp.int32, (c, c), 1)
    tri = col <= row
    if seg is not None and seg < c:
        sh = seg.bit_length() - 1
        tri = tri & (lax.shift_right_logical(row, sh) == lax.shift_right_logical(col, sh))
    tri = tri.astype(BF16)
    hi, mid, lo = _split3(x)
    out = jnp.dot(tri, lo, preferred_element_type=F32)
    out = out + jnp.dot(tri, mid, preferred_element_type=F32)
    return out + jnp.dot(tri, hi, preferred_element_type=F32)


def _rms(x, g):
    return x * lax.rsqrt(jnp.mean(x * x, axis=-1, keepdims=True) + RMS_EPS) * g


def _softplus(x):
    return jnp.maximum(x, 0.0) + jnp.log1p(jnp.exp(-jnp.abs(x)))


def _sigmoid(x):
    return 1.0 / (1.0 + jnp.exp(-x))


def _silu(x):
    return x * _sigmoid(x)


LANES = 128


def _lane_block(rows, width, size):
    lane = lax.broadcasted_iota(jnp.int32, (rows, width), 1)
    return lax.shift_right_logical(lane, size.bit_length() - 1)


def _head_sums(x, hd):
    rows, width = x.shape
    per = LANES // hd
    blk = _lane_block(rows, LANES, hd)
    out = []
    for g in range(width // LANES):
        xg = x[:, g * LANES:(g + 1) * LANES]
        acc = None
        for i in reversed(range(per)):
            col = jnp.sum(jnp.where(blk == i, xg, 0.0), axis=-1, keepdims=True)
            acc = jnp.broadcast_to(col, (rows, LANES)) if acc is None else jnp.where(blk == i, col, acc)
        out.append(acc)
    return out[0] if len(out) == 1 else jnp.concatenate(out, axis=1)


def _blockdiag(x, size):
    r, width = x.shape
    blk = _lane_block(r, width, size)
    return jnp.concatenate([jnp.where(blk == i, x, 0.0) for i in range(width // size)], axis=0)


def _run_streams(*gens):
    live = list(gens)
    while live:
        for g in list(live):
            try:
                next(g)
            except StopIteration:
                live.remove(g)


def _tile(n, pref):
    t = min(n, pref)
    while n % t:
        t //= 2
    return t


def _in_proj_kernel(sections, x_ref, g_ref, w_ref, *o_refs):
    xn = _rms(x_ref[...], g_ref[...]).astype(BF16)
    start = 0
    for width, o_ref in zip(sections, o_refs):
        o_ref[...] = jnp.dot(xn, w_ref[:, start:start + width], preferred_element_type=F32)
        start += width


def _in_proj(x, g, w, sections, l):
    m, d = x.shape
    tm = _tile(m, IN_PROJ_ROWS)

    def resident(a, cols):
        return pl.BlockSpec((None, a.shape[1], cols), lambda i: (l, 0, 0),
                            pipeline_mode=pl.Buffered(1))

    return pl.pallas_call(
        functools.partial(_in_proj_kernel, sections),
        grid=(m // tm,),
        in_specs=[pl.BlockSpec((tm, d), lambda i: (i, 0)), resident(g, d),
                  resident(w, sum(sections))],
        out_specs=[pl.BlockSpec((tm, n), lambda i: (i, 0)) for n in sections],
        out_shape=[jax.ShapeDtypeStruct((m, n), F32) for n in sections],
        compiler_params=_cparams("parallel"),
        name="in_proj",
    )(x, g, w)


def _ffn_kernel(cw, x_ref, gpre_ref, wg_ref, wu_ref, wd_ref, gpost_ref, o_ref, h_ref):
    ff = wg_ref.shape[-1]
    xn = _rms(x_ref[...], gpre_ref[...]).astype(BF16)

    def gate_up(i):
        cs = slice(i * cw, (i + 1) * cw)
        return (jnp.dot(xn, wg_ref[:, cs], preferred_element_type=F32),
                jnp.dot(xn, wu_ref[:, cs], preferred_element_type=F32))

    nchunk = ff // cw
    cur = gate_up(0)
    for i in range(nchunk):
        nxt = gate_up(i + 1) if i + 1 < nchunk else None
        h_ref[:, i * cw:(i + 1) * cw] = (_silu(cur[0]) * cur[1]).astype(BF16)
        cur = nxt
    y = jnp.dot(h_ref[...], wd_ref[...], preferred_element_type=F32)
    o_ref[...] = x_ref[...] + 0.5 * _rms(y, gpost_ref[...])


def _ffn(x, gpre, wg, wu, wd, gpost, l):
    m, d = x.shape
    ff = wg.shape[-1]
    tm = _tile(m, FFN_ROWS)
    cw = FFN_CHUNK if ff % FFN_CHUNK == 0 else ff

    def resident(a):
        return pl.BlockSpec((None,) + a.shape[1:], lambda i: (l,) + (0,) * (a.ndim - 1),
                            pipeline_mode=pl.Buffered(1))

    return pl.pallas_call(
        functools.partial(_ffn_kernel, cw),
        grid=(m // tm,),
        in_specs=[pl.BlockSpec((tm, d), lambda i: (i, 0)), resident(gpre), resident(wg),
                  resident(wu), resident(wd), resident(gpost)],
        out_specs=pl.BlockSpec((tm, d), lambda i: (i, 0)),
        out_shape=jax.ShapeDtypeStruct((m, d), F32),
        scratch_shapes=[pltpu.VMEM((tm, ff), BF16)],
        compiler_params=_cparams("parallel"),
        name="ffn",
    )(x, gpre, wg, wu, wd, gpost)


def _merge_kernel(goff, x_ref, gpre_ref, win_ref, oa_ref, ob_ref, oc_ref, pa_ref, pb_ref, pc_ref,
                  wo_ref, g_ref, o_ref):
    d = x_ref.shape[-1]
    x = x_ref[...]
    xn = _rms(x, gpre_ref[...]).astype(BF16)
    merged = None
    for i, (o_b, p_b) in enumerate(((oa_ref, pa_ref), (ob_ref, pb_ref), (oc_ref, pc_ref))):
        gate = jnp.dot(xn, win_ref[:, goff + i * d:goff + (i + 1) * d], preferred_element_type=F32)
        term = _sigmoid(gate) * _mm(o_b[...], p_b[...])
        merged = term if merged is None else merged + term
    y = _mm(merged, wo_ref[...])
    o_ref[...] = x + _rms(y, g_ref[...])


def _merge(x, gpre, w_in, goff, oa, ob, oc, pa, pb, pc, wo, g, l):
    m, d = x.shape
    tm = _tile(m, 512)

    def rows(w):
        return pl.BlockSpec((tm, w), lambda i: (i, 0))

    def whole(a):
        return pl.BlockSpec((None,) + a.shape[1:], lambda i: (l,) + (0,) * (a.ndim - 1),
                            pipeline_mode=pl.Buffered(1))

    return pl.pallas_call(
        functools.partial(_merge_kernel, goff),
        grid=(m // tm,),
        in_specs=[rows(d), whole(gpre), whole(w_in), rows(oa.shape[1]), rows(ob.shape[1]),
                  rows(oc.shape[1]), whole(pa), whole(pb), whole(pc), whole(wo), whole(g)],
        out_specs=rows(d),
        out_shape=jax.ShapeDtypeStruct((m, d), F32),
        compiler_params=_cparams("parallel"),
        name="merge_out",
    )(x, gpre, w_in, oa, ob, oc, pa, pb, pc, wo, g)


def _rwkv_prep(heads, hd, dw, da, nb, clen, p_ref, r0, prm, prev_ref, pf_ref, ops):
    (mu_ref, w0_ref, w2_ref, a0_ref, a2_ref, g2_ref, kk_ref, ka_ref, rk_ref) = prm
    l2_ref, bt_ref, kt_ref, h2_ref, v_ref, bv_ref, g_ref, egl_ref = ops
    aw = heads * hd
    rows = nb * clen
    seq = [slice(j * clen, (j + 1) * clen) for j in range(nb)]

    p = jnp.concatenate([p_ref[j, r0:r0 + clen, :] for j in range(nb)], axis=0)
    pf_ref[...] = pltpu.roll(p, 1, 0)
    for j in range(nb):
        pf_ref[j * clen:j * clen + 1, :] = prev_ref[j:j + 1, :]
        prev_ref[j:j + 1, :] = p_ref[j, r0 + clen - 1:r0 + clen, :]
    xs = p + mu_ref[...] * (pf_ref[...] - p)
    yield
    r = xs[:, :aw]
    k = xs[:, aw:2 * aw]
    v = xs[:, 2 * aw:3 * aw]
    w1 = xs[:, 3 * aw:3 * aw + dw]
    a1 = xs[:, 3 * aw + dw:3 * aw + dw + da]
    g1 = xs[:, 3 * aw + dw + da:]
    w = -_softplus(-(w0_ref[...] + _mm(jnp.tanh(w1), w2_ref[...]))) - 0.5
    logw = -jnp.exp(w)
    yield
    a_icl = _sigmoid(a0_ref[...] + _mm(a1, a2_ref[...]))
    gate = _mm(_sigmoid(g1), g2_ref[...])
    for j in range(nb):
        g_ref[j] = gate[seq[j]]
    yield
    kkraw = k * kk_ref[...]
    k2 = k * (1.0 + (a_icl - 1.0) * ka_ref[...])
    rkk = r * k2 * rk_ref[...]
    ginc = _cumsum_rows(logw, clen)
    yield
    eg = jnp.exp(ginc)
    egx = jnp.exp(ginc - logw)
    einv = jnp.exp(-ginc)
    yield
    scale = lax.rsqrt(jnp.maximum(_head_sums(kkraw * kkraw, hd), KK_EPS * KK_EPS))
    yield
    bonus = _head_sums(rkk, hd)
    kk = kkraw * scale
    bvec = kk * a_icl
    yield
    for j in range(nb):
        l2_ref[j, :clen, :] = -kk[seq[j]] * egx[seq[j]]
        l2_ref[j, clen:, :] = r[seq[j]] * eg[seq[j]]
    yield
    for j in range(nb):
        bt_ref[j] = bvec[seq[j]] * einv[seq[j]]
        kt_ref[j] = k2[seq[j]] * einv[seq[j]]
    yield
    for j in range(nb):
        glast = ginc[(j + 1) * clen - 1:(j + 1) * clen, :]
        ehat = jnp.exp(glast - ginc[seq[j]])
        h2_ref[j, :clen, :] = bvec[seq[j]] * ehat
        h2_ref[j, clen:, :] = k2[seq[j]] * ehat
        egl_ref[j] = jnp.exp(glast)
    yield
    for j in range(nb):
        v_ref[j] = v[seq[j]]
        bv_ref[j] = bonus[seq[j]] * v[seq[j]]
    yield


def _rwkv_chain(heads, hd, nb, clen, ops, st_ref, lng_ref, lnb_ref, o_ref, o_r0):
    l2_ref, bt_ref, kt_ref, h2_ref, v_ref, bv_ref, g_ref, egl_ref = ops
    per = LANES // hd
    cw = per * clen
    units = [(j, g) for j in range(nb) for g in range(heads // per)]

    def ld(ref):
        return [ref[j, :, g * LANES:(g + 1) * LANES] for j, g in units]

    ri = lax.broadcasted_iota(jnp.int32, (2 * clen, cw), 0)
    cpos = lax.broadcasted_iota(jnp.int32, (2 * clen, cw), 1) & (clen - 1)
    mask2 = cpos < jnp.where(ri < clen, ri, ri - (clen - 1))
    eye = ((lax.broadcasted_iota(jnp.int32, (clen, cw), 1) & (clen - 1))
           == lax.broadcasted_iota(jnp.int32, (clen, cw), 0)).astype(F32)
    blk = _lane_block(hd, LANES, hd)

    lhs2 = ld(l2_ref)
    ab2 = [jnp.where(mask2, _mm_nt(x, _blockdiag(y, hd)), 0.0) for x, y in zip(lhs2, ld(bt_ref))]
    yield
    ak2 = [jnp.where(mask2, _mm_nt(x, _blockdiag(y, hd)), 0.0) for x, y in zip(lhs2, ld(kt_ref))]
    yield
    s0 = [st_ref[j, g] for j, g in units]
    vs = ld(v_ref)
    xy = [_mm_nt(x, _blockdiag(s, hd)) + _mm(a, _blockdiag(vv, hd))
          for x, s, a, vv in zip(lhs2, s0, ak2, vs)]
    yield
    pw = [m[:clen] for m in ab2]
    tinv = [eye + m for m in pw]
    span = 2
    while span < clen:
        pwd = [_blockdiag(m, clen) for m in pw]
        pw = [_mm(m, md) for m, md in zip(pw, pwd)]
        yield
        pwd = [_blockdiag(m, clen) for m in pw]
        tinv = [t + _mm(t, md) for t, md in zip(tinv, pwd)]
        yield
        span *= 2
    u = [_mm(t, _blockdiag(m[:clen], hd)) for t, m in zip(tinv, xy)]
    yield
    y = [m[clen:] + _mm(a[clen:], _blockdiag(uu, hd)) for m, a, uu in zip(xy, ab2, u)]
    yield
    full = [_mm_tn(jnp.concatenate([uu, vv], axis=0), hh) for uu, vv, hh in zip(u, vs, ld(h2_ref))]
    for (j, g), s, f in zip(units, s0, full):
        own = f[(per - 1) * hd:, :]
        for i in reversed(range(per - 1)):
            own = jnp.where(blk == i, f[i * hd:(i + 1) * hd, :], own)
        st_ref[j, g] = s * egl_ref[j, :, g * LANES:(g + 1) * LANES] + own
    yield
    bvs = ld(bv_ref)
    gates = ld(g_ref)
    for i, (j, g) in enumerate(units):
        sl = slice(g * LANES, (g + 1) * LANES)
        yc = y[i] - _head_sums(y[i], hd) * (1.0 / hd)
        var = _head_sums(yc * yc, hd) * (1.0 / hd)
        yy = yc * lax.rsqrt(var + GN_EPS) * lng_ref[:, sl] + lnb_ref[:, sl]
        o_ref[j, o_r0:o_r0 + clen, sl] = (yy + bvs[i]) * gates[i]
        if i % (heads // per) == heads // per - 1:
            yield


def _rwkv_step_kernel(heads, hd, dw, da, nb, clen, nsub, has_stack, *refs):
    refs = list(refs)
    p_ref = refs.pop(0)
    pn_ref = refs.pop(0) if nsub == 2 else None
    shift_ref, s0_ref = refs[:2]
    prm = refs[2:11]
    lng_ref, lnb_ref = refs[11:13]
    rest = refs[13 + (1 if has_stack else 0):]
    o_ref, nshift_ref, ns_ref, st_ref, prev_ref, pf_ref = rest[:6]
    ops_a = rest[6:14]
    ops_b = rest[14:22]
    c = pl.program_id(1)
    per = LANES // hd
    groups = [(j, g) for j in range(nb) for g in range(heads // per)]

    @pl.when(c == 0)
    def _():
        prev_ref[...] = shift_ref[...]
        for j, g in groups:
            st_ref[j, g] = jnp.concatenate([s0_ref[j, g * per + i] for i in range(per)], axis=1)

    def prep(src, r0, ops):
        return _rwkv_prep(heads, hd, dw, da, nb, clen, src, r0, prm, prev_ref, pf_ref, ops)

    def chain(ops, o_r0):
        return _rwkv_chain(heads, hd, nb, clen, ops, st_ref, lng_ref, lnb_ref, o_ref, o_r0)

    if nsub == 1:
        _run_streams(prep(p_ref, 0, ops_a))
        _run_streams(chain(ops_a, 0))
    else:
        @pl.when(c == 0)
        def _():
            _run_streams(prep(p_ref, 0, ops_a))

        _run_streams(chain(ops_a, 0), prep(p_ref, clen, ops_b))
        _run_streams(chain(ops_b, clen), prep(pn_ref, 0, ops_a))

    @pl.when(c == pl.num_programs(1) - 1)
    def _():
        nshift_ref[...] = prev_ref[...]
        for j, g in groups:
            for i in range(per):
                ns_ref[j, g * per + i] = st_ref[j, g, :, i * hd:(i + 1) * hd]


def _rwkv_mix(pa, shift, s0, stacked, P, l, nseq, tlen, chunk, nb):
    ap = pa.shape[1]
    depth, _, heads, hd, _ = s0.shape
    aw = heads * hd
    dw = P['rwkv_w2'].shape[1]
    da = P['rwkv_a2'].shape[1]
    nch = tlen // chunk
    nsub = 2 if nch % 2 == 0 else 1
    assert nsub == 2 or nch == 1
    nstep = nch // nsub
    ngrp = nseq // nb
    shift = shift.reshape(depth, ngrp, nb, ap)
    pa = pa.reshape(nseq, tlen, ap)

    def vec(a):
        return pl.BlockSpec((None,) + a.shape[1:], lambda b, c: (l,) + (0,) * (a.ndim - 1))

    names = ['rwkv_mu', 'rwkv_w0', 'rwkv_w2', 'rwkv_a0', 'rwkv_a2', 'rwkv_g2', 'rwkv_k_k',
             'rwkv_k_a', 'rwkv_r_k', 'rwkv_ln_g', 'rwkv_ln_b']
    params = [P[n] for n in names]
    args = [pa]
    in_specs = [pl.BlockSpec((nb, nsub * chunk, ap), lambda b, c: (b, c, 0))]
    if nsub == 2:
        args.append(pa)
        in_specs.append(pl.BlockSpec(
            (nb, chunk, ap), lambda b, c: (b, jnp.minimum(2 * c + 2, nch - 1), 0)))
    args += [shift, s0] + params
    in_specs += [
        pl.BlockSpec((None, None, nb, ap), lambda b, c: (l, b, 0, 0)),
        pl.BlockSpec((None, nb, heads, hd, hd), lambda b, c: (l, b, 0, 0, 0)),
    ] + [vec(a) for a in params]
    aliases = {}
    if stacked is not None:
        aliases = {len(args): 2}
        args.append(stacked)
        in_specs.append(pl.BlockSpec(memory_space=pl.ANY))

    def operand_set():
        f = lambda r: pltpu.VMEM((nb, r, aw), F32)
        return [f(2 * chunk), f(chunk), f(chunk), f(2 * chunk), f(chunk), f(chunk), f(chunk), f(1)]

    o, n_shift, stacked = pl.pallas_call(
        functools.partial(_rwkv_step_kernel, heads, hd, dw, da, nb, chunk, nsub,
                          stacked is not None),
        grid=(ngrp, nstep),
        in_specs=in_specs,
        out_specs=[
            pl.BlockSpec((nb, nsub * chunk, aw), lambda b, c: (b, c, 0)),
            pl.BlockSpec((None, nb, ap), lambda b, c: (b, 0, 0)),
            pl.BlockSpec((None, nb, heads, hd, hd), lambda b, c: (l, b, 0, 0, 0)),
        ],
        out_shape=[
            jax.ShapeDtypeStruct((nseq, tlen, aw), F32),
            jax.ShapeDtypeStruct((ngrp, nb, ap), F32),
            jax.ShapeDtypeStruct((depth, nseq, heads, hd, hd), F32),
        ],
        scratch_shapes=[pltpu.VMEM((nb, aw // LANES, hd, LANES), F32), pltpu.VMEM((nb, ap), F32),
                        pltpu.VMEM((nb * chunk, ap), F32)] + operand_set() + operand_set(),
        input_output_aliases=aliases,
        compiler_params=_cparams("parallel", "arbitrary"),
        name="rwkv7",
    )(*args)
    return o.reshape(nseq * tlen, aw), n_shift.reshape(nseq, ap), stacked


def _lru_kernel(bw, seqs, p_ref, cs_ref, h0_ref, cw_ref, cb_ref, wa_ref, ba_ref, wx_ref, bx_ref,
                lam_ref, o_ref, nconv_ref, nh_ref, tail_ref, h_ref):
    c = pl.program_id(1)
    rows = p_ref.shape[0]
    ncv = cs_ref.shape[1]
    nseg = rows // SUBLANES

    if seqs == 1:
        @pl.when(c == 0)
        def _():
            tail_ref[...] = jnp.zeros_like(tail_ref)
            tail_ref[SUBLANES - ncv:, :] = cs_ref[0]
            h_ref[...] = h0_ref[0]
    else:
        tail_ref[...] = jnp.zeros_like(tail_ref)
        for j in range(seqs):
            tail_ref[(j + 1) * SUBLANES - ncv:(j + 1) * SUBLANES, :] = cs_ref[j]

    xb = p_ref[:, :bw]
    gb = p_ref[:, bw:]
    rowi = lax.broadcasted_iota(jnp.int32, (rows, 1), 0)
    pos = rowi & (SUBLANES - 1)

    def seg_roll(x, d):
        return pltpu.roll(x.reshape(nseg, SUBLANES, bw), d, 1).reshape(rows, bw)

    if seqs == 1 and rows > SUBLANES:
        before = jnp.concatenate([tail_ref[...], xb[:rows - SUBLANES, :]], axis=0)
    else:
        before = tail_ref[...]
    xc = cb_ref[...] + xb * cw_ref[ncv:ncv + 1, :]
    for d in range(1, ncv + 1):
        sh = jnp.where(pos >= d, seg_roll(xb, d), seg_roll(before, d))
        xc = xc + sh * cw_ref[ncv - d:ncv - d + 1, :]
    if seqs == 1:
        tail_ref[...] = xb[rows - SUBLANES:, :]

    rg = _sigmoid(_mm(xc, wa_ref[...]) + ba_ref[...])
    ig = _sigmoid(_mm(xc, wx_ref[...]) + bx_ref[...])
    log_a = -LRU_C * rg * _softplus(-lam_ref[...])
    a = jnp.exp(log_a)
    b = jnp.sqrt(-jnp.tanh(log_a) * (1.0 + a * a)) * ig * xc

    d = 1
    while d < SUBLANES:
        keep = pos >= d
        a_sh = jnp.where(keep, seg_roll(a, d), 1.0)
        b_sh = jnp.where(keep, seg_roll(b, d), 0.0)
        b = a * b_sh + b
        a = a_sh * a
        d *= 2
    pieces = []
    hc = h_ref[...] if seqs == 1 else None
    for s in range(nseg):
        rs = slice(s * SUBLANES, (s + 1) * SUBLANES)
        h_s = a[rs] * (hc if seqs == 1 else h0_ref[s]) + b[rs]
        hc = h_s[SUBLANES - 1:, :]
        pieces.append(h_s)
    h = pieces[0] if nseg == 1 else jnp.concatenate(pieces, axis=0)
    o_ref[...] = h * jax.nn.gelu(gb)

    if seqs == 1:
        h_ref[...] = hc

        @pl.when(c == pl.num_programs(1) - 1)
        def _():
            nconv_ref[0] = xb[rows - ncv:, :]
            nh_ref[0] = hc
    else:
        for s in range(seqs):
            nconv_ref[s] = xb[(s + 1) * SUBLANES - ncv:(s + 1) * SUBLANES, :]
            nh_ref[s] = pieces[s][SUBLANES - 1:, :]


def _lru(pb, conv, h0, P, l, nseq, tlen, tile, seqs):
    bw = h0.shape[-1]
    ncv = conv.shape[2]
    nt = tlen // tile
    assert seqs == 1 or (nt == 1 and tile == SUBLANES)
    rows = seqs * tile

    def vec(a):
        return pl.BlockSpec((None,) + a.shape[1:], lambda b, c: (l,) + (0,) * (a.ndim - 1))

    names = ['lru_conv_w', 'lru_conv_b', 'lru_wa_bd', 'lru_ba', 'lru_wx_bd', 'lru_bx', 'lru_lam']
    params = [P[n] for n in names]
    return pl.pallas_call(
        functools.partial(_lru_kernel, bw, seqs),
        grid=(nseq // seqs, nt),
        in_specs=[
            pl.BlockSpec((rows, 2 * bw), lambda b, c: (b * nt + c, 0)),
            pl.BlockSpec((None, seqs, ncv, bw), lambda b, c: (l, b, 0, 0)),
            pl.BlockSpec((None, seqs, 1, bw), lambda b, c: (l, b, 0, 0)),
        ] + [vec(a) for a in params],
        out_specs=[
            pl.BlockSpec((rows, bw), lambda b, c: (b * nt + c, 0)),
            pl.BlockSpec((seqs, ncv, bw), lambda b, c: (b, 0, 0)),
            pl.BlockSpec((seqs, 1, bw), lambda b, c: (b, 0, 0)),
        ],
        out_shape=[
            jax.ShapeDtypeStruct((nseq * tlen, bw), F32),
            jax.ShapeDtypeStruct((nseq, ncv, bw), F32),
            jax.ShapeDtypeStruct((nseq, 1, bw), F32),
        ],
        scratch_shapes=[pltpu.VMEM((SUBLANES if seqs == 1 else rows, bw), F32),
                        pltpu.VMEM((1, bw), F32)],
        compiler_params=_cparams("parallel", "arbitrary"),
        name="rglru",
    )(pb, conv, h0, *params)


def _hgrn_kernel(heads, kd, vd, sub, nb, clen, p_ref, s0_ref, lb_ref, ng_ref, *rest):
    o_ref, ns_ref, st_ref = rest[-3:]
    c = pl.program_id(1)
    kw = heads * kd
    vw = heads * vd
    pairs = [(j, h) for j in range(nb) for h in range(heads)]

    @pl.when(c == 0)
    def _():
        for j, h in pairs:
            st_ref[j, h] = s0_ref[j, h].T

    p = p_ref[0] if nb == 1 else jnp.concatenate([p_ref[j] for j in range(nb)], axis=0)
    lb = lb_ref[...]
    q = _silu(p[:, :kw])
    f = lb + (1.0 - lb) * _sigmoid(p[:, kw:2 * kw])
    logf = jnp.log(f)
    kf = 1.0 - f
    v = p[:, 2 * kw:2 * kw + vw]
    gz = p[:, 2 * kw + vw:]

    g = _cumsum_rows(logf, clen)
    qg = q * jnp.exp(g)
    glast = [g[(j + 1) * clen - 1:(j + 1) * clen, :] for j in range(nb)]
    khat = [kf[j * clen:(j + 1) * clen, :] * jnp.exp(glast[j] - g[j * clen:(j + 1) * clen, :])
            for j in range(nb)]
    eglast = [jnp.exp(x) for x in glast]

    nsub = clen // sub
    srow = lax.broadcasted_iota(jnp.int32, (sub, 1), 0)
    lane = lax.broadcasted_iota(jnp.int32, (sub, 128), 1)

    st = [st_ref[j, h] for j, h in pairs]
    o_inter = [_mm_nt(qg[j * clen:(j + 1) * clen, h * kd:(h + 1) * kd], s)
               for (j, h), s in zip(pairs, st)]
    st_new = [s * eglast[j][:, h * kd:(h + 1) * kd]
              + _mm_tn(v[j * clen:(j + 1) * clen, h * vd:(h + 1) * vd],
                       khat[j][:, h * kd:(h + 1) * kd])
              for (j, h), s in zip(pairs, st)]
    for (j, h), s in zip(pairs, st_new):
        st_ref[j, h] = s

    blocks = [(pi, j, h, i) for pi, (j, h) in enumerate(pairs) for i in range(nsub)]
    rel = []
    for pi, j, h, i in blocks:
        if i == 0:
            continue
        ks = slice(h * kd, (h + 1) * kd)
        r0 = j * clen + i * sub
        gref = g[r0 - 1:r0, ks]
        q_rel = q[r0:r0 + sub, ks] * jnp.exp(g[r0:r0 + sub, ks] - gref)
        k_rel = kf[j * clen:r0, ks] * jnp.exp(gref - g[j * clen:r0, ks])
        rel.append(((pi, i), q_rel, k_rel, v[j * clen:r0, h * vd:(h + 1) * vd]))
    att_off = [(key, _mm_nt(q_rel, k_rel), vv) for key, q_rel, k_rel, vv in rel]
    o_off = {key: _mm(att, vv) for key, att, vv in att_off}

    fqk = []
    for pi, j, h, i in blocks:
        ks = slice(h * kd, (h + 1) * kd)
        r0 = j * clen + i * sub
        fqk.append((f[r0:r0 + sub, ks], q[r0:r0 + sub, ks], kf[r0:r0 + sub, ks]))
    att_t = [jnp.zeros((sub, 128), F32) for _ in blocks]
    dec = [jnp.zeros((sub, kd), F32) for _ in blocks]
    for t in range(sub):
        dec = [jnp.where(srow == t, k_i, d * f_i[t:t + 1, :]) for d, (f_i, q_i, k_i) in zip(dec, fqk)]
        cols = [jnp.sum(q_i[t:t + 1, :] * d, axis=-1, keepdims=True)
                for d, (f_i, q_i, k_i) in zip(dec, fqk)]
        att_t = [jnp.where(lane == t, col, a) for col, a in zip(cols, att_t)]
    o_diag = [_mm_tn(a[:, :sub],
                     v[j * clen + i * sub:j * clen + (i + 1) * sub, h * vd:(h + 1) * vd])
              for a, (pi, j, h, i) in zip(att_t, blocks)]

    for pi, (j, h) in enumerate(pairs):
        vs = slice(h * vd, (h + 1) * vd)
        parts = []
        for i in range(nsub):
            o_i = o_diag[pi * nsub + i]
            if i > 0:
                o_i = o_i + o_off[(pi, i)]
            parts.append(o_i)
        o = o_inter[pi] + (parts[0] if nsub == 1 else jnp.concatenate(parts, axis=0))
        o = o * lax.rsqrt(jnp.mean(o * o, axis=-1, keepdims=True) + RMS_EPS) * ng_ref[:, vs]
        o_ref[j, :, vs] = o * _silu(gz[j * clen:(j + 1) * clen, vs])

    @pl.when(c == pl.num_programs(1) - 1)
    def _():
        for j, h in pairs:
            ns_ref[j, h] = st_ref[j, h].T


def _hgrn(pc, s0, stacked, lb, ng, l, nseq, tlen, chunk, nb):
    depth, _, heads, kd, vd = s0.shape
    width = pc.shape[1]
    nch = tlen // chunk
    sub = min(HGRN_SUB, chunk)
    args = [pc.reshape(nseq, tlen, width), s0, lb, ng]
    in_specs = [
        pl.BlockSpec((nb, chunk, width), lambda b, c: (b, c, 0)),
        pl.BlockSpec((None, nb, heads, kd, vd), lambda b, c: (l, b, 0, 0, 0)),
        pl.BlockSpec((None, 1, heads * kd), lambda b, c: (l, 0, 0)),
        pl.BlockSpec((None, 1, heads * vd), lambda b, c: (l, 0, 0)),
    ]
    aliases = {}
    if stacked is not None:
        aliases = {len(args): 1}
        args.append(stacked)
        in_specs.append(pl.BlockSpec(memory_space=pl.ANY))
    o, stacked = pl.pallas_call(
        functools.partial(_hgrn_kernel, heads, kd, vd, sub, nb, chunk),
        grid=(nseq // nb, nch),
        in_specs=in_specs,
        out_specs=[
            pl.BlockSpec((nb, chunk, heads * vd), lambda b, c: (b, c, 0)),
            pl.BlockSpec((None, nb, heads, kd, vd), lambda b, c: (l, b, 0, 0, 0)),
        ],
        out_shape=[
            jax.ShapeDtypeStruct((nseq, tlen, heads * vd), F32),
            jax.ShapeDtypeStruct((depth, nseq, heads, kd, vd), F32),
        ],
        scratch_shapes=[pltpu.VMEM((nb, heads, vd, kd), F32)],
        input_output_aliases=aliases,
        compiler_params=_cparams("parallel", "arbitrary"),
        name="hgrn2",
    )(*args)
    return o.reshape(nseq * tlen, heads * vd), stacked


def _run_trunk(x, s_rwkv, s_shift, s_lru, s_conv, s_hgrn, P):
    nseq, tlen, d = x.shape
    depth = s_rwkv.shape[0]
    x = x.reshape(nseq * tlen, d)
    lru4 = s_lru[:, :, None, :]
    rw_chunk = min(RWKV_CHUNK, tlen)
    rw_nb = _tile(nseq, max(RWKV_SEQS, RWKV_CHUNK // rw_chunk))
    hg_chunk = min(HGRN_CHUNK, tlen)
    hg_nb = _tile(nseq, max(HGRN_SEQS, HGRN_CHUNK // hg_chunk))
    lru_tile = min(LRU_TILE, tlen)
    lru_seqs = _tile(nseq, LRU_SEQS) if lru_tile == tlen == SUBLANES else 1
    outs = ([], [], [])
    n_rwkv = n_hgrn = None
    for l in range(depth):
        x = _ffn(x, P['ffn1_pre_g'], P['ffn1_wg'], P['ffn1_wu'], P['ffn1_wd'], P['ffn1_post_g'], l)
        pa, pb, pc = _in_proj(x, P['mix_pre_g'], P['w_in'], P['w_in_sections'], l)
        oa, n_shift, n_rwkv = _rwkv_mix(pa, s_shift, s_rwkv, n_rwkv, P, l, nseq, tlen, rw_chunk,
                                        rw_nb)
        ob, n_conv, n_lru = _lru(pb, s_conv, lru4, P, l, nseq, tlen, lru_tile, lru_seqs)
        oc, n_hgrn = _hgrn(pc, s_hgrn, n_hgrn, P['hgrn_lb'], P['hgrn_norm_g'], l, nseq, tlen,
                           hg_chunk, hg_nb)
        x = _merge(x, P['mix_pre_g'], P['w_in'], sum(P['w_in_sections']), oa, ob, oc, P['proj_a'],
                   P['proj_b'], P['proj_c'], P['w_out'], P['mix_post_g'], l)
        x = _ffn(x, P['ffn2_pre_g'], P['ffn2_wg'], P['ffn2_wu'], P['ffn2_wd'], P['ffn2_post_g'], l)
        for lst, t in zip(outs, (n_shift, n_lru[:, 0], n_conv)):
            lst.append(t)
    n_shift, n_lru, n_conv = (jnp.stack(lst, axis=0) for lst in outs)
    return x.reshape(nseq, tlen, d), (n_rwkv, n_shift, n_lru, n_conv, n_hgrn)


def _block_diag(w):
    depth, g, i, j = w.shape
    eye = jnp.eye(g, dtype=w.dtype)
    return jnp.einsum('lgij,gh->lgihj', w, eye).reshape(depth, g * i, g * j)


def _prepare_params(raw, a_proj, b_width, c_kwidth, c_width):
    P = {}
    for n in ('ffn1_wg', 'ffn1_wu', 'ffn1_wd', 'ffn2_wg', 'ffn2_wu', 'ffn2_wd', 'proj_a', 'proj_b',
              'proj_c', 'w_out', 'w_in', 'rwkv_w2', 'rwkv_a2', 'rwkv_g2'):
        P[n] = raw[n].astype(BF16)
    P['w_in_sections'] = (a_proj, 2 * b_width, 2 * c_kwidth + 2 * c_width)
    for n in ('ffn1_pre_g', 'ffn1_post_g', 'mix_pre_g', 'mix_post_g', 'ffn2_pre_g', 'ffn2_post_g',
              'rwkv_mu', 'rwkv_w0', 'rwkv_a0', 'rwkv_k_k', 'rwkv_k_a', 'rwkv_ln_g', 'rwkv_ln_b',
              'lru_conv_b', 'lru_ba', 'lru_bx', 'lru_lam', 'hgrn_norm_g'):
        P[n] = raw[n][:, None, :]
    depth = raw['rwkv_r_k'].shape[0]
    P['rwkv_r_k'] = raw['rwkv_r_k'].reshape(depth, 1, -1)
    P['lru_conv_w'] = raw['lru_conv_w']
    P['lru_wa_bd'] = _block_diag(raw['lru_wa']).astype(BF16)
    P['lru_wx_bd'] = _block_diag(raw['lru_wx']).astype(BF16)
    lb_cum = jnp.cumsum(jax.nn.softmax(raw['hgrn_lb_logits'].astype(F32), axis=0), axis=0)
    P['hgrn_lb'] = (lb_cum - lb_cum[0])[:, None, :]
    return P


def kernel(x_prompt, x_sample, state_rwkv, state_shift, state_lru, state_conv, state_hgrn, ffn1_pre_g, ffn1_post_g, ffn1_wg, ffn1_wu, ffn1_wd, mix_pre_g, mix_post_g, w_in, rwkv_mu, rwkv_w0, rwkv_w2, rwkv_a0, rwkv_a2, rwkv_g2, rwkv_k_k, rwkv_k_a, rwkv_r_k, rwkv_ln_g, rwkv_ln_b, lru_conv_w, lru_conv_b, lru_wa, lru_ba, lru_wx, lru_bx, lru_lam, hgrn_lb_logits, hgrn_norm_g, proj_a, proj_b, proj_c, w_out, ffn2_pre_g, ffn2_post_g, ffn2_wg, ffn2_wu, ffn2_wd):
    raw = dict(
        ffn1_pre_g=ffn1_pre_g, ffn1_post_g=ffn1_post_g, ffn1_wg=ffn1_wg, ffn1_wu=ffn1_wu,
        ffn1_wd=ffn1_wd, mix_pre_g=mix_pre_g, mix_post_g=mix_post_g, w_in=w_in, rwkv_mu=rwkv_mu,
        rwkv_w0=rwkv_w0, rwkv_w2=rwkv_w2, rwkv_a0=rwkv_a0, rwkv_a2=rwkv_a2, rwkv_g2=rwkv_g2,
        rwkv_k_k=rwkv_k_k, rwkv_k_a=rwkv_k_a, rwkv_r_k=rwkv_r_k, rwkv_ln_g=rwkv_ln_g,
        rwkv_ln_b=rwkv_ln_b, lru_conv_w=lru_conv_w, lru_conv_b=lru_conv_b, lru_wa=lru_wa,
        lru_ba=lru_ba, lru_wx=lru_wx, lru_bx=lru_bx, lru_lam=lru_lam,
        hgrn_lb_logits=hgrn_lb_logits, hgrn_norm_g=hgrn_norm_g, proj_a=proj_a, proj_b=proj_b,
        proj_c=proj_c, w_out=w_out, ffn2_pre_g=ffn2_pre_g, ffn2_post_g=ffn2_post_g,
        ffn2_wg=ffn2_wg, ffn2_wu=ffn2_wu, ffn2_wd=ffn2_wd)
    depth, _, heads, hd, _ = state_rwkv.shape
    a_proj = state_shift.shape[-1]
    b_width = state_lru.shape[-1]
    _, _, c_heads, c_kd, c_vd = state_hgrn.shape
    P = _prepare_params(raw, a_proj, b_width, c_heads * c_kd, c_heads * c_vd)
    dt = state_rwkv.dtype
    bp = x_prompt.shape[0]
    y_prompt, p_st = _run_trunk(
        x_prompt,
        jnp.zeros((depth, bp, heads, hd, hd), dt),
        jnp.zeros((depth, bp, a_proj), dt),
        jnp.zeros((depth, bp, b_width), dt),
        jnp.zeros((depth, bp) + state_conv.shape[2:], dt),
        jnp.zeros((depth, bp, c_heads, c_kd, c_vd), dt),
        P)
    y_sample, s_st = _run_trunk(x_sample, state_rwkv, state_shift, state_lru, state_conv,
                                state_hgrn, P)
    return (y_prompt, y_sample) + tuple(t.astype(dt) for t in p_st) + tuple(t.astype(dt) for t in s_st)
```

```python
import functools

import jax
import jax.numpy as jnp
from jax import lax
from jax.experimental import pallas as pl
from jax.experimental.pallas import tpu as pltpu

F32 = jnp.float32
BF16 = jnp.bfloat16
RMS_EPS = 1e-6
GN_EPS = 64e-5
LRU_C = 8.0
KK_EPS = 1e-12

VMEM_LIMIT_BYTES = 48 * 1024 * 1024
SUBLANES = 8
HGRN_SUB = 8
RWKV_CHUNK = 64
HGRN_CHUNK = 64
RWKV_SEQS = 4
HGRN_SEQS = 2
LRU_TILE = 512
LRU_SEQS = 16
IN_PROJ_ROWS = 512
FFN_ROWS = 512
FFN_CHUNK = 256


def _cparams(*sem):
    return pltpu.CompilerParams(dimension_semantics=sem, vmem_limit_bytes=VMEM_LIMIT_BYTES)


def _mm(a, b):
    return jnp.dot(a.astype(BF16), b.astype(BF16), preferred_element_type=F32)


def _mm_nt(a, b):
    return lax.dot_general(a.astype(BF16), b.astype(BF16), (((1,), (1,)), ((), ())),
                           preferred_element_type=F32)


def _mm_tn(a, b):
    return lax.dot_general(a.astype(BF16), b.astype(BF16), (((0,), (0,)), ((), ())),
                           preferred_element_type=F32)


def _split3(x):
    hi = x.astype(BF16)
    r1 = x - hi.astype(F32)
    mid = r1.astype(BF16)
    lo = (r1 - mid.astype(F32)).astype(BF16)
    return hi, mid, lo


def _cumsum_rows(x, seg=None):
    c = x.shape[0]
    row = lax.broadcasted_iota(jnp.int32, (c, c), 0)
    col = lax.broadcasted_iota(jnp.int32, (c, c), 1)
    tri = col <= row
    if seg is not None and seg < c:
        sh = seg.bit_length() - 1
        tri = tri & (lax.shift_right_logical(row, sh) == lax.shift_right_logical(col, sh))
    tri = tri.astype(BF16)
    hi, mid, lo = _split3(x)
    out = jnp.dot(tri, lo, preferred_element_type=F32)
    out = out + jnp.dot(tri, mid, preferred_element_type=F32)
    return out + jnp.dot(tri, hi, preferred_element_type=F32)


def _rms(x, g):
    return x * lax.rsqrt(jnp.mean(x * x, axis=-1, keepdims=True) + RMS_EPS) * g


def _softplus(x):
    return jnp.maximum(x, 0.0) + jnp.log1p(jnp.exp(-jnp.abs(x)))


def _sigmoid(x):
    return 0.5 * jnp.tanh(0.5 * x) + 0.5


def _silu(x):
    return x * _sigmoid(x)


LANES = 128


def _lane_block(rows, width, size):
    lane = lax.broadcasted_iota(jnp.int32, (rows, width), 1)
    return lax.shift_right_logical(lane, size.bit_length() - 1)


def _head_sums(x, hd):
    rows, width = x.shape
    per = LANES // hd
    blk = _lane_block(rows, LANES, hd)
    out = []
    for g in range(width // LANES):
        xg = x[:, g * LANES:(g + 1) * LANES]
        acc = None
        for i in reversed(range(per)):
            col = jnp.sum(jnp.where(blk == i, xg, 0.0), axis=-1, keepdims=True)
            acc = jnp.broadcast_to(col, (rows, LANES)) if acc is None else jnp.where(blk == i, col, acc)
        out.append(acc)
    return out[0] if len(out) == 1 else jnp.concatenate(out, axis=1)


def _blockdiag(x, size):
    r, width = x.shape
    blk = _lane_block(r, width, size)
    return jnp.concatenate([jnp.where(blk == i, x, 0.0) for i in range(width // size)], axis=0)


def _run_streams(*gens):
    live = list(gens)
    while live:
        for g in list(live):
            try:
                next(g)
            except StopIteration:
                live.remove(g)


def _tile(n, pref):
    t = min(n, pref)
    while n % t:
        t //= 2
    return t


def _in_proj_kernel(sections, x_ref, g_ref, w_ref, *o_refs):
    xn = _rms(x_ref[...], g_ref[...]).astype(BF16)
    start = 0
    for width, o_ref in zip(sections, o_refs):
        o_ref[...] = jnp.dot(xn, w_ref[:, start:start + width], preferred_element_type=F32)
        start += width


def _in_proj(x, g, w, sections, l):
    m, d = x.shape
    tm = _tile(m, IN_PROJ_ROWS)

    def resident(a, cols):
        return pl.BlockSpec((None, a.shape[1], cols), lambda i: (l, 0, 0),
                            pipeline_mode=pl.Buffered(1))

    return pl.pallas_call(
        functools.partial(_in_proj_kernel, sections),
        grid=(m // tm,),
        in_specs=[pl.BlockSpec((tm, d), lambda i: (i, 0)), resident(g, d),
                  resident(w, sum(sections))],
        out_specs=[pl.BlockSpec((tm, n), lambda i: (i, 0)) for n in sections],
        out_shape=[jax.ShapeDtypeStruct((m, n), F32) for n in sections],
        compiler_params=_cparams("parallel"),
        name="in_proj",
    )(x, g, w)


def _ffn_kernel(cw, x_ref, gpre_ref, wg_ref, wu_ref, wd_ref, gpost_ref, o_ref, h_ref):
    ff = wg_ref.shape[-1]
    xn = _rms(x_ref[...], gpre_ref[...]).astype(BF16)

    def gate_up(i):
        cs = slice(i * cw, (i + 1) * cw)
        return (jnp.dot(xn, wg_ref[:, cs], preferred_element_type=F32),
                jnp.dot(xn, wu_ref[:, cs], preferred_element_type=F32))

    nchunk = ff // cw
    cur = gate_up(0)
    for i in range(nchunk):
        nxt = gate_up(i + 1) if i + 1 < nchunk else None
        h_ref[:, i * cw:(i + 1) * cw] = (_silu(cur[0]) * cur[1]).astype(BF16)
        cur = nxt
    y = jnp.dot(h_ref[...], wd_ref[...], preferred_element_type=F32)
    o_ref[...] = x_ref[...] + 0.5 * _rms(y, gpost_ref[...])


def _ffn(x, gpre, wg, wu, wd, gpost, l):
    m, d = x.shape
    ff = wg.shape[-1]
    tm = _tile(m, FFN_ROWS)
    cw = FFN_CHUNK if ff % FFN_CHUNK == 0 else ff

    def resident(a):
        return pl.BlockSpec((None,) + a.shape[1:], lambda i: (l,) + (0,) * (a.ndim - 1),
                            pipeline_mode=pl.Buffered(1))

    return pl.pallas_call(
        functools.partial(_ffn_kernel, cw),
        grid=(m // tm,),
        in_specs=[pl.BlockSpec((tm, d), lambda i: (i, 0)), resident(gpre), resident(wg),
                  resident(wu), resident(wd), resident(gpost)],
        out_specs=pl.BlockSpec((tm, d), lambda i: (i, 0)),
        out_shape=jax.ShapeDtypeStruct((m, d), F32),
        scratch_shapes=[pltpu.VMEM((tm, ff), BF16)],
        compiler_params=_cparams("parallel"),
        name="ffn",
    )(x, gpre, wg, wu, wd, gpost)


def _merge_kernel(goff, x_ref, gpre_ref, win_ref, oa_ref, ob_ref, oc_ref, pa_ref, pb_ref, pc_ref,
                  wo_ref, g_ref, o_ref):
    d = x_ref.shape[-1]
    x = x_ref[...]
    xn = _rms(x, gpre_ref[...]).astype(BF16)
    merged = None
    for i, (o_b, p_b) in enumerate(((oa_ref, pa_ref), (ob_ref, pb_ref), (oc_ref, pc_ref))):
        gate = jnp.dot(xn, win_ref[:, goff + i * d:goff + (i + 1) * d], preferred_element_type=F32)
        term = _sigmoid(gate) * _mm(o_b[...], p_b[...])
        merged = term if merged is None else merged + term
    y = _mm(merged, wo_ref[...])
    o_ref[...] = x + _rms(y, g_ref[...])


def _merge(x, gpre, w_in, goff, oa, ob, oc, pa, pb, pc, wo, g, l):
    m, d = x.shape
    tm = _tile(m, 512)

    def rows(w):
        return pl.BlockSpec((tm, w), lambda i: (i, 0))

    def whole(a):
        return pl.BlockSpec((None,) + a.shape[1:], lambda i: (l,) + (0,) * (a.ndim - 1),
                            pipeline_mode=pl.Buffered(1))

    return pl.pallas_call(
        functools.partial(_merge_kernel, goff),
        grid=(m // tm,),
        in_specs=[rows(d), whole(gpre), whole(w_in), rows(oa.shape[1]), rows(ob.shape[1]),
                  rows(oc.shape[1]), whole(pa), whole(pb), whole(pc), whole(wo), whole(g)],
        out_specs=rows(d),
        out_shape=jax.ShapeDtypeStruct((m, d), F32),
        compiler_params=_cparams("parallel"),
        name="merge_out",
    )(x, gpre, w_in, oa, ob, oc, pa, pb, pc, wo, g)


def _rwkv_prep(heads, hd, dw, da, nb, clen, p_ref, r0, prm, prev_ref, pf_ref, ops):
    (mu_ref, w0_ref, w2_ref, a0_ref, a2_ref, g2_ref, kk_ref, ka_ref, rk_ref) = prm
    l2_ref, bt_ref, kt_ref, h2_ref, v_ref, bv_ref, g_ref, egl_ref = ops
    aw = heads * hd
    rows = nb * clen
    seq = [slice(j * clen, (j + 1) * clen) for j in range(nb)]

    p = jnp.concatenate([p_ref[j, r0:r0 + clen, :] for j in range(nb)], axis=0)
    pf_ref[...] = pltpu.roll(p, 1, 0)
    for j in range(nb):
        pf_ref[j * clen:j * clen + 1, :] = prev_ref[j:j + 1, :]
        prev_ref[j:j + 1, :] = p_ref[j, r0 + clen - 1:r0 + clen, :]
    xs = p + mu_ref[...] * (pf_ref[...] - p)
    yield
    r = xs[:, :aw]
    k = xs[:, aw:2 * aw]
    v = xs[:, 2 * aw:3 * aw]
    w1 = xs[:, 3 * aw:3 * aw + dw]
    a1 = xs[:, 3 * aw + dw:3 * aw + dw + da]
    g1 = xs[:, 3 * aw + dw + da:]
    w = -_softplus(-(w0_ref[...] + _mm(jnp.tanh(w1), w2_ref[...]))) - 0.5
    logw = -jnp.exp(w)
    yield
    a_icl = _sigmoid(a0_ref[...] + _mm(a1, a2_ref[...]))
    gate = _mm(_sigmoid(g1), g2_ref[...])
    for j in range(nb):
        g_ref[j] = gate[seq[j]]
    yield
    kkraw = k * kk_ref[...]
    k2 = k * (1.0 + (a_icl - 1.0) * ka_ref[...])
    rkk = r * k2 * rk_ref[...]
    ginc = _cumsum_rows(logw, clen)
    yield
    eg = jnp.exp(ginc)
    egx = jnp.exp(ginc - logw)
    einv = jnp.exp(-ginc)
    yield
    scale = lax.rsqrt(jnp.maximum(_head_sums(kkraw * kkraw, hd), KK_EPS * KK_EPS))
    yield
    bonus = _head_sums(rkk, hd)
    kk = kkraw * scale
    bvec = kk * a_icl
    yield
    for j in range(nb):
        l2_ref[j, :clen, :] = -kk[seq[j]] * egx[seq[j]]
        l2_ref[j, clen:, :] = r[seq[j]] * eg[seq[j]]
    yield
    for j in range(nb):
        bt_ref[j] = bvec[seq[j]] * einv[seq[j]]
        kt_ref[j] = k2[seq[j]] * einv[seq[j]]
    yield
    for j in range(nb):
        glast = ginc[(j + 1) * clen - 1:(j + 1) * clen, :]
        ehat = jnp.exp(glast - ginc[seq[j]])
        h2_ref[j, :clen, :] = bvec[seq[j]] * ehat
        h2_ref[j, clen:, :] = k2[seq[j]] * ehat
        egl_ref[j] = jnp.exp(glast)
    yield
    for j in range(nb):
        v_ref[j] = v[seq[j]]
        bv_ref[j] = bonus[seq[j]] * v[seq[j]]
    yield


def _rwkv_chain(heads, hd, nb, clen, ops, st_ref, lng_ref, lnb_ref, o_ref, o_r0):
    l2_ref, bt_ref, kt_ref, h2_ref, v_ref, bv_ref, g_ref, egl_ref = ops
    per = LANES // hd
    cw = per * clen
    units = [(j, g) for j in range(nb) for g in range(heads // per)]

    def ld(ref):
        return [ref[j, :, g * LANES:(g + 1) * LANES] for j, g in units]

    ri = lax.broadcasted_iota(jnp.int32, (2 * clen, cw), 0)
    cpos = lax.broadcasted_iota(jnp.int32, (2 * clen, cw), 1) & (clen - 1)
    mask2 = cpos < jnp.where(ri < clen, ri, ri - (clen - 1))
    eye = ((lax.broadcasted_iota(jnp.int32, (clen, cw), 1) & (clen - 1))
           == lax.broadcasted_iota(jnp.int32, (clen, cw), 0)).astype(F32)
    blk = _lane_block(hd, LANES, hd)

    lhs2 = ld(l2_ref)
    ab2 = [jnp.where(mask2, _mm_nt(x, _blockdiag(y, hd)), 0.0) for x, y in zip(lhs2, ld(bt_ref))]
    yield
    ak2 = [jnp.where(mask2, _mm_nt(x, _blockdiag(y, hd)), 0.0) for x, y in zip(lhs2, ld(kt_ref))]
    yield
    s0 = [st_ref[j, g] for j, g in units]
    vs = ld(v_ref)
    xy = [_mm_nt(x, _blockdiag(s, hd)) + _mm(a, _blockdiag(vv, hd))
          for x, s, a, vv in zip(lhs2, s0, ak2, vs)]
    yield
    pw = [m[:clen] for m in ab2]
    tinv = [eye + m for m in pw]
    span = 2
    while span < clen:
        pwd = [_blockdiag(m, clen) for m in pw]
        pw = [_mm(m, md) for m, md in zip(pw, pwd)]
        yield
        pwd = [_blockdiag(m, clen) for m in pw]
        tinv = [t + _mm(t, md) for t, md in zip(tinv, pwd)]
        yield
        span *= 2
    u = [_mm(t, _blockdiag(m[:clen], hd)) for t, m in zip(tinv, xy)]
    yield
    y = [m[clen:] + _mm(a[clen:], _blockdiag(uu, hd)) for m, a, uu in zip(xy, ab2, u)]
    yield
    full = [_mm_tn(jnp.concatenate([uu, vv], axis=0), hh) for uu, vv, hh in zip(u, vs, ld(h2_ref))]
    for (j, g), s, f in zip(units, s0, full):
        own = f[(per - 1) * hd:, :]
        for i in reversed(range(per - 1)):
            own = jnp.where(blk == i, f[i * hd:(i + 1) * hd, :], own)
        st_ref[j, g] = s * egl_ref[j, :, g * LANES:(g + 1) * LANES] + own
    yield
    bvs = ld(bv_ref)
    gates = ld(g_ref)
    for i, (j, g) in enumerate(units):
        sl = slice(g * LANES, (g + 1) * LANES)
        yc = y[i] - _head_sums(y[i], hd) * (1.0 / hd)
        var = _head_sums(yc * yc, hd) * (1.0 / hd)
        yy = yc * lax.rsqrt(var + GN_EPS) * lng_ref[:, sl] + lnb_ref[:, sl]
        o_ref[j, o_r0:o_r0 + clen, sl] = (yy + bvs[i]) * gates[i]
        if i % (heads // per) == heads // per - 1:
            yield


def _rwkv_step_kernel(heads, hd, dw, da, nb, clen, nsub, has_stack, *refs):
    refs = list(refs)
    p_ref = refs.pop(0)
    pn_ref = refs.pop(0) if nsub == 2 else None
    shift_ref, s0_ref = refs[:2]
    prm = refs[2:11]
    lng_ref, lnb_ref = refs[11:13]
    rest = refs[13 + (1 if has_stack else 0):]
    o_ref, nshift_ref, ns_ref, st_ref, prev_ref, pf_ref = rest[:6]
    ops_a = rest[6:14]
    ops_b = rest[14:22]
    c = pl.program_id(1)
    per = LANES // hd
    groups = [(j, g) for j in range(nb) for g in range(heads // per)]

    @pl.when(c == 0)
    def _():
        prev_ref[...] = shift_ref[...]
        for j, g in groups:
            st_ref[j, g] = jnp.concatenate([s0_ref[j, g * per + i] for i in range(per)], axis=1)

    def prep(src, r0, ops):
        return _rwkv_prep(heads, hd, dw, da, nb, clen, src, r0, prm, prev_ref, pf_ref, ops)

    def chain(ops, o_r0):
        return _rwkv_chain(heads, hd, nb, clen, ops, st_ref, lng_ref, lnb_ref, o_ref, o_r0)

    if nsub == 1:
        _run_streams(prep(p_ref, 0, ops_a))
        _run_streams(chain(ops_a, 0))
    else:
        @pl.when(c == 0)
        def _():
            _run_streams(prep(p_ref, 0, ops_a))

        _run_streams(chain(ops_a, 0), prep(p_ref, clen, ops_b))
        _run_streams(chain(ops_b, clen), prep(pn_ref, 0, ops_a))

    @pl.when(c == pl.num_programs(1) - 1)
    def _():
        nshift_ref[...] = prev_ref[...]
        for j, g in groups:
            for i in range(per):
                ns_ref[j, g * per + i] = st_ref[j, g, :, i * hd:(i + 1) * hd]


def _rwkv_mix(pa, shift, s0, stacked, P, l, nseq, tlen, chunk, nb):
    ap = pa.shape[1]
    depth, _, heads, hd, _ = s0.shape
    aw = heads * hd
    dw = P['rwkv_w2'].shape[1]
    da = P['rwkv_a2'].shape[1]
    nch = tlen // chunk
    nsub = 2 if nch % 2 == 0 else 1
    assert nsub == 2 or nch == 1
    nstep = nch // nsub
    ngrp = nseq // nb
    shift = shift.reshape(depth, ngrp, nb, ap)
    pa = pa.reshape(nseq, tlen, ap)

    def vec(a):
        return pl.BlockSpec((None,) + a.shape[1:], lambda b, c: (l,) + (0,) * (a.ndim - 1))

    names = ['rwkv_mu', 'rwkv_w0', 'rwkv_w2', 'rwkv_a0', 'rwkv_a2', 'rwkv_g2', 'rwkv_k_k',
             'rwkv_k_a', 'rwkv_r_k', 'rwkv_ln_g', 'rwkv_ln_b']
    params = [P[n] for n in names]
    args = [pa]
    in_specs = [pl.BlockSpec((nb, nsub * chunk, ap), lambda b, c: (b, c, 0))]
    if nsub == 2:
        args.append(pa)
        in_specs.append(pl.BlockSpec(
            (nb, chunk, ap), lambda b, c: (b, jnp.minimum(2 * c + 2, nch - 1), 0)))
    args += [shift, s0] + params
    in_specs += [
        pl.BlockSpec((None, None, nb, ap), lambda b, c: (l, b, 0, 0)),
        pl.BlockSpec((None, nb, heads, hd, hd), lambda b, c: (l, b, 0, 0, 0)),
    ] + [vec(a) for a in params]
    aliases = {}
    if stacked is not None:
        aliases = {len(args): 2}
        args.append(stacked)
        in_specs.append(pl.BlockSpec(memory_space=pl.ANY))

    def operand_set():
        f = lambda r: pltpu.VMEM((nb, r, aw), F32)
        return [f(2 * chunk), f(chunk), f(chunk), f(2 * chunk), f(chunk), f(chunk), f(chunk), f(1)]

    o, n_shift, stacked = pl.pallas_call(
        functools.partial(_rwkv_step_kernel, heads, hd, dw, da, nb, chunk, nsub,
                          stacked is not None),
        grid=(ngrp, nstep),
        in_specs=in_specs,
        out_specs=[
            pl.BlockSpec((nb, nsub * chunk, aw), lambda b, c: (b, c, 0)),
            pl.BlockSpec((None, nb, ap), lambda b, c: (b, 0, 0)),
            pl.BlockSpec((None, nb, heads, hd, hd), lambda b, c: (l, b, 0, 0, 0)),
        ],
        out_shape=[
            jax.ShapeDtypeStruct((nseq, tlen, aw), F32),
            jax.ShapeDtypeStruct((ngrp, nb, ap), F32),
            jax.ShapeDtypeStruct((depth, nseq, heads, hd, hd), F32),
        ],
        scratch_shapes=[pltpu.VMEM((nb, aw // LANES, hd, LANES), F32), pltpu.VMEM((nb, ap), F32),
                        pltpu.VMEM((nb * chunk, ap), F32)] + operand_set() + operand_set(),
        input_output_aliases=aliases,
        compiler_params=_cparams("parallel", "arbitrary"),
        name="rwkv7",
    )(*args)
    return o.reshape(nseq * tlen, aw), n_shift.reshape(nseq, ap), stacked


def _lru_kernel(bw, seqs, p_ref, cs_ref, h0_ref, cw_ref, cb_ref, wa_ref, ba_ref, wx_ref, bx_ref,
                lam_ref, o_ref, nconv_ref, nh_ref, tail_ref, h_ref):
    c = pl.program_id(1)
    rows = p_ref.shape[0]
    ncv = cs_ref.shape[1]
    nseg = rows // SUBLANES

    if seqs == 1:
        @pl.when(c == 0)
        def _():
            tail_ref[...] = jnp.zeros_like(tail_ref)
            tail_ref[SUBLANES - ncv:, :] = cs_ref[0]
            h_ref[...] = h0_ref[0]
    else:
        tail_ref[...] = jnp.zeros_like(tail_ref)
        for j in range(seqs):
            tail_ref[(j + 1) * SUBLANES - ncv:(j + 1) * SUBLANES, :] = cs_ref[j]

    xb = p_ref[:, :bw]
    gb = p_ref[:, bw:]
    rowi = lax.broadcasted_iota(jnp.int32, (rows, 1), 0)
    pos = rowi & (SUBLANES - 1)

    def seg_roll(x, d):
        return pltpu.roll(x.reshape(nseg, SUBLANES, bw), d, 1).reshape(rows, bw)

    if seqs == 1 and rows > SUBLANES:
        before = jnp.concatenate([tail_ref[...], xb[:rows - SUBLANES, :]], axis=0)
    else:
        before = tail_ref[...]
    xc = cb_ref[...] + xb * cw_ref[ncv:ncv + 1, :]
    for d in range(1, ncv + 1):
        sh = jnp.where(pos >= d, seg_roll(xb, d), seg_roll(before, d))
        xc = xc + sh * cw_ref[ncv - d:ncv - d + 1, :]
    if seqs == 1:
        tail_ref[...] = xb[rows - SUBLANES:, :]

    rg = _sigmoid(_mm(xc, wa_ref[...]) + ba_ref[...])
    ig = _sigmoid(_mm(xc, wx_ref[...]) + bx_ref[...])
    log_a = -LRU_C * rg * _softplus(-lam_ref[...])
    a = jnp.exp(log_a)
    b = jnp.sqrt(-jnp.tanh(log_a) * (1.0 + a * a)) * ig * xc

    d = 1
    while d < SUBLANES:
        keep = pos >= d
        a_sh = jnp.where(keep, seg_roll(a, d), 1.0)
        b_sh = jnp.where(keep, seg_roll(b, d), 0.0)
        b = a * b_sh + b
        a = a_sh * a
        d *= 2
    pieces = []
    hc = h_ref[...] if seqs == 1 else None
    for s in range(nseg):
        rs = slice(s * SUBLANES, (s + 1) * SUBLANES)
        h_s = a[rs] * (hc if seqs == 1 else h0_ref[s]) + b[rs]
        hc = h_s[SUBLANES - 1:, :]
        pieces.append(h_s)
    h = pieces[0] if nseg == 1 else jnp.concatenate(pieces, axis=0)
    o_ref[...] = h * jax.nn.gelu(gb)

    if seqs == 1:
        h_ref[...] = hc

        @pl.when(c == pl.num_programs(1) - 1)
        def _():
            nconv_ref[0] = xb[rows - ncv:, :]
            nh_ref[0] = hc
    else:
        for s in range(seqs):
            nconv_ref[s] = xb[(s + 1) * SUBLANES - ncv:(s + 1) * SUBLANES, :]
            nh_ref[s] = pieces[s][SUBLANES - 1:, :]


def _lru(pb, conv, h0, P, l, nseq, tlen, tile, seqs):
    bw = h0.shape[-1]
    ncv = conv.shape[2]
    nt = tlen // tile
    assert seqs == 1 or (nt == 1 and tile == SUBLANES)
    rows = seqs * tile

    def vec(a):
        return pl.BlockSpec((None,) + a.shape[1:], lambda b, c: (l,) + (0,) * (a.ndim - 1))

    names = ['lru_conv_w', 'lru_conv_b', 'lru_wa_bd', 'lru_ba', 'lru_wx_bd', 'lru_bx', 'lru_lam']
    params = [P[n] for n in names]
    return pl.pallas_call(
        functools.partial(_lru_kernel, bw, seqs),
        grid=(nseq // seqs, nt),
        in_specs=[
            pl.BlockSpec((rows, 2 * bw), lambda b, c: (b * nt + c, 0)),
            pl.BlockSpec((None, seqs, ncv, bw), lambda b, c: (l, b, 0, 0)),
            pl.BlockSpec((None, seqs, 1, bw), lambda b, c: (l, b, 0, 0)),
        ] + [vec(a) for a in params],
        out_specs=[
            pl.BlockSpec((rows, bw), lambda b, c: (b * nt + c, 0)),
            pl.BlockSpec((seqs, ncv, bw), lambda b, c: (b, 0, 0)),
            pl.BlockSpec((seqs, 1, bw), lambda b, c: (b, 0, 0)),
        ],
        out_shape=[
            jax.ShapeDtypeStruct((nseq * tlen, bw), F32),
            jax.ShapeDtypeStruct((nseq, ncv, bw), F32),
            jax.ShapeDtypeStruct((nseq, 1, bw), F32),
        ],
        scratch_shapes=[pltpu.VMEM((SUBLANES if seqs == 1 else rows, bw), F32),
                        pltpu.VMEM((1, bw), F32)],
        compiler_params=_cparams("parallel", "arbitrary"),
        name="rglru",
    )(pb, conv, h0, *params)


def _hgrn_kernel(heads, kd, vd, sub, nb, clen, p_ref, s0_ref, lb_ref, ng_ref, *rest):
    o_ref, ns_ref, st_ref = rest[-3:]
    c = pl.program_id(1)
    kw = heads * kd
    vw = heads * vd
    pairs = [(j, h) for j in range(nb) for h in range(heads)]

    @pl.when(c == 0)
    def _():
        for j, h in pairs:
            st_ref[j, h] = s0_ref[j, h].T

    p = p_ref[0] if nb == 1 else jnp.concatenate([p_ref[j] for j in range(nb)], axis=0)
    lb = lb_ref[...]
    q = _silu(p[:, :kw])
    f = lb + (1.0 - lb) * _sigmoid(p[:, kw:2 * kw])
    logf = jnp.log(f)
    kf = 1.0 - f
    v = p[:, 2 * kw:2 * kw + vw]
    gz = p[:, 2 * kw + vw:]

    g = _cumsum_rows(logf, clen)
    qg = q * jnp.exp(g)
    glast = [g[(j + 1) * clen - 1:(j + 1) * clen, :] for j in range(nb)]
    khat = [kf[j * clen:(j + 1) * clen, :] * jnp.exp(glast[j] - g[j * clen:(j + 1) * clen, :])
            for j in range(nb)]
    eglast = [jnp.exp(x) for x in glast]

    rows = nb * clen
    nblk = clen // sub
    rowi = lax.broadcasted_iota(jnp.int32, (rows, 1), 0)

    def usl(x, j, h, width):
        return x[j * clen:(j + 1) * clen, h * width:(h + 1) * width]

    st = [st_ref[j, h] for j, h in pairs]
    vs = [usl(v, j, h, vd) for j, h in pairs]
    o = [_mm_nt(usl(qg, j, h, kd), s) for (j, h), s in zip(pairs, st)]
    st_new = [s * eglast[j][:, h * kd:(h + 1) * kd] + _mm_tn(vv, khat[j][:, h * kd:(h + 1) * kd])
              for (j, h), s, vv in zip(pairs, st, vs)]
    for (j, h), s in zip(pairs, st_new):
        st_ref[j, h] = s

    ri = lax.broadcasted_iota(jnp.int32, (clen, clen), 0)
    ci = lax.broadcasted_iota(jnp.int32, (clen, clen), 1)
    att = None
    hs = sub
    while hs < clen:
        later = (rowi & hs) != 0
        gref = jnp.concatenate(
            [jnp.broadcast_to(g[r0 + hs - 1:r0 + hs, :], (2 * hs, kw)) for r0 in range(0, rows, 2 * hs)],
            axis=0)
        q_rel = q * jnp.exp(jnp.where(later, g - gref, -jnp.inf))
        k_rel = kf * jnp.exp(jnp.where(later, -jnp.inf, gref - g))
        sh = (2 * hs).bit_length() - 1
        same = lax.shift_right_logical(ri, sh) == lax.shift_right_logical(ci, sh)
        part = [jnp.where(same, _mm_nt(usl(q_rel, j, h, kd), usl(k_rel, j, h, kd)), 0.0)
                for j, h in pairs]
        att = part if att is None else [a + b for a, b in zip(att, part)]
        hs *= 2
    if att is not None:
        o = [oo + _mm(a, vv) for oo, a, vv in zip(o, att, vs)]

    def blocks3(x, j, h):
        return usl(x, j, h, kd).reshape(nblk, sub, kd)

    f3 = [blocks3(f, j, h) for j, h in pairs]
    q3 = [blocks3(q, j, h) for j, h in pairs]
    k3 = [blocks3(kf, j, h) for j, h in pairs]
    srow = lax.broadcasted_iota(jnp.int32, (nblk, sub, 1), 1)
    tcol = (lax.broadcasted_iota(jnp.int32, (nblk, sub, LANES), 2)
            - sub * lax.broadcasted_iota(jnp.int32, (nblk, sub, LANES), 0))
    att_t = [jnp.zeros((nblk, sub, LANES), F32) for _ in pairs]
    dec = [jnp.zeros((nblk, sub, kd), F32) for _ in pairs]
    for t in range(sub):
        dec = [jnp.where(srow == t, kk, d * ff[:, t:t + 1, :]) for d, ff, kk in zip(dec, f3, k3)]
        cols = [jnp.sum(qq[:, t:t + 1, :] * d, axis=-1, keepdims=True) for d, qq in zip(dec, q3)]
        att_t = [jnp.where(tcol == t, col, a) for col, a in zip(cols, att_t)]
    o = [oo + _mm_tn(a.reshape(clen, LANES)[:, :clen], vv) for oo, a, vv in zip(o, att_t, vs)]

    for (j, h), oo in zip(pairs, o):
        hv = slice(h * vd, (h + 1) * vd)
        oo = oo * lax.rsqrt(jnp.mean(oo * oo, axis=-1, keepdims=True) + RMS_EPS) * ng_ref[:, hv]
        o_ref[j, :, hv] = oo * _silu(gz[j * clen:(j + 1) * clen, hv])

    @pl.when(c == pl.num_programs(1) - 1)
    def _():
        for j, h in pairs:
            ns_ref[j, h] = st_ref[j, h].T


def _hgrn(pc, s0, stacked, lb, ng, l, nseq, tlen, chunk, nb):
    depth, _, heads, kd, vd = s0.shape
    width = pc.shape[1]
    nch = tlen // chunk
    sub = min(HGRN_SUB, chunk)
    args = [pc.reshape(nseq, tlen, width), s0, lb, ng]
    in_specs = [
        pl.BlockSpec((nb, chunk, width), lambda b, c: (b, c, 0)),
        pl.BlockSpec((None, nb, heads, kd, vd), lambda b, c: (l, b, 0, 0, 0)),
        pl.BlockSpec((None, 1, heads * kd), lambda b, c: (l, 0, 0)),
        pl.BlockSpec((None, 1, heads * vd), lambda b, c: (l, 0, 0)),
    ]
    aliases = {}
    if stacked is not None:
        aliases = {len(args): 1}
        args.append(stacked)
        in_specs.append(pl.BlockSpec(memory_space=pl.ANY))
    o, stacked = pl.pallas_call(
        functools.partial(_hgrn_kernel, heads, kd, vd, sub, nb, chunk),
        grid=(nseq // nb, nch),
        in_specs=in_specs,
        out_specs=[
            pl.BlockSpec((nb, chunk, heads * vd), lambda b, c: (b, c, 0)),
            pl.BlockSpec((None, nb, heads, kd, vd), lambda b, c: (l, b, 0, 0, 0)),
        ],
        out_shape=[
            jax.ShapeDtypeStruct((nseq, tlen, heads * vd), F32),
            jax.ShapeDtypeStruct((depth, nseq, heads, kd, vd), F32),
        ],
        scratch_shapes=[pltpu.VMEM((nb, heads, vd, kd), F32)],
        input_output_aliases=aliases,
        compiler_params=_cparams("parallel", "arbitrary"),
        name="hgrn2",
    )(*args)
    return o.reshape(nseq * tlen, heads * vd), stacked


def _run_trunk(x, s_rwkv, s_shift, s_lru, s_conv, s_hgrn, P):
    nseq, tlen, d = x.shape
    depth = s_rwkv.shape[0]
    x = x.reshape(nseq * tlen, d)
    lru4 = s_lru[:, :, None, :]
    rw_chunk = min(RWKV_CHUNK, tlen)
    rw_nb = _tile(nseq, max(RWKV_SEQS, RWKV_CHUNK // rw_chunk))
    hg_chunk = min(HGRN_CHUNK, tlen)
    hg_nb = _tile(nseq, max(HGRN_SEQS, HGRN_CHUNK // hg_chunk))
    lru_tile = min(LRU_TILE, tlen)
    lru_seqs = _tile(nseq, LRU_SEQS) if lru_tile == tlen == SUBLANES else 1
    outs = ([], [], [])
    n_rwkv = n_hgrn = None
    for l in range(depth):
        x = _ffn(x, P['ffn1_pre_g'], P['ffn1_wg'], P['ffn1_wu'], P['ffn1_wd'], P['ffn1_post_g'], l)
        pa, pb, pc = _in_proj(x, P['mix_pre_g'], P['w_in'], P['w_in_sections'], l)
        oa, n_shift, n_rwkv = _rwkv_mix(pa, s_shift, s_rwkv, n_rwkv, P, l, nseq, tlen, rw_chunk,
                                        rw_nb)
        ob, n_conv, n_lru = _lru(pb, s_conv, lru4, P, l, nseq, tlen, lru_tile, lru_seqs)
        oc, n_hgrn = _hgrn(pc, s_hgrn, n_hgrn, P['hgrn_lb'], P['hgrn_norm_g'], l, nseq, tlen,
                           hg_chunk, hg_nb)
        x = _merge(x, P['mix_pre_g'], P['w_in'], sum(P['w_in_sections']), oa, ob, oc, P['proj_a'],
                   P['proj_b'], P['proj_c'], P['w_out'], P['mix_post_g'], l)
        x = _ffn(x, P['ffn2_pre_g'], P['ffn2_wg'], P['ffn2_wu'], P['ffn2_wd'], P['ffn2_post_g'], l)
        for lst, t in zip(outs, (n_shift, n_lru[:, 0], n_conv)):
            lst.append(t)
    n_shift, n_lru, n_conv = (jnp.stack(lst, axis=0) for lst in outs)
    return x.reshape(nseq, tlen, d), (n_rwkv, n_shift, n_lru, n_conv, n_hgrn)


def _block_diag(w):
    depth, g, i, j = w.shape
    eye = jnp.eye(g, dtype=w.dtype)
    return jnp.einsum('lgij,gh->lgihj', w, eye).reshape(depth, g * i, g * j)


def _prepare_params(raw, a_proj, b_width, c_kwidth, c_width):
    P = {}
    for n in ('ffn1_wg', 'ffn1_wu', 'ffn1_wd', 'ffn2_wg', 'ffn2_wu', 'ffn2_wd', 'proj_a', 'proj_b',
              'proj_c', 'w_out', 'w_in', 'rwkv_w2', 'rwkv_a2', 'rwkv_g2'):
        P[n] = raw[n].astype(BF16)
    P['w_in_sections'] = (a_proj, 2 * b_width, 2 * c_kwidth + 2 * c_width)
    for n in ('ffn1_pre_g', 'ffn1_post_g', 'mix_pre_g', 'mix_post_g', 'ffn2_pre_g', 'ffn2_post_g',
              'rwkv_mu', 'rwkv_w0', 'rwkv_a0', 'rwkv_k_k', 'rwkv_k_a', 'rwkv_ln_g', 'rwkv_ln_b',
              'lru_conv_b', 'lru_ba', 'lru_bx', 'lru_lam', 'hgrn_norm_g'):
        P[n] = raw[n][:, None, :]
    depth = raw['rwkv_r_k'].shape[0]
    P['rwkv_r_k'] = raw['rwkv_r_k'].reshape(depth, 1, -1)
    P['lru_conv_w'] = raw['lru_conv_w']
    P['lru_wa_bd'] = _block_diag(raw['lru_wa']).astype(BF16)
    P['lru_wx_bd'] = _block_diag(raw['lru_wx']).astype(BF16)
    lb_cum = jnp.cumsum(jax.nn.softmax(raw['hgrn_lb_logits'].astype(F32), axis=0), axis=0)
    P['hgrn_lb'] = (lb_cum - lb_cum[0])[:, None, :]
    return P


def kernel(x_prompt, x_sample, state_rwkv, state_shift, state_lru, state_conv, state_hgrn, ffn1_pre_g, ffn1_post_g, ffn1_wg, ffn1_wu, ffn1_wd, mix_pre_g, mix_post_g, w_in, rwkv_mu, rwkv_w0, rwkv_w2, rwkv_a0, rwkv_a2, rwkv_g2, rwkv_k_k, rwkv_k_a, rwkv_r_k, rwkv_ln_g, rwkv_ln_b, lru_conv_w, lru_conv_b, lru_wa, lru_ba, lru_wx, lru_bx, lru_lam, hgrn_lb_logits, hgrn_norm_g, proj_a, proj_b, proj_c, w_out, ffn2_pre_g, ffn2_post_g, ffn2_wg, ffn2_wu, ffn2_wd):
    raw = dict(
        ffn1_pre_g=ffn1_pre_g, ffn1_post_g=ffn1_post_g, ffn1_wg=ffn1_wg, ffn1_wu=ffn1_wu,
        ffn1_wd=ffn1_wd, mix_pre_g=mix_pre_g, mix_post_g=mix_post_g, w_in=w_in, rwkv_mu=rwkv_mu,
        rwkv_w0=rwkv_w0, rwkv_w2=rwkv_w2, rwkv_a0=rwkv_a0, rwkv_a2=rwkv_a2, rwkv_g2=rwkv_g2,
        rwkv_k_k=rwkv_k_k, rwkv_k_a=rwkv_k_a, rwkv_r_k=rwkv_r_k, rwkv_ln_g=rwkv_ln_g,
        rwkv_ln_b=rwkv_ln_b, lru_conv_w=lru_conv_w, lru_conv_b=lru_conv_b, lru_wa=lru_wa,
        lru_ba=lru_ba, lru_wx=lru_wx, lru_bx=lru_bx, lru_lam=lru_lam,
        hgrn_lb_logits=hgrn_lb_logits, hgrn_norm_g=hgrn_norm_g, proj_a=proj_a, proj_b=proj_b,
        proj_c=proj_c, w_out=w_out, ffn2_pre_g=ffn2_pre_g, ffn2_post_g=ffn2_post_g,
        ffn2_wg=ffn2_wg, ffn2_wu=ffn2_wu, ffn2_wd=ffn2_wd)
    depth, _, heads, hd, _ = state_rwkv.shape
    a_proj = state_shift.shape[-1]
    b_width = state_lru.shape[-1]
    _, _, c_heads, c_kd, c_vd = state_hgrn.shape
    P = _prepare_params(raw, a_proj, b_width, c_heads * c_kd, c_heads * c_vd)
    dt = state_rwkv.dtype
    bp = x_prompt.shape[0]
    y_prompt, p_st = _run_trunk(
        x_prompt,
        jnp.zeros((depth, bp, heads, hd, hd), dt),
        jnp.zeros((depth, bp, a_proj), dt),
        jnp.zeros((depth, bp, b_width), dt),
        jnp.zeros((depth, bp) + state_conv.shape[2:], dt),
        jnp.zeros((depth, bp, c_heads, c_kd, c_vd), dt),
        P)
    y_sample, s_st = _run_trunk(x_sample, state_rwkv, state_shift, state_lru, state_conv,
                                state_hgrn, P)
    return (y_prompt, y_sample) + tuple(t.astype(dt) for t in p_st) + tuple(t.astype(dt) for t in s_st)
```

```python
import functools

import jax
import jax.numpy as jnp
from jax import lax
from jax.experimental import pallas as pl
from jax.experimental.pallas import tpu as pltpu

F32 = jnp.float32
BF16 = jnp.bfloat16
RMS_EPS = 1e-6
GN_EPS = 64e-5
LRU_C = 8.0
KK_EPS = 1e-12

VMEM_LIMIT_BYTES = 48 * 1024 * 1024
SUBLANES = 8
HGRN_SUB = 8
RWKV_CHUNK = 64
HGRN_CHUNK = 64
RWKV_SEQS = 4
HGRN_SEQS = 4
LRU_TILE = 512
LRU_SEQS = 16
IN_PROJ_ROWS = 512
FFN_ROWS = 512
FFN_CHUNK = 256


def _cparams(*sem):
    return pltpu.CompilerParams(dimension_semantics=sem, vmem_limit_bytes=VMEM_LIMIT_BYTES)


def _mm(a, b):
    return jnp.dot(a.astype(BF16), b.astype(BF16), preferred_element_type=F32)


def _mm_nt(a, b):
    return lax.dot_general(a.astype(BF16), b.astype(BF16), (((1,), (1,)), ((), ())),
                           preferred_element_type=F32)


def _mm_tn(a, b):
    return lax.dot_general(a.astype(BF16), b.astype(BF16), (((0,), (0,)), ((), ())),
                           preferred_element_type=F32)


def _split3(x):
    hi = x.astype(BF16)
    r1 = x - hi.astype(F32)
    mid = r1.astype(BF16)
    lo = (r1 - mid.astype(F32)).astype(BF16)
    return hi, mid, lo


def _cumsum_rows(x, seg=None):
    c = x.shape[0]
    row = lax.broadcasted_iota(jnp.int32, (c, c), 0)
    col = lax.broadcasted_iota(jnp.int32, (c, c), 1)
    tri = col <= row
    if seg is not None and seg < c:
        sh = seg.bit_length() - 1
        tri = tri & (lax.shift_right_logical(row, sh) == lax.shift_right_logical(col, sh))
    tri = tri.astype(BF16)
    hi, mid, lo = _split3(x)
    out = jnp.dot(tri, lo, preferred_element_type=F32)
    out = out + jnp.dot(tri, mid, preferred_element_type=F32)
    return out + jnp.dot(tri, hi, preferred_element_type=F32)


def _rms(x, g):
    return x * lax.rsqrt(jnp.mean(x * x, axis=-1, keepdims=True) + RMS_EPS) * g


def _softplus(x):
    return jnp.maximum(x, 0.0) + jnp.log1p(jnp.exp(-jnp.abs(x)))


def _sigmoid(x):
    return 0.5 * jnp.tanh(0.5 * x) + 0.5


def _silu(x):
    return x * _sigmoid(x)


LANES = 128


def _lane_block(rows, width, size):
    lane = lax.broadcasted_iota(jnp.int32, (rows, width), 1)
    return lax.shift_right_logical(lane, size.bit_length() - 1)


def _head_sums(x, hd):
    rows, width = x.shape
    per = LANES // hd
    blk = _lane_block(rows, LANES, hd)
    out = []
    for g in range(width // LANES):
        xg = x[:, g * LANES:(g + 1) * LANES]
        acc = None
        for i in reversed(range(per)):
            col = jnp.sum(jnp.where(blk == i, xg, 0.0), axis=-1, keepdims=True)
            acc = jnp.broadcast_to(col, (rows, LANES)) if acc is None else jnp.where(blk == i, col, acc)
        out.append(acc)
    return out[0] if len(out) == 1 else jnp.concatenate(out, axis=1)


def _blockdiag(x, size):
    r, width = x.shape
    blk = _lane_block(r, width, size)
    return jnp.concatenate([jnp.where(blk == i, x, 0.0) for i in range(width // size)], axis=0)


def _run_streams(*gens):
    live = list(gens)
    while live:
        for g in list(live):
            try:
                next(g)
            except StopIteration:
                live.remove(g)


def _tile(n, pref):
    t = min(n, pref)
    while n % t:
        t //= 2
    return t


def _in_proj_kernel(sections, bw, kw, vw, x_ref, g_ref, w_ref, lb_ref, pa_ref, pb_ref, pc_ref):
    xn = _rms(x_ref[...], g_ref[...]).astype(BF16)

    def proj(start, width):
        return jnp.dot(xn, w_ref[:, start:start + width], preferred_element_type=F32)

    o_b = sections[0]
    o_c = o_b + sections[1]
    pa_ref[...] = proj(0, sections[0])
    pb_ref[:, :bw] = proj(o_b, bw)
    pb_ref[:, bw:] = jax.nn.gelu(proj(o_b + bw, bw))
    lb = lb_ref[...]
    pc_ref[:, :kw] = _silu(proj(o_c, kw))
    pc_ref[:, kw:2 * kw] = lb + (1.0 - lb) * _sigmoid(proj(o_c + kw, kw))
    pc_ref[:, 2 * kw:2 * kw + vw] = proj(o_c + 2 * kw, vw)
    pc_ref[:, 2 * kw + vw:] = _silu(proj(o_c + 2 * kw + vw, vw))


def _in_proj(x, g, w, lb, sections, bw, kw, vw, l):
    m, d = x.shape
    tm = _tile(m, IN_PROJ_ROWS)

    def resident(a, cols):
        return pl.BlockSpec((None, a.shape[1], cols), lambda i: (l, 0, 0),
                            pipeline_mode=pl.Buffered(1))

    return pl.pallas_call(
        functools.partial(_in_proj_kernel, sections, bw, kw, vw),
        grid=(m // tm,),
        in_specs=[pl.BlockSpec((tm, d), lambda i: (i, 0)), resident(g, d),
                  resident(w, sum(sections)), resident(lb, kw)],
        out_specs=[pl.BlockSpec((tm, n), lambda i: (i, 0)) for n in sections],
        out_shape=[jax.ShapeDtypeStruct((m, n), F32) for n in sections],
        compiler_params=_cparams("parallel"),
        name="in_proj",
    )(x, g, w, lb)


def _ffn_kernel(cw, x_ref, gpre_ref, wg_ref, wu_ref, wd_ref, gpost_ref, o_ref, h_ref):
    ff = wg_ref.shape[-1]
    xn = _rms(x_ref[...], gpre_ref[...]).astype(BF16)

    def gate_up(i):
        cs = slice(i * cw, (i + 1) * cw)
        return (jnp.dot(xn, wg_ref[:, cs], preferred_element_type=F32),
                jnp.dot(xn, wu_ref[:, cs], preferred_element_type=F32))

    nchunk = ff // cw
    cur = gate_up(0)
    for i in range(nchunk):
        nxt = gate_up(i + 1) if i + 1 < nchunk else None
        h_ref[:, i * cw:(i + 1) * cw] = (_silu(cur[0]) * cur[1]).astype(BF16)
        cur = nxt
    y = jnp.dot(h_ref[...], wd_ref[...], preferred_element_type=F32)
    o_ref[...] = x_ref[...] + 0.5 * _rms(y, gpost_ref[...])


def _ffn(x, gpre, wg, wu, wd, gpost, l):
    m, d = x.shape
    ff = wg.shape[-1]
    tm = _tile(m, FFN_ROWS)
    cw = FFN_CHUNK if ff % FFN_CHUNK == 0 else ff

    def resident(a):
        return pl.BlockSpec((None,) + a.shape[1:], lambda i: (l,) + (0,) * (a.ndim - 1),
                            pipeline_mode=pl.Buffered(1))

    return pl.pallas_call(
        functools.partial(_ffn_kernel, cw),
        grid=(m // tm,),
        in_specs=[pl.BlockSpec((tm, d), lambda i: (i, 0)), resident(gpre), resident(wg),
                  resident(wu), resident(wd), resident(gpost)],
        out_specs=pl.BlockSpec((tm, d), lambda i: (i, 0)),
        out_shape=jax.ShapeDtypeStruct((m, d), F32),
        scratch_shapes=[pltpu.VMEM((tm, ff), BF16)],
        compiler_params=_cparams("parallel"),
        name="ffn",
    )(x, gpre, wg, wu, wd, gpost)


def _merge_kernel(goff, x_ref, gpre_ref, win_ref, oa_ref, ob_ref, oc_ref, pa_ref, pb_ref, pc_ref,
                  wo_ref, g_ref, o_ref):
    d = x_ref.shape[-1]
    x = x_ref[...]
    xn = _rms(x, gpre_ref[...]).astype(BF16)
    merged = None
    for i, (o_b, p_b) in enumerate(((oa_ref, pa_ref), (ob_ref, pb_ref), (oc_ref, pc_ref))):
        gate = jnp.dot(xn, win_ref[:, goff + i * d:goff + (i + 1) * d], preferred_element_type=F32)
        term = _sigmoid(gate) * _mm(o_b[...], p_b[...])
        merged = term if merged is None else merged + term
    y = _mm(merged, wo_ref[...])
    o_ref[...] = x + _rms(y, g_ref[...])


def _merge(x, gpre, w_in, goff, oa, ob, oc, pa, pb, pc, wo, g, l):
    m, d = x.shape
    tm = _tile(m, 512)

    def rows(w):
        return pl.BlockSpec((tm, w), lambda i: (i, 0))

    def whole(a):
        return pl.BlockSpec((None,) + a.shape[1:], lambda i: (l,) + (0,) * (a.ndim - 1),
                            pipeline_mode=pl.Buffered(1))

    return pl.pallas_call(
        functools.partial(_merge_kernel, goff),
        grid=(m // tm,),
        in_specs=[rows(d), whole(gpre), whole(w_in), rows(oa.shape[1]), rows(ob.shape[1]),
                  rows(oc.shape[1]), whole(pa), whole(pb), whole(pc), whole(wo), whole(g)],
        out_specs=rows(d),
        out_shape=jax.ShapeDtypeStruct((m, d), F32),
        compiler_params=_cparams("parallel"),
        name="merge_out",
    )(x, gpre, w_in, oa, ob, oc, pa, pb, pc, wo, g)


def _rwkv_prep(heads, hd, dw, da, nb, clen, p_ref, r0, prm, prev_ref, pf_ref, ops):
    (mu_ref, w0_ref, w2_ref, a0_ref, a2_ref, g2_ref, kk_ref, ka_ref, rk_ref) = prm
    l2_ref, bt_ref, kt_ref, h2_ref, v_ref, bv_ref, g_ref, egl_ref = ops
    aw = heads * hd
    rows = nb * clen
    seq = [slice(j * clen, (j + 1) * clen) for j in range(nb)]

    p = jnp.concatenate([p_ref[j, r0:r0 + clen, :] for j in range(nb)], axis=0)
    pf_ref[...] = pltpu.roll(p, 1, 0)
    for j in range(nb):
        pf_ref[j * clen:j * clen + 1, :] = prev_ref[j:j + 1, :]
        prev_ref[j:j + 1, :] = p_ref[j, r0 + clen - 1:r0 + clen, :]
    xs = p + mu_ref[...] * (pf_ref[...] - p)
    yield
    r = xs[:, :aw]
    k = xs[:, aw:2 * aw]
    v = xs[:, 2 * aw:3 * aw]
    w1 = xs[:, 3 * aw:3 * aw + dw]
    a1 = xs[:, 3 * aw + dw:3 * aw + dw + da]
    g1 = xs[:, 3 * aw + dw + da:]
    w = -_softplus(-(w0_ref[...] + _mm(jnp.tanh(w1), w2_ref[...]))) - 0.5
    logw = -jnp.exp(w)
    yield
    a_icl = _sigmoid(a0_ref[...] + _mm(a1, a2_ref[...]))
    gate = _mm(_sigmoid(g1), g2_ref[...])
    for j in range(nb):
        g_ref[j] = gate[seq[j]]
    yield
    kkraw = k * kk_ref[...]
    k2 = k * (1.0 + (a_icl - 1.0) * ka_ref[...])
    rkk = r * k2 * rk_ref[...]
    ginc = _cumsum_rows(logw, clen)
    yield
    eg = jnp.exp(ginc)
    egx = jnp.exp(ginc - logw)
    einv = jnp.exp(-ginc)
    yield
    scale = lax.rsqrt(jnp.maximum(_head_sums(kkraw * kkraw, hd), KK_EPS * KK_EPS))
    yield
    bonus = _head_sums(rkk, hd)
    kk = kkraw * scale
    bvec = kk * a_icl
    yield
    for j in range(nb):
        l2_ref[j, :clen, :] = -kk[seq[j]] * egx[seq[j]]
        l2_ref[j, clen:, :] = r[seq[j]] * eg[seq[j]]
    yield
    for j in range(nb):
        bt_ref[j] = bvec[seq[j]] * einv[seq[j]]
        kt_ref[j] = k2[seq[j]] * einv[seq[j]]
    yield
    for j in range(nb):
        glast = ginc[(j + 1) * clen - 1:(j + 1) * clen, :]
        ehat = jnp.exp(glast - ginc[seq[j]])
        h2_ref[j, :clen, :] = bvec[seq[j]] * ehat
        h2_ref[j, clen:, :] = k2[seq[j]] * ehat
        egl_ref[j] = jnp.exp(glast)
    yield
    for j in range(nb):
        v_ref[j] = v[seq[j]]
        bv_ref[j] = bonus[seq[j]] * v[seq[j]]
    yield


def _rwkv_chain(heads, hd, nb, clen, ops, st_ref, lng_ref, lnb_ref, o_ref, o_r0):
    l2_ref, bt_ref, kt_ref, h2_ref, v_ref, bv_ref, g_ref, egl_ref = ops
    per = LANES // hd
    cw = per * clen
    units = [(j, g) for j in range(nb) for g in range(heads // per)]

    def ld(ref):
        return [ref[j, :, g * LANES:(g + 1) * LANES] for j, g in units]

    ri = lax.broadcasted_iota(jnp.int32, (2 * clen, cw), 0)
    cpos = lax.broadcasted_iota(jnp.int32, (2 * clen, cw), 1) & (clen - 1)
    mask2 = cpos < jnp.where(ri < clen, ri, ri - (clen - 1))
    eye = ((lax.broadcasted_iota(jnp.int32, (clen, cw), 1) & (clen - 1))
           == lax.broadcasted_iota(jnp.int32, (clen, cw), 0)).astype(F32)
    blk = _lane_block(hd, LANES, hd)

    lhs2 = ld(l2_ref)
    ab2 = [jnp.where(mask2, _mm_nt(x, _blockdiag(y, hd)), 0.0) for x, y in zip(lhs2, ld(bt_ref))]
    yield
    ak2 = [jnp.where(mask2, _mm_nt(x, _blockdiag(y, hd)), 0.0) for x, y in zip(lhs2, ld(kt_ref))]
    yield
    s0 = [st_ref[j, g] for j, g in units]
    vs = ld(v_ref)
    xy = [_mm_nt(x, _blockdiag(s, hd)) + _mm(a, _blockdiag(vv, hd))
          for x, s, a, vv in zip(lhs2, s0, ak2, vs)]
    yield
    pw = [m[:clen] for m in ab2]
    tinv = [eye + m for m in pw]
    span = 2
    while span < clen:
        pwd = [_blockdiag(m, clen) for m in pw]
        pw = [_mm(m, md) for m, md in zip(pw, pwd)]
        yield
        pwd = [_blockdiag(m, clen) for m in pw]
        tinv = [t + _mm(t, md) for t, md in zip(tinv, pwd)]
        yield
        span *= 2
    u = [_mm(t, _blockdiag(m[:clen], hd)) for t, m in zip(tinv, xy)]
    yield
    y = [m[clen:] + _mm(a[clen:], _blockdiag(uu, hd)) for m, a, uu in zip(xy, ab2, u)]
    yield
    full = [_mm_tn(jnp.concatenate([uu, vv], axis=0), hh) for uu, vv, hh in zip(u, vs, ld(h2_ref))]
    for (j, g), s, f in zip(units, s0, full):
        own = f[(per - 1) * hd:, :]
        for i in reversed(range(per - 1)):
            own = jnp.where(blk == i, f[i * hd:(i + 1) * hd, :], own)
        st_ref[j, g] = s * egl_ref[j, :, g * LANES:(g + 1) * LANES] + own
    yield
    bvs = ld(bv_ref)
    gates = ld(g_ref)
    for i, (j, g) in enumerate(units):
        sl = slice(g * LANES, (g + 1) * LANES)
        yc = y[i] - _head_sums(y[i], hd) * (1.0 / hd)
        var = _head_sums(yc * yc, hd) * (1.0 / hd)
        yy = yc * lax.rsqrt(var + GN_EPS) * lng_ref[:, sl] + lnb_ref[:, sl]
        o_ref[j, o_r0:o_r0 + clen, sl] = (yy + bvs[i]) * gates[i]
        if i % (heads // per) == heads // per - 1:
            yield


def _rwkv_step_kernel(heads, hd, dw, da, nb, clen, nsub, has_stack, *refs):
    refs = list(refs)
    p_ref = refs.pop(0)
    pn_ref = refs.pop(0) if nsub == 2 else None
    shift_ref, s0_ref = refs[:2]
    prm = refs[2:11]
    lng_ref, lnb_ref = refs[11:13]
    rest = refs[13 + (1 if has_stack else 0):]
    o_ref, nshift_ref, ns_ref, st_ref, prev_ref, pf_ref = rest[:6]
    ops_a = rest[6:14]
    ops_b = rest[14:22]
    c = pl.program_id(1)
    per = LANES // hd
    groups = [(j, g) for j in range(nb) for g in range(heads // per)]

    @pl.when(c == 0)
    def _():
        prev_ref[...] = shift_ref[...]
        for j, g in groups:
            st_ref[j, g] = jnp.concatenate([s0_ref[j, g * per + i] for i in range(per)], axis=1)

    def prep(src, r0, ops):
        return _rwkv_prep(heads, hd, dw, da, nb, clen, src, r0, prm, prev_ref, pf_ref, ops)

    def chain(ops, o_r0):
        return _rwkv_chain(heads, hd, nb, clen, ops, st_ref, lng_ref, lnb_ref, o_ref, o_r0)

    if nsub == 1:
        _run_streams(prep(p_ref, 0, ops_a))
        _run_streams(chain(ops_a, 0))
    else:
        @pl.when(c == 0)
        def _():
            _run_streams(prep(p_ref, 0, ops_a))

        _run_streams(chain(ops_a, 0), prep(p_ref, clen, ops_b))
        _run_streams(chain(ops_b, clen), prep(pn_ref, 0, ops_a))

    @pl.when(c == pl.num_programs(1) - 1)
    def _():
        nshift_ref[...] = prev_ref[...]
        for j, g in groups:
            for i in range(per):
                ns_ref[j, g * per + i] = st_ref[j, g, :, i * hd:(i + 1) * hd]


def _rwkv_mix(pa, shift, s0, stacked, P, l, nseq, tlen, chunk, nb):
    ap = pa.shape[1]
    depth, _, heads, hd, _ = s0.shape
    aw = heads * hd
    dw = P['rwkv_w2'].shape[1]
    da = P['rwkv_a2'].shape[1]
    nch = tlen // chunk
    nsub = 2 if nch % 2 == 0 else 1
    assert nsub == 2 or nch == 1
    nstep = nch // nsub
    ngrp = nseq // nb
    shift = shift.reshape(depth, ngrp, nb, ap)
    pa = pa.reshape(nseq, tlen, ap)

    def vec(a):
        return pl.BlockSpec((None,) + a.shape[1:], lambda b, c: (l,) + (0,) * (a.ndim - 1))

    names = ['rwkv_mu', 'rwkv_w0', 'rwkv_w2', 'rwkv_a0', 'rwkv_a2', 'rwkv_g2', 'rwkv_k_k',
             'rwkv_k_a', 'rwkv_r_k', 'rwkv_ln_g', 'rwkv_ln_b']
    params = [P[n] for n in names]
    args = [pa]
    in_specs = [pl.BlockSpec((nb, nsub * chunk, ap), lambda b, c: (b, c, 0))]
    if nsub == 2:
        args.append(pa)
        in_specs.append(pl.BlockSpec(
            (nb, chunk, ap), lambda b, c: (b, jnp.minimum(2 * c + 2, nch - 1), 0)))
    args += [shift, s0] + params
    in_specs += [
        pl.BlockSpec((None, None, nb, ap), lambda b, c: (l, b, 0, 0)),
        pl.BlockSpec((None, nb, heads, hd, hd), lambda b, c: (l, b, 0, 0, 0)),
    ] + [vec(a) for a in params]
    aliases = {}
    if stacked is not None:
        aliases = {len(args): 2}
        args.append(stacked)
        in_specs.append(pl.BlockSpec(memory_space=pl.ANY))

    def operand_set():
        f = lambda r: pltpu.VMEM((nb, r, aw), F32)
        return [f(2 * chunk), f(chunk), f(chunk), f(2 * chunk), f(chunk), f(chunk), f(chunk), f(1)]

    o, n_shift, stacked = pl.pallas_call(
        functools.partial(_rwkv_step_kernel, heads, hd, dw, da, nb, chunk, nsub,
                          stacked is not None),
        grid=(ngrp, nstep),
        in_specs=in_specs,
        out_specs=[
            pl.BlockSpec((nb, nsub * chunk, aw), lambda b, c: (b, c, 0)),
            pl.BlockSpec((None, nb, ap), lambda b, c: (b, 0, 0)),
            pl.BlockSpec((None, nb, heads, hd, hd), lambda b, c: (l, b, 0, 0, 0)),
        ],
        out_shape=[
            jax.ShapeDtypeStruct((nseq, tlen, aw), F32),
            jax.ShapeDtypeStruct((ngrp, nb, ap), F32),
            jax.ShapeDtypeStruct((depth, nseq, heads, hd, hd), F32),
        ],
        scratch_shapes=[pltpu.VMEM((nb, aw // LANES, hd, LANES), F32), pltpu.VMEM((nb, ap), F32),
                        pltpu.VMEM((nb * chunk, ap), F32)] + operand_set() + operand_set(),
        input_output_aliases=aliases,
        compiler_params=_cparams("parallel", "arbitrary"),
        name="rwkv7",
    )(*args)
    return o.reshape(nseq * tlen, aw), n_shift.reshape(nseq, ap), stacked


def _lru_kernel(bw, seqs, p_ref, cs_ref, h0_ref, cw_ref, cb_ref, wa_ref, ba_ref, wx_ref, bx_ref,
                lam_ref, o_ref, nconv_ref, nh_ref, tail_ref, h_ref):
    c = pl.program_id(1)
    rows = p_ref.shape[0]
    ncv = cs_ref.shape[1]
    nseg = rows // SUBLANES

    if seqs == 1:
        @pl.when(c == 0)
        def _():
            tail_ref[...] = jnp.zeros_like(tail_ref)
            tail_ref[SUBLANES - ncv:, :] = cs_ref[0]
            h_ref[...] = h0_ref[0]
    else:
        tail_ref[...] = jnp.zeros_like(tail_ref)
        for j in range(seqs):
            tail_ref[(j + 1) * SUBLANES - ncv:(j + 1) * SUBLANES, :] = cs_ref[j]

    xb = p_ref[:, :bw]
    gb = p_ref[:, bw:]
    rowi = lax.broadcasted_iota(jnp.int32, (rows, 1), 0)
    pos = rowi & (SUBLANES - 1)

    def seg_roll(x, d):
        return pltpu.roll(x.reshape(nseg, SUBLANES, bw), d, 1).reshape(rows, bw)

    if seqs == 1 and rows > SUBLANES:
        before = jnp.concatenate([tail_ref[...], xb[:rows - SUBLANES, :]], axis=0)
    else:
        before = tail_ref[...]
    xc = cb_ref[...] + xb * cw_ref[ncv:ncv + 1, :]
    for d in range(1, ncv + 1):
        sh = jnp.where(pos >= d, seg_roll(xb, d), seg_roll(before, d))
        xc = xc + sh * cw_ref[ncv - d:ncv - d + 1, :]
    if seqs == 1:
        tail_ref[...] = xb[rows - SUBLANES:, :]

    rg = _sigmoid(_mm(xc, wa_ref[...]) + ba_ref[...])
    ig = _sigmoid(_mm(xc, wx_ref[...]) + bx_ref[...])
    log_a = -LRU_C * rg * _softplus(-lam_ref[...])
    a = jnp.exp(log_a)
    b = jnp.sqrt(-jnp.tanh(log_a) * (1.0 + a * a)) * ig * xc

    d = 1
    while d < SUBLANES:
        keep = pos >= d
        a_sh = jnp.where(keep, seg_roll(a, d), 1.0)
        b_sh = jnp.where(keep, seg_roll(b, d), 0.0)
        b = a * b_sh + b
        a = a_sh * a
        d *= 2
    pieces = []
    hc = h_ref[...] if seqs == 1 else None
    for s in range(nseg):
        rs = slice(s * SUBLANES, (s + 1) * SUBLANES)
        h_s = a[rs] * (hc if seqs == 1 else h0_ref[s]) + b[rs]
        hc = h_s[SUBLANES - 1:, :]
        pieces.append(h_s)
    h = pieces[0] if nseg == 1 else jnp.concatenate(pieces, axis=0)
    o_ref[...] = h * gb

    if seqs == 1:
        h_ref[...] = hc

        @pl.when(c == pl.num_programs(1) - 1)
        def _():
            nconv_ref[0] = xb[rows - ncv:, :]
            nh_ref[0] = hc
    else:
        for s in range(seqs):
            nconv_ref[s] = xb[(s + 1) * SUBLANES - ncv:(s + 1) * SUBLANES, :]
            nh_ref[s] = pieces[s][SUBLANES - 1:, :]


def _lru(pb, conv, h0, P, l, nseq, tlen, tile, seqs):
    bw = h0.shape[-1]
    ncv = conv.shape[2]
    nt = tlen // tile
    assert seqs == 1 or (nt == 1 and tile == SUBLANES)
    rows = seqs * tile

    def vec(a):
        return pl.BlockSpec((None,) + a.shape[1:], lambda b, c: (l,) + (0,) * (a.ndim - 1))

    names = ['lru_conv_w', 'lru_conv_b', 'lru_wa_bd', 'lru_ba', 'lru_wx_bd', 'lru_bx', 'lru_lam']
    params = [P[n] for n in names]
    return pl.pallas_call(
        functools.partial(_lru_kernel, bw, seqs),
        grid=(nseq // seqs, nt),
        in_specs=[
            pl.BlockSpec((rows, 2 * bw), lambda b, c: (b * nt + c, 0)),
            pl.BlockSpec((None, seqs, ncv, bw), lambda b, c: (l, b, 0, 0)),
            pl.BlockSpec((None, seqs, 1, bw), lambda b, c: (l, b, 0, 0)),
        ] + [vec(a) for a in params],
        out_specs=[
            pl.BlockSpec((rows, bw), lambda b, c: (b * nt + c, 0)),
            pl.BlockSpec((seqs, ncv, bw), lambda b, c: (b, 0, 0)),
            pl.BlockSpec((seqs, 1, bw), lambda b, c: (b, 0, 0)),
        ],
        out_shape=[
            jax.ShapeDtypeStruct((nseq * tlen, bw), F32),
            jax.ShapeDtypeStruct((nseq, ncv, bw), F32),
            jax.ShapeDtypeStruct((nseq, 1, bw), F32),
        ],
        scratch_shapes=[pltpu.VMEM((SUBLANES if seqs == 1 else rows, bw), F32),
                        pltpu.VMEM((1, bw), F32)],
        compiler_params=_cparams("parallel", "arbitrary"),
        name="rglru",
    )(pb, conv, h0, *params)


def _hgrn_kernel(heads, kd, vd, sub, nb, clen, p_ref, s0_ref, ng_ref, *rest):
    o_ref, ns_ref, st_ref = rest[-3:]
    c = pl.program_id(1)
    kw = heads * kd
    vw = heads * vd
    pairs = [(j, h) for j in range(nb) for h in range(heads)]

    @pl.when(c == 0)
    def _():
        for j, h in pairs:
            st_ref[j, h] = s0_ref[j, h].T

    p = p_ref[0] if nb == 1 else jnp.concatenate([p_ref[j] for j in range(nb)], axis=0)
    q = p[:, :kw]
    f = p[:, kw:2 * kw]
    logf = jnp.log(f)
    kf = 1.0 - f
    v = p[:, 2 * kw:2 * kw + vw]
    gzs = p[:, 2 * kw + vw:]

    g = _cumsum_rows(logf, clen)
    qg = q * jnp.exp(g)
    glast = [g[(j + 1) * clen - 1:(j + 1) * clen, :] for j in range(nb)]
    khat = [kf[j * clen:(j + 1) * clen, :] * jnp.exp(glast[j] - g[j * clen:(j + 1) * clen, :])
            for j in range(nb)]
    eglast = [jnp.exp(x) for x in glast]

    rows = nb * clen
    nblk = clen // sub
    rowi = lax.broadcasted_iota(jnp.int32, (rows, 1), 0)

    def usl(x, j, h, width):
        return x[j * clen:(j + 1) * clen, h * width:(h + 1) * width]

    st = [st_ref[j, h] for j, h in pairs]
    vs = [usl(v, j, h, vd) for j, h in pairs]
    o = [_mm_nt(usl(qg, j, h, kd), s) for (j, h), s in zip(pairs, st)]
    st_new = [s * eglast[j][:, h * kd:(h + 1) * kd] + _mm_tn(vv, khat[j][:, h * kd:(h + 1) * kd])
              for (j, h), s, vv in zip(pairs, st, vs)]
    for (j, h), s in zip(pairs, st_new):
        st_ref[j, h] = s

    ri = lax.broadcasted_iota(jnp.int32, (clen, clen), 0)
    ci = lax.broadcasted_iota(jnp.int32, (clen, clen), 1)
    att = None
    hs = sub
    while hs < clen:
        later = (rowi & hs) != 0
        gref = jnp.concatenate(
            [jnp.broadcast_to(g[r0 + hs - 1:r0 + hs, :], (2 * hs, kw)) for r0 in range(0, rows, 2 * hs)],
            axis=0)
        q_rel = q * jnp.exp(jnp.where(later, g - gref, -jnp.inf))
        k_rel = kf * jnp.exp(jnp.where(later, -jnp.inf, gref - g))
        sh = (2 * hs).bit_length() - 1
        same = lax.shift_right_logical(ri, sh) == lax.shift_right_logical(ci, sh)
        part = [jnp.where(same, _mm_nt(usl(q_rel, j, h, kd), usl(k_rel, j, h, kd)), 0.0)
                for j, h in pairs]
        att = part if att is None else [a + b for a, b in zip(att, part)]
        hs *= 2
    if att is not None:
        o = [oo + _mm(a, vv) for oo, a, vv in zip(o, att, vs)]

    def blocks3(x, j, h):
        return usl(x, j, h, kd).reshape(nblk, sub, kd)

    f3 = [blocks3(f, j, h) for j, h in pairs]
    q3 = [blocks3(q, j, h) for j, h in pairs]
    k3 = [blocks3(kf, j, h) for j, h in pairs]
    srow = lax.broadcasted_iota(jnp.int32, (nblk, sub, 1), 1)
    tcol = (lax.broadcasted_iota(jnp.int32, (nblk, sub, LANES), 2)
            - sub * lax.broadcasted_iota(jnp.int32, (nblk, sub, LANES), 0))
    att_t = [jnp.zeros((nblk, sub, LANES), F32) for _ in pairs]
    dec = [jnp.zeros((nblk, sub, kd), F32) for _ in pairs]
    for t in range(sub):
        dec = [jnp.where(srow == t, kk, d * ff[:, t:t + 1, :]) for d, ff, kk in zip(dec, f3, k3)]
        cols = [jnp.sum(qq[:, t:t + 1, :] * d, axis=-1, keepdims=True) for d, qq in zip(dec, q3)]
        att_t = [jnp.where(tcol == t, col, a) for col, a in zip(cols, att_t)]
    o = [oo + _mm_tn(a.reshape(clen, LANES)[:, :clen], vv) for oo, a, vv in zip(o, att_t, vs)]

    for (j, h), oo in zip(pairs, o):
        hv = slice(h * vd, (h + 1) * vd)
        oo = oo * lax.rsqrt(jnp.mean(oo * oo, axis=-1, keepdims=True) + RMS_EPS) * ng_ref[:, hv]
        o_ref[j, :, hv] = oo * gzs[j * clen:(j + 1) * clen, hv]

    @pl.when(c == pl.num_programs(1) - 1)
    def _():
        for j, h in pairs:
            ns_ref[j, h] = st_ref[j, h].T


def _hgrn(pc, s0, stacked, ng, l, nseq, tlen, chunk, nb):
    depth, _, heads, kd, vd = s0.shape
    width = pc.shape[1]
    nch = tlen // chunk
    sub = min(HGRN_SUB, chunk)
    args = [pc.reshape(nseq, tlen, width), s0, ng]
    in_specs = [
        pl.BlockSpec((nb, chunk, width), lambda b, c: (b, c, 0)),
        pl.BlockSpec((None, nb, heads, kd, vd), lambda b, c: (l, b, 0, 0, 0)),
        pl.BlockSpec((None, 1, heads * vd), lambda b, c: (l, 0, 0)),
    ]
    aliases = {}
    if stacked is not None:
        aliases = {len(args): 1}
        args.append(stacked)
        in_specs.append(pl.BlockSpec(memory_space=pl.ANY))
    o, stacked = pl.pallas_call(
        functools.partial(_hgrn_kernel, heads, kd, vd, sub, nb, chunk),
        grid=(nseq // nb, nch),
        in_specs=in_specs,
        out_specs=[
            pl.BlockSpec((nb, chunk, heads * vd), lambda b, c: (b, c, 0)),
            pl.BlockSpec((None, nb, heads, kd, vd), lambda b, c: (l, b, 0, 0, 0)),
        ],
        out_shape=[
            jax.ShapeDtypeStruct((nseq, tlen, heads * vd), F32),
            jax.ShapeDtypeStruct((depth, nseq, heads, kd, vd), F32),
        ],
        scratch_shapes=[pltpu.VMEM((nb, heads, vd, kd), F32)],
        input_output_aliases=aliases,
        compiler_params=_cparams("parallel", "arbitrary"),
        name="hgrn2",
    )(*args)
    return o.reshape(nseq * tlen, heads * vd), stacked


def _run_trunk(x, s_rwkv, s_shift, s_lru, s_conv, s_hgrn, P):
    nseq, tlen, d = x.shape
    depth = s_rwkv.shape[0]
    x = x.reshape(nseq * tlen, d)
    lru4 = s_lru[:, :, None, :]
    rw_chunk = min(RWKV_CHUNK, tlen)
    rw_nb = _tile(nseq, max(RWKV_SEQS, RWKV_CHUNK // rw_chunk))
    hg_chunk = min(HGRN_CHUNK, tlen)
    hg_nb = _tile(nseq, max(HGRN_SEQS, HGRN_CHUNK // hg_chunk))
    lru_tile = min(LRU_TILE, tlen)
    lru_seqs = _tile(nseq, LRU_SEQS) if lru_tile == tlen == SUBLANES else 1
    outs = ([], [], [])
    n_rwkv = n_hgrn = None
    for l in range(depth):
        x = _ffn(x, P['ffn1_pre_g'], P['ffn1_wg'], P['ffn1_wu'], P['ffn1_wd'], P['ffn1_post_g'], l)
        pa, pb, pc = _in_proj(x, P['mix_pre_g'], P['w_in'], P['hgrn_lb'], P['w_in_sections'],
                              s_lru.shape[-1], s_hgrn.shape[2] * s_hgrn.shape[3],
                              s_hgrn.shape[2] * s_hgrn.shape[4], l)
        oa, n_shift, n_rwkv = _rwkv_mix(pa, s_shift, s_rwkv, n_rwkv, P, l, nseq, tlen, rw_chunk,
                                        rw_nb)
        ob, n_conv, n_lru = _lru(pb, s_conv, lru4, P, l, nseq, tlen, lru_tile, lru_seqs)
        oc, n_hgrn = _hgrn(pc, s_hgrn, n_hgrn, P['hgrn_norm_g'], l, nseq, tlen, hg_chunk, hg_nb)
        x = _merge(x, P['mix_pre_g'], P['w_in'], sum(P['w_in_sections']), oa, ob, oc, P['proj_a'],
                   P['proj_b'], P['proj_c'], P['w_out'], P['mix_post_g'], l)
        x = _ffn(x, P['ffn2_pre_g'], P['ffn2_wg'], P['ffn2_wu'], P['ffn2_wd'], P['ffn2_post_g'], l)
        for lst, t in zip(outs, (n_shift, n_lru[:, 0], n_conv)):
            lst.append(t)
    n_shift, n_lru, n_conv = (jnp.stack(lst, axis=0) for lst in outs)
    return x.reshape(nseq, tlen, d), (n_rwkv, n_shift, n_lru, n_conv, n_hgrn)


def _block_diag(w):
    depth, g, i, j = w.shape
    eye = jnp.eye(g, dtype=w.dtype)
    return jnp.einsum('lgij,gh->lgihj', w, eye).reshape(depth, g * i, g * j)


def _prepare_params(raw, a_proj, b_width, c_kwidth, c_width):
    P = {}
    for n in ('ffn1_wg', 'ffn1_wu', 'ffn1_wd', 'ffn2_wg', 'ffn2_wu', 'ffn2_wd', 'proj_a', 'proj_b',
              'proj_c', 'w_out', 'w_in', 'rwkv_w2', 'rwkv_a2', 'rwkv_g2'):
        P[n] = raw[n].astype(BF16)
    P['w_in_sections'] = (a_proj, 2 * b_width, 2 * c_kwidth + 2 * c_width)
    for n in ('ffn1_pre_g', 'ffn1_post_g', 'mix_pre_g', 'mix_post_g', 'ffn2_pre_g', 'ffn2_post_g',
              'rwkv_mu', 'rwkv_w0', 'rwkv_a0', 'rwkv_k_k', 'rwkv_k_a', 'rwkv_ln_g', 'rwkv_ln_b',
              'lru_conv_b', 'lru_ba', 'lru_bx', 'lru_lam', 'hgrn_norm_g'):
        P[n] = raw[n][:, None, :]
    depth = raw['rwkv_r_k'].shape[0]
    P['rwkv_r_k'] = raw['rwkv_r_k'].reshape(depth, 1, -1)
    P['lru_conv_w'] = raw['lru_conv_w']
    P['lru_wa_bd'] = _block_diag(raw['lru_wa']).astype(BF16)
    P['lru_wx_bd'] = _block_diag(raw['lru_wx']).astype(BF16)
    lb_cum = jnp.cumsum(jax.nn.softmax(raw['hgrn_lb_logits'].astype(F32), axis=0), axis=0)
    P['hgrn_lb'] = (lb_cum - lb_cum[0])[:, None, :]
    return P


def kernel(x_prompt, x_sample, state_rwkv, state_shift, state_lru, state_conv, state_hgrn, ffn1_pre_g, ffn1_post_g, ffn1_wg, ffn1_wu, ffn1_wd, mix_pre_g, mix_post_g, w_in, rwkv_mu, rwkv_w0, rwkv_w2, rwkv_a0, rwkv_a2, rwkv_g2, rwkv_k_k, rwkv_k_a, rwkv_r_k, rwkv_ln_g, rwkv_ln_b, lru_conv_w, lru_conv_b, lru_wa, lru_ba, lru_wx, lru_bx, lru_lam, hgrn_lb_logits, hgrn_norm_g, proj_a, proj_b, proj_c, w_out, ffn2_pre_g, ffn2_post_g, ffn2_wg, ffn2_wu, ffn2_wd):
    raw = dict(
        ffn1_pre_g=ffn1_pre_g, ffn1_post_g=ffn1_post_g, ffn1_wg=ffn1_wg, ffn1_wu=ffn1_wu,
        ffn1_wd=ffn1_wd, mix_pre_g=mix_pre_g, mix_post_g=mix_post_g, w_in=w_in, rwkv_mu=rwkv_mu,
        rwkv_w0=rwkv_w0, rwkv_w2=rwkv_w2, rwkv_a0=rwkv_a0, rwkv_a2=rwkv_a2, rwkv_g2=rwkv_g2,
        rwkv_k_k=rwkv_k_k, rwkv_k_a=rwkv_k_a, rwkv_r_k=rwkv_r_k, rwkv_ln_g=rwkv_ln_g,
        rwkv_ln_b=rwkv_ln_b, lru_conv_w=lru_conv_w, lru_conv_b=lru_conv_b, lru_wa=lru_wa,
        lru_ba=lru_ba, lru_wx=lru_wx, lru_bx=lru_bx, lru_lam=lru_lam,
        hgrn_lb_logits=hgrn_lb_logits, hgrn_norm_g=hgrn_norm_g, proj_a=proj_a, proj_b=proj_b,
        proj_c=proj_c, w_out=w_out, ffn2_pre_g=ffn2_pre_g, ffn2_post_g=ffn2_post_g,
        ffn2_wg=ffn2_wg, ffn2_wu=ffn2_wu, ffn2_wd=ffn2_wd)
    depth, _, heads, hd, _ = state_rwkv.shape
    a_proj = state_shift.shape[-1]
    b_width = state_lru.shape[-1]
    _, _, c_heads, c_kd, c_vd = state_hgrn.shape
    P = _prepare_params(raw, a_proj, b_width, c_heads * c_kd, c_heads * c_vd)
    dt = state_rwkv.dtype
    bp = x_prompt.shape[0]
    y_prompt, p_st = _run_trunk(
        x_prompt,
        jnp.zeros((depth, bp, heads, hd, hd), dt),
        jnp.zeros((depth, bp, a_proj), dt),
        jnp.zeros((depth, bp, b_width), dt),
        jnp.zeros((depth, bp) + state_conv.shape[2:], dt),
        jnp.zeros((depth, bp, c_heads, c_kd, c_vd), dt),
        P)
    y_sample, s_st = _run_trunk(x_sample, state_rwkv, state_shift, state_lru, state_conv,
                                state_hgrn, P)
    return (y_prompt, y_sample) + tuple(t.astype(dt) for t in p_st) + tuple(t.astype(dt) for t in s_st)
```

```python
import functools

import jax
import jax.numpy as jnp
from jax import lax
from jax.experimental import pallas as pl
from jax.experimental.pallas import tpu as pltpu

F32 = jnp.float32
BF16 = jnp.bfloat16
RMS_EPS = 1e-6
GN_EPS = 64e-5
LRU_C = 8.0
KK_EPS = 1e-12

VMEM_LIMIT_BYTES = 48 * 1024 * 1024
SUBLANES = 8
LANES = 128
HGRN_SUB = 8
RWKV_CHUNK = 64
HGRN_CHUNK = 64
RWKV_SEQS = 4
HGRN_SEQS = 4
LRU_TILE = 512
LRU_SEQS = 16
IN_PROJ_ROWS = 512
MERGE_ROWS = 512
FFN_ROWS = 512
FFN_CHUNK = 256


def _cparams(*sem):
    return pltpu.CompilerParams(dimension_semantics=sem, vmem_limit_bytes=VMEM_LIMIT_BYTES)


def _mm(a, b):
    return jnp.dot(a.astype(BF16), b.astype(BF16), preferred_element_type=F32)


def _mm_nt(a, b):
    return lax.dot_general(a.astype(BF16), b.astype(BF16), (((1,), (1,)), ((), ())),
                           preferred_element_type=F32)


def _mm_tn(a, b):
    return lax.dot_general(a.astype(BF16), b.astype(BF16), (((0,), (0,)), ((), ())),
                           preferred_element_type=F32)


def _split3(x):
    hi = x.astype(BF16)
    r1 = x - hi.astype(F32)
    mid = r1.astype(BF16)
    lo = (r1 - mid.astype(F32)).astype(BF16)
    return hi, mid, lo


def _cumsum_rows(x, seg=None):
    c = x.shape[0]
    row = lax.broadcasted_iota(jnp.int32, (c, c), 0)
    col = lax.broadcasted_iota(jnp.int32, (c, c), 1)
    tri = col <= row
    if seg is not None and seg < c:
        sh = seg.bit_length() - 1
        tri = tri & (lax.shift_right_logical(row, sh) == lax.shift_right_logical(col, sh))
    tri = tri.astype(BF16)
    hi, mid, lo = _split3(x)
    out = jnp.dot(tri, lo, preferred_element_type=F32)
    out = out + jnp.dot(tri, mid, preferred_element_type=F32)
    return out + jnp.dot(tri, hi, preferred_element_type=F32)


def _rms(x, g):
    return x * lax.rsqrt(jnp.mean(x * x, axis=-1, keepdims=True) + RMS_EPS) * g


def _softplus(x):
    return jnp.maximum(x, 0.0) + jnp.log1p(jnp.exp(-jnp.abs(x)))


def _sigmoid(x):
    return 0.5 * jnp.tanh(0.5 * x) + 0.5


def _silu(x):
    return x * _sigmoid(x)


def _lane_block(rows, width, size):
    lane = lax.broadcasted_iota(jnp.int32, (rows, width), 1)
    return lax.shift_right_logical(lane, size.bit_length() - 1)


def _head_sums(x, hd):
    rows, width = x.shape
    per = LANES // hd
    blk = _lane_block(rows, LANES, hd)
    out = []
    for g in range(width // LANES):
        xg = x[:, g * LANES:(g + 1) * LANES]
        acc = None
        for i in reversed(range(per)):
            col = jnp.sum(jnp.where(blk == i, xg, 0.0), axis=-1, keepdims=True)
            acc = jnp.broadcast_to(col, (rows, LANES)) if acc is None else jnp.where(blk == i, col, acc)
        out.append(acc)
    return out[0] if len(out) == 1 else jnp.concatenate(out, axis=1)


def _blockdiag(x, size):
    r, width = x.shape
    blk = _lane_block(r, width, size)
    return jnp.concatenate([jnp.where(blk == i, x, 0.0) for i in range(width // size)], axis=0)


def _run_streams(*gens):
    live = list(gens)
    while live:
        for g in list(live):
            try:
                next(g)
            except StopIteration:
                live.remove(g)


def _tile(n, pref):
    t = min(n, pref)
    while n % t:
        t //= 2
    return t


def _in_proj_kernel(sections, bw, kw, vw, x_ref, g_ref, w_ref, lb_ref, pa_ref, pb_ref, pc_ref):
    xn = _rms(x_ref[...], g_ref[...]).astype(BF16)

    def proj(start, width):
        return jnp.dot(xn, w_ref[:, start:start + width], preferred_element_type=F32)

    o_b = sections[0]
    o_c = o_b + sections[1]
    pa_ref[...] = proj(0, sections[0])
    pb_ref[:, :bw] = proj(o_b, bw)
    pb_ref[:, bw:] = jax.nn.gelu(proj(o_b + bw, bw))
    lb = lb_ref[...]
    pc_ref[:, :kw] = _silu(proj(o_c, kw))
    pc_ref[:, kw:2 * kw] = lb + (1.0 - lb) * _sigmoid(proj(o_c + kw, kw))
    pc_ref[:, 2 * kw:2 * kw + vw] = proj(o_c + 2 * kw, vw)
    pc_ref[:, 2 * kw + vw:] = _silu(proj(o_c + 2 * kw + vw, vw))


def _in_proj(x, g, w, lb, sections, bw, kw, vw, l):
    m, d = x.shape
    tm = _tile(m, IN_PROJ_ROWS)

    def resident(a, cols):
        return pl.BlockSpec((None, a.shape[1], cols), lambda i: (l, 0, 0),
                            pipeline_mode=pl.Buffered(1))

    return pl.pallas_call(
        functools.partial(_in_proj_kernel, sections, bw, kw, vw),
        grid=(m // tm,),
        in_specs=[pl.BlockSpec((tm, d), lambda i: (i, 0)), resident(g, d),
                  resident(w, sum(sections)), resident(lb, kw)],
        out_specs=[pl.BlockSpec((tm, n), lambda i: (i, 0)) for n in sections],
        out_shape=[jax.ShapeDtypeStruct((m, n), F32) for n in sections],
        compiler_params=_cparams("parallel"),
        name="in_proj",
    )(x, g, w, lb)


def _ffn_kernel(cw, x_ref, gpre_ref, wg_ref, wu_ref, wd_ref, gpost_ref, o_ref, h_ref):
    ff = wg_ref.shape[-1]
    xn = _rms(x_ref[...], gpre_ref[...]).astype(BF16)

    def gate_up(i):
        cs = slice(i * cw, (i + 1) * cw)
        return (jnp.dot(xn, wg_ref[:, cs], preferred_element_type=F32),
                jnp.dot(xn, wu_ref[:, cs], preferred_element_type=F32))

    nchunk = ff // cw
    cur = gate_up(0)
    for i in range(nchunk):
        nxt = gate_up(i + 1) if i + 1 < nchunk else None
        h_ref[:, i * cw:(i + 1) * cw] = (_silu(cur[0]) * cur[1]).astype(BF16)
        cur = nxt
    y = jnp.dot(h_ref[...], wd_ref[...], preferred_element_type=F32)
    o_ref[...] = x_ref[...] + 0.5 * _rms(y, gpost_ref[...])


def _ffn(x, gpre, wg, wu, wd, gpost, l):
    m, d = x.shape
    ff = wg.shape[-1]
    tm = _tile(m, FFN_ROWS)
    cw = FFN_CHUNK if ff % FFN_CHUNK == 0 else ff

    def resident(a):
        return pl.BlockSpec((None,) + a.shape[1:], lambda i: (l,) + (0,) * (a.ndim - 1),
                            pipeline_mode=pl.Buffered(1))

    return pl.pallas_call(
        functools.partial(_ffn_kernel, cw),
        grid=(m // tm,),
        in_specs=[pl.BlockSpec((tm, d), lambda i: (i, 0)), resident(gpre), resident(wg),
                  resident(wu), resident(wd), resident(gpost)],
        out_specs=pl.BlockSpec((tm, d), lambda i: (i, 0)),
        out_shape=jax.ShapeDtypeStruct((m, d), F32),
        scratch_shapes=[pltpu.VMEM((tm, ff), BF16)],
        compiler_params=_cparams("parallel"),
        name="ffn",
    )(x, gpre, wg, wu, wd, gpost)


def _merge_kernel(goff, x_ref, gpre_ref, win_ref, oa_ref, ob_ref, oc_ref, pa_ref, pb_ref, pc_ref,
                  wo_ref, g_ref, o_ref):
    d = x_ref.shape[-1]
    x = x_ref[...]
    xn = _rms(x, gpre_ref[...]).astype(BF16)
    merged = None
    for i, (o_b, p_b) in enumerate(((oa_ref, pa_ref), (ob_ref, pb_ref), (oc_ref, pc_ref))):
        gate = jnp.dot(xn, win_ref[:, goff + i * d:goff + (i + 1) * d], preferred_element_type=F32)
        term = _sigmoid(gate) * _mm(o_b[...], p_b[...])
        merged = term if merged is None else merged + term
    y = _mm(merged, wo_ref[...])
    o_ref[...] = x + _rms(y, g_ref[...])


def _merge(x, gpre, w_in, goff, oa, ob, oc, pa, pb, pc, wo, g, l):
    m, d = x.shape
    tm = _tile(m, MERGE_ROWS)

    def rows(w):
        return pl.BlockSpec((tm, w), lambda i: (i, 0))

    def whole(a):
        return pl.BlockSpec((None,) + a.shape[1:], lambda i: (l,) + (0,) * (a.ndim - 1),
                            pipeline_mode=pl.Buffered(1))

    return pl.pallas_call(
        functools.partial(_merge_kernel, goff),
        grid=(m // tm,),
        in_specs=[rows(d), whole(gpre), whole(w_in), rows(oa.shape[1]), rows(ob.shape[1]),
                  rows(oc.shape[1]), whole(pa), whole(pb), whole(pc), whole(wo), whole(g)],
        out_specs=rows(d),
        out_shape=jax.ShapeDtypeStruct((m, d), F32),
        compiler_params=_cparams("parallel"),
        name="merge_out",
    )(x, gpre, w_in, oa, ob, oc, pa, pb, pc, wo, g)


def _rwkv_prep(heads, hd, dw, da, nb, clen, p_ref, r0, prm, prev_ref, pf_ref, ops):
    (mu_ref, w0_ref, w2_ref, a0_ref, a2_ref, g2_ref, kk_ref, ka_ref, rk_ref) = prm
    l2_ref, bt_ref, kt_ref, h2_ref, v_ref, bv_ref, g_ref, egl_ref = ops
    aw = heads * hd
    seq = [slice(j * clen, (j + 1) * clen) for j in range(nb)]

    p = jnp.concatenate([p_ref[j, r0:r0 + clen, :] for j in range(nb)], axis=0)
    pf_ref[...] = pltpu.roll(p, 1, 0)
    for j in range(nb):
        pf_ref[j * clen:j * clen + 1, :] = prev_ref[j:j + 1, :]
        prev_ref[j:j + 1, :] = p_ref[j, r0 + clen - 1:r0 + clen, :]
    xs = p + mu_ref[...] * (pf_ref[...] - p)
    yield
    r = xs[:, :aw]
    k = xs[:, aw:2 * aw]
    v = xs[:, 2 * aw:3 * aw]
    w1 = xs[:, 3 * aw:3 * aw + dw]
    a1 = xs[:, 3 * aw + dw:3 * aw + dw + da]
    g1 = xs[:, 3 * aw + dw + da:]
    w = -_softplus(-(w0_ref[...] + _mm(jnp.tanh(w1), w2_ref[...]))) - 0.5
    logw = -jnp.exp(w)
    yield
    a_icl = _sigmoid(a0_ref[...] + _mm(a1, a2_ref[...]))
    gate = _mm(_sigmoid(g1), g2_ref[...])
    for j in range(nb):
        g_ref[j] = gate[seq[j]]
    yield
    kkraw = k * kk_ref[...]
    k2 = k * (1.0 + (a_icl - 1.0) * ka_ref[...])
    rkk = r * k2 * rk_ref[...]
    ginc = _cumsum_rows(logw, clen)
    yield
    eg = jnp.exp(ginc)
    egx = jnp.exp(ginc - logw)
    einv = jnp.exp(-ginc)
    yield
    scale = lax.rsqrt(jnp.maximum(_head_sums(kkraw * kkraw, hd), KK_EPS * KK_EPS))
    yield
    bonus = _head_sums(rkk, hd)
    kk = kkraw * scale
    bvec = kk * a_icl
    yield
    for j in range(nb):
        l2_ref[j, :clen, :] = -kk[seq[j]] * egx[seq[j]]
        l2_ref[j, clen:, :] = r[seq[j]] * eg[seq[j]]
    yield
    for j in range(nb):
        bt_ref[j] = bvec[seq[j]] * einv[seq[j]]
        kt_ref[j] = k2[seq[j]] * einv[seq[j]]
    yield
    for j in range(nb):
        glast = ginc[(j + 1) * clen - 1:(j + 1) * clen, :]
        ehat = jnp.exp(glast - ginc[seq[j]])
        h2_ref[j, :clen, :] = bvec[seq[j]] * ehat
        h2_ref[j, clen:, :] = k2[seq[j]] * ehat
        egl_ref[j] = jnp.exp(glast)
    yield
    for j in range(nb):
        v_ref[j] = v[seq[j]]
        bv_ref[j] = bonus[seq[j]] * v[seq[j]]
    yield


def _rwkv_chain(heads, hd, nb, clen, ops, st_ref, lng_ref, lnb_ref, o_ref, o_r0):
    l2_ref, bt_ref, kt_ref, h2_ref, v_ref, bv_ref, g_ref, egl_ref = ops
    per = LANES // hd
    cw = per * clen
    units = [(j, g) for j in range(nb) for g in range(heads // per)]

    def ld(ref):
        return [ref[j, :, g * LANES:(g + 1) * LANES] for j, g in units]

    ri = lax.broadcasted_iota(jnp.int32, (2 * clen, cw), 0)
    cpos = lax.broadcasted_iota(jnp.int32, (2 * clen, cw), 1) & (clen - 1)
    mask2 = cpos < jnp.where(ri < clen, ri, ri - (clen - 1))
    eye = ((lax.broadcasted_iota(jnp.int32, (clen, cw), 1) & (clen - 1))
           == lax.broadcasted_iota(jnp.int32, (clen, cw), 0)).astype(F32)
    blk = _lane_block(hd, LANES, hd)

    lhs2 = ld(l2_ref)
    ab2 = [jnp.where(mask2, _mm_nt(x, _blockdiag(y, hd)), 0.0) for x, y in zip(lhs2, ld(bt_ref))]
    yield
    ak2 = [jnp.where(mask2, _mm_nt(x, _blockdiag(y, hd)), 0.0) for x, y in zip(lhs2, ld(kt_ref))]
    yield
    s0 = [st_ref[j, g] for j, g in units]
    vs = ld(v_ref)
    xy = [_mm_nt(x, _blockdiag(s, hd)) + _mm(a, _blockdiag(vv, hd))
          for x, s, a, vv in zip(lhs2, s0, ak2, vs)]
    yield
    pw = [m[:clen] for m in ab2]
    tinv = [eye + m for m in pw]
    span = 2
    while span < clen:
        pwd = [_blockdiag(m, clen) for m in pw]
        pw = [_mm(m, md) for m, md in zip(pw, pwd)]
        yield
        pwd = [_blockdiag(m, clen) for m in pw]
        tinv = [t + _mm(t, md) for t, md in zip(tinv, pwd)]
        yield
        span *= 2
    u = [_mm(t, _blockdiag(m[:clen], hd)) for t, m in zip(tinv, xy)]
    yield
    y = [m[clen:] + _mm(a[clen:], _blockdiag(uu, hd)) for m, a, uu in zip(xy, ab2, u)]
    yield
    full = [_mm_tn(jnp.concatenate([uu, vv], axis=0), hh) for uu, vv, hh in zip(u, vs, ld(h2_ref))]
    for (j, g), s, f in zip(units, s0, full):
        own = f[(per - 1) * hd:, :]
        for i in reversed(range(per - 1)):
            own = jnp.where(blk == i, f[i * hd:(i + 1) * hd, :], own)
        st_ref[j, g] = s * egl_ref[j, :, g * LANES:(g + 1) * LANES] + own
    yield
    bvs = ld(bv_ref)
    gates = ld(g_ref)
    for i, (j, g) in enumerate(units):
        sl = slice(g * LANES, (g + 1) * LANES)
        yc = y[i] - _head_sums(y[i], hd) * (1.0 / hd)
        var = _head_sums(yc * yc, hd) * (1.0 / hd)
        yy = yc * lax.rsqrt(var + GN_EPS) * lng_ref[:, sl] + lnb_ref[:, sl]
        o_ref[j, o_r0:o_r0 + clen, sl] = (yy + bvs[i]) * gates[i]
        if i % (heads // per) == heads // per - 1:
            yield


def _rwkv_step_kernel(heads, hd, dw, da, nb, clen, nsub, has_stack, *refs):
    refs = list(refs)
    p_ref = refs.pop(0)
    pn_ref = refs.pop(0) if nsub == 2 else None
    shift_ref, s0_ref = refs[:2]
    prm = refs[2:11]
    lng_ref, lnb_ref = refs[11:13]
    rest = refs[13 + (1 if has_stack else 0):]
    o_ref, nshift_ref, ns_ref, st_ref, prev_ref, pf_ref = rest[:6]
    ops_a = rest[6:14]
    ops_b = rest[14:22]
    c = pl.program_id(1)
    per = LANES // hd
    groups = [(j, g) for j in range(nb) for g in range(heads // per)]

    @pl.when(c == 0)
    def _():
        prev_ref[...] = shift_ref[...]
        for j, g in groups:
            st_ref[j, g] = jnp.concatenate([s0_ref[j, g * per + i] for i in range(per)], axis=1)

    def prep(src, r0, ops):
        return _rwkv_prep(heads, hd, dw, da, nb, clen, src, r0, prm, prev_ref, pf_ref, ops)

    def chain(ops, o_r0):
        return _rwkv_chain(heads, hd, nb, clen, ops, st_ref, lng_ref, lnb_ref, o_ref, o_r0)

    if nsub == 1:
        _run_streams(prep(p_ref, 0, ops_a))
        _run_streams(chain(ops_a, 0))
    else:
        @pl.when(c == 0)
        def _():
            _run_streams(prep(p_ref, 0, ops_a))

        _run_streams(chain(ops_a, 0), prep(p_ref, clen, ops_b))
        _run_streams(chain(ops_b, clen), prep(pn_ref, 0, ops_a))

    @pl.when(c == pl.num_programs(1) - 1)
    def _():
        nshift_ref[...] = prev_ref[...]
        for j, g in groups:
            for i in range(per):
                ns_ref[j, g * per + i] = st_ref[j, g, :, i * hd:(i + 1) * hd]


def _rwkv_mix(pa, shift, s0, stacked, P, l, nseq, tlen, chunk, nb):
    ap = pa.shape[1]
    depth, _, heads, hd, _ = s0.shape
    aw = heads * hd
    dw = P['rwkv_w2'].shape[1]
    da = P['rwkv_a2'].shape[1]
    nch = tlen // chunk
    nsub = 2 if nch % 2 == 0 else 1
    assert nsub == 2 or nch == 1
    nstep = nch // nsub
    ngrp = nseq // nb
    shift = shift.reshape(depth, ngrp, nb, ap)
    pa = pa.reshape(nseq, tlen, ap)

    def vec(a):
        return pl.BlockSpec((None,) + a.shape[1:], lambda b, c: (l,) + (0,) * (a.ndim - 1))

    names = ['rwkv_mu', 'rwkv_w0', 'rwkv_w2', 'rwkv_a0', 'rwkv_a2', 'rwkv_g2', 'rwkv_k_k',
             'rwkv_k_a', 'rwkv_r_k', 'rwkv_ln_g', 'rwkv_ln_b']
    params = [P[n] for n in names]
    args = [pa]
    in_specs = [pl.BlockSpec((nb, nsub * chunk, ap), lambda b, c: (b, c, 0))]
    if nsub == 2:
        args.append(pa)
        in_specs.append(pl.BlockSpec(
            (nb, chunk, ap), lambda b, c: (b, jnp.minimum(2 * c + 2, nch - 1), 0)))
    args += [shift, s0] + params
    in_specs += [
        pl.BlockSpec((None, None, nb, ap), lambda b, c: (l, b, 0, 0)),
        pl.BlockSpec((None, nb, heads, hd, hd), lambda b, c: (l, b, 0, 0, 0)),
    ] + [vec(a) for a in params]
    aliases = {}
    if stacked is not None:
        aliases = {len(args): 2}
        args.append(stacked)
        in_specs.append(pl.BlockSpec(memory_space=pl.ANY))

    def operand_set():
        f = lambda r: pltpu.VMEM((nb, r, aw), F32)
        return [f(2 * chunk), f(chunk), f(chunk), f(2 * chunk), f(chunk), f(chunk), f(chunk), f(1)]

    o, n_shift, stacked = pl.pallas_call(
        functools.partial(_rwkv_step_kernel, heads, hd, dw, da, nb, chunk, nsub,
                          stacked is not None),
        grid=(ngrp, nstep),
        in_specs=in_specs,
        out_specs=[
            pl.BlockSpec((nb, nsub * chunk, aw), lambda b, c: (b, c, 0)),
            pl.BlockSpec((None, nb, ap), lambda b, c: (b, 0, 0)),
            pl.BlockSpec((None, nb, heads, hd, hd), lambda b, c: (l, b, 0, 0, 0)),
        ],
        out_shape=[
            jax.ShapeDtypeStruct((nseq, tlen, aw), F32),
            jax.ShapeDtypeStruct((ngrp, nb, ap), F32),
            jax.ShapeDtypeStruct((depth, nseq, heads, hd, hd), F32),
        ],
        scratch_shapes=[pltpu.VMEM((nb, aw // LANES, hd, LANES), F32), pltpu.VMEM((nb, ap), F32),
                        pltpu.VMEM((nb * chunk, ap), F32)] + operand_set() + operand_set(),
        input_output_aliases=aliases,
        compiler_params=_cparams("parallel", "arbitrary"),
        name="rwkv7",
    )(*args)
    return o.reshape(nseq * tlen, aw), n_shift.reshape(nseq, ap), stacked


def _lru_kernel(bw, seqs, p_ref, cs_ref, h0_ref, cw_ref, cb_ref, wa_ref, ba_ref, wx_ref, bx_ref,
                lam_ref, o_ref, nconv_ref, nh_ref, tail_ref, h_ref):
    c = pl.program_id(1)
    rows = p_ref.shape[0]
    ncv = cs_ref.shape[1]
    nseg = rows // SUBLANES

    if seqs == 1:
        @pl.when(c == 0)
        def _():
            tail_ref[...] = jnp.zeros_like(tail_ref)
            tail_ref[SUBLANES - ncv:, :] = cs_ref[0]
            h_ref[...] = h0_ref[0]
    else:
        tail_ref[...] = jnp.zeros_like(tail_ref)
        for j in range(seqs):
            tail_ref[(j + 1) * SUBLANES - ncv:(j + 1) * SUBLANES, :] = cs_ref[j]

    xb = p_ref[:, :bw]
    gb = p_ref[:, bw:]
    rowi = lax.broadcasted_iota(jnp.int32, (rows, 1), 0)
    pos = rowi & (SUBLANES - 1)

    def seg_roll(x, d):
        return pltpu.roll(x.reshape(nseg, SUBLANES, bw), d, 1).reshape(rows, bw)

    if seqs == 1 and rows > SUBLANES:
        before = jnp.concatenate([tail_ref[...], xb[:rows - SUBLANES, :]], axis=0)
    else:
        before = tail_ref[...]
    xc = cb_ref[...] + xb * cw_ref[ncv:ncv + 1, :]
    for d in range(1, ncv + 1):
        sh = jnp.where(pos >= d, seg_roll(xb, d), seg_roll(before, d))
        xc = xc + sh * cw_ref[ncv - d:ncv - d + 1, :]
    if seqs == 1:
        tail_ref[...] = xb[rows - SUBLANES:, :]

    rg = _sigmoid(_mm(xc, wa_ref[...]) + ba_ref[...])
    ig = _sigmoid(_mm(xc, wx_ref[...]) + bx_ref[...])
    log_a = -LRU_C * rg * _softplus(-lam_ref[...])
    a = jnp.exp(log_a)
    b = jnp.sqrt(-jnp.tanh(log_a) * (1.0 + a * a)) * ig * xc

    d = 1
    while d < SUBLANES:
        keep = pos >= d
        a_sh = jnp.where(keep, seg_roll(a, d), 1.0)
        b_sh = jnp.where(keep, seg_roll(b, d), 0.0)
        b = a * b_sh + b
        a = a_sh * a
        d *= 2
    pieces = []
    hc = h_ref[...] if seqs == 1 else None
    for s in range(nseg):
        rs = slice(s * SUBLANES, (s + 1) * SUBLANES)
        h_s = a[rs] * (hc if seqs == 1 else h0_ref[s]) + b[rs]
        hc = h_s[SUBLANES - 1:, :]
        pieces.append(h_s)
    h = pieces[0] if nseg == 1 else jnp.concatenate(pieces, axis=0)
    o_ref[...] = h * gb

    if seqs == 1:
        h_ref[...] = hc

        @pl.when(c == pl.num_programs(1) - 1)
        def _():
            nconv_ref[0] = xb[rows - ncv:, :]
            nh_ref[0] = hc
    else:
        for s in range(seqs):
            nconv_ref[s] = xb[(s + 1) * SUBLANES - ncv:(s + 1) * SUBLANES, :]
            nh_ref[s] = pieces[s][SUBLANES - 1:, :]


def _lru(pb, conv, h0, P, l, nseq, tlen, tile, seqs):
    bw = h0.shape[-1]
    ncv = conv.shape[2]
    nt = tlen // tile
    assert seqs == 1 or (nt == 1 and tile == SUBLANES)
    rows = seqs * tile

    def vec(a):
        return pl.BlockSpec((None,) + a.shape[1:], lambda b, c: (l,) + (0,) * (a.ndim - 1))

    names = ['lru_conv_w', 'lru_conv_b', 'lru_wa_bd', 'lru_ba', 'lru_wx_bd', 'lru_bx', 'lru_lam']
    params = [P[n] for n in names]
    return pl.pallas_call(
        functools.partial(_lru_kernel, bw, seqs),
        grid=(nseq // seqs, nt),
        in_specs=[
            pl.BlockSpec((rows, 2 * bw), lambda b, c: (b * nt + c, 0)),
            pl.BlockSpec((None, seqs, ncv, bw), lambda b, c: (l, b, 0, 0)),
            pl.BlockSpec((None, seqs, 1, bw), lambda b, c: (l, b, 0, 0)),
        ] + [vec(a) for a in params],
        out_specs=[
            pl.BlockSpec((rows, bw), lambda b, c: (b * nt + c, 0)),
            pl.BlockSpec((seqs, ncv, bw), lambda b, c: (b, 0, 0)),
            pl.BlockSpec((seqs, 1, bw), lambda b, c: (b, 0, 0)),
        ],
        out_shape=[
            jax.ShapeDtypeStruct((nseq * tlen, bw), F32),
            jax.ShapeDtypeStruct((nseq, ncv, bw), F32),
            jax.ShapeDtypeStruct((nseq, 1, bw), F32),
        ],
        scratch_shapes=[pltpu.VMEM((SUBLANES if seqs == 1 else rows, bw), F32),
                        pltpu.VMEM((1, bw), F32)],
        compiler_params=_cparams("parallel", "arbitrary"),
        name="rglru",
    )(pb, conv, h0, *params)


def _hgrn_kernel(heads, kd, vd, sub, nb, clen, p_ref, s0_ref, ng_ref, *rest):
    o_ref, ns_ref, st_ref = rest[-3:]
    c = pl.program_id(1)
    kw = heads * kd
    vw = heads * vd
    pairs = [(j, h) for j in range(nb) for h in range(heads)]

    @pl.when(c == 0)
    def _():
        for j, h in pairs:
            st_ref[j, h] = s0_ref[j, h].T

    p = p_ref[0] if nb == 1 else jnp.concatenate([p_ref[j] for j in range(nb)], axis=0)
    q = p[:, :kw]
    f = p[:, kw:2 * kw]
    logf = jnp.log(f)
    kf = 1.0 - f
    v = p[:, 2 * kw:2 * kw + vw]
    gzs = p[:, 2 * kw + vw:]

    g = _cumsum_rows(logf, clen)
    qg = q * jnp.exp(g)
    glast = [g[(j + 1) * clen - 1:(j + 1) * clen, :] for j in range(nb)]
    khat = [kf[j * clen:(j + 1) * clen, :] * jnp.exp(glast[j] - g[j * clen:(j + 1) * clen, :])
            for j in range(nb)]
    eglast = [jnp.exp(x) for x in glast]

    rows = nb * clen
    nblk = clen // sub
    rowi = lax.broadcasted_iota(jnp.int32, (rows, 1), 0)

    def usl(x, j, h, width):
        return x[j * clen:(j + 1) * clen, h * width:(h + 1) * width]

    st = [st_ref[j, h] for j, h in pairs]
    vs = [usl(v, j, h, vd) for j, h in pairs]
    o = [_mm_nt(usl(qg, j, h, kd), s) for (j, h), s in zip(pairs, st)]
    st_new = [s * eglast[j][:, h * kd:(h + 1) * kd] + _mm_tn(vv, khat[j][:, h * kd:(h + 1) * kd])
              for (j, h), s, vv in zip(pairs, st, vs)]
    for (j, h), s in zip(pairs, st_new):
        st_ref[j, h] = s

    ri = lax.broadcasted_iota(jnp.int32, (clen, clen), 0)
    ci = lax.broadcasted_iota(jnp.int32, (clen, clen), 1)
    att = None
    hs = sub
    while hs < clen:
        later = (rowi & hs) != 0
        gref = jnp.concatenate(
            [jnp.broadcast_to(g[r0 + hs - 1:r0 + hs, :], (2 * hs, kw)) for r0 in range(0, rows, 2 * hs)],
            axis=0)
        q_rel = q * jnp.exp(jnp.where(later, g - gref, -jnp.inf))
        k_rel = kf * jnp.exp(jnp.where(later, -jnp.inf, gref - g))
        sh = (2 * hs).bit_length() - 1
        same = lax.shift_right_logical(ri, sh) == lax.shift_right_logical(ci, sh)
        part = [jnp.where(same, _mm_nt(usl(q_rel, j, h, kd), usl(k_rel, j, h, kd)), 0.0)
                for j, h in pairs]
        att = part if att is None else [a + b for a, b in zip(att, part)]
        hs *= 2
    if att is not None:
        o = [oo + _mm(a, vv) for oo, a, vv in zip(o, att, vs)]

    def blocks3(x, j, h):
        return usl(x, j, h, kd).reshape(nblk, sub, kd)

    f3 = [blocks3(f, j, h) for j, h in pairs]
    q3 = [blocks3(q, j, h) for j, h in pairs]
    k3 = [blocks3(kf, j, h) for j, h in pairs]
    srow = lax.broadcasted_iota(jnp.int32, (nblk, sub, 1), 1)
    tcol = (lax.broadcasted_iota(jnp.int32, (nblk, sub, LANES), 2)
            - sub * lax.broadcasted_iota(jnp.int32, (nblk, sub, LANES), 0))
    att_t = [jnp.zeros((nblk, sub, LANES), F32) for _ in pairs]
    dec = [jnp.zeros((nblk, sub, kd), F32) for _ in pairs]
    for t in range(sub):
        dec = [jnp.where(srow == t, kk, d * ff[:, t:t + 1, :]) for d, ff, kk in zip(dec, f3, k3)]
        cols = [jnp.sum(qq[:, t:t + 1, :] * d, axis=-1, keepdims=True) for d, qq in zip(dec, q3)]
        att_t = [jnp.where(tcol == t, col, a) for col, a in zip(cols, att_t)]
    o = [oo + _mm_tn(a.reshape(clen, LANES)[:, :clen], vv) for oo, a, vv in zip(o, att_t, vs)]

    for (j, h), oo in zip(pairs, o):
        hv = slice(h * vd, (h + 1) * vd)
        oo = oo * lax.rsqrt(jnp.mean(oo * oo, axis=-1, keepdims=True) + RMS_EPS) * ng_ref[:, hv]
        o_ref[j, :, hv] = oo * gzs[j * clen:(j + 1) * clen, hv]

    @pl.when(c == pl.num_programs(1) - 1)
    def _():
        for j, h in pairs:
            ns_ref[j, h] = st_ref[j, h].T


def _hgrn(pc, s0, stacked, ng, l, nseq, tlen, chunk, nb):
    depth, _, heads, kd, vd = s0.shape
    width = pc.shape[1]
    nch = tlen // chunk
    sub = min(HGRN_SUB, chunk)
    args = [pc.reshape(nseq, tlen, width), s0, ng]
    in_specs = [
        pl.BlockSpec((nb, chunk, width), lambda b, c: (b, c, 0)),
        pl.BlockSpec((None, nb, heads, kd, vd), lambda b, c: (l, b, 0, 0, 0)),
        pl.BlockSpec((None, 1, heads * vd), lambda b, c: (l, 0, 0)),
    ]
    aliases = {}
    if stacked is not None:
        aliases = {len(args): 1}
        args.append(stacked)
        in_specs.append(pl.BlockSpec(memory_space=pl.ANY))
    o, stacked = pl.pallas_call(
        functools.partial(_hgrn_kernel, heads, kd, vd, sub, nb, chunk),
        grid=(nseq // nb, nch),
        in_specs=in_specs,
        out_specs=[
            pl.BlockSpec((nb, chunk, heads * vd), lambda b, c: (b, c, 0)),
            pl.BlockSpec((None, nb, heads, kd, vd), lambda b, c: (l, b, 0, 0, 0)),
        ],
        out_shape=[
            jax.ShapeDtypeStruct((nseq, tlen, heads * vd), F32),
            jax.ShapeDtypeStruct((depth, nseq, heads, kd, vd), F32),
        ],
        scratch_shapes=[pltpu.VMEM((nb, heads, vd, kd), F32)],
        input_output_aliases=aliases,
        compiler_params=_cparams("parallel", "arbitrary"),
        name="hgrn2",
    )(*args)
    return o.reshape(nseq * tlen, heads * vd), stacked


def _run_trunk(x, s_rwkv, s_shift, s_lru, s_conv, s_hgrn, P):
    nseq, tlen, d = x.shape
    depth = s_rwkv.shape[0]
    x = x.reshape(nseq * tlen, d)
    lru4 = s_lru[:, :, None, :]
    rw_chunk = min(RWKV_CHUNK, tlen)
    rw_nb = _tile(nseq, max(RWKV_SEQS, RWKV_CHUNK // rw_chunk))
    hg_chunk = min(HGRN_CHUNK, tlen)
    hg_nb = _tile(nseq, max(HGRN_SEQS, HGRN_CHUNK // hg_chunk))
    lru_tile = min(LRU_TILE, tlen)
    lru_seqs = _tile(nseq, LRU_SEQS) if lru_tile == tlen == SUBLANES else 1
    outs = ([], [], [])
    n_rwkv = n_hgrn = None
    for l in range(depth):
        x = _ffn(x, P['ffn1_pre_g'], P['ffn1_wg'], P['ffn1_wu'], P['ffn1_wd'], P['ffn1_post_g'], l)
        pa, pb, pc = _in_proj(x, P['mix_pre_g'], P['w_in'], P['hgrn_lb'], P['w_in_sections'],
                              s_lru.shape[-1], s_hgrn.shape[2] * s_hgrn.shape[3],
                              s_hgrn.shape[2] * s_hgrn.shape[4], l)
        oa, n_shift, n_rwkv = _rwkv_mix(pa, s_shift, s_rwkv, n_rwkv, P, l, nseq, tlen, rw_chunk,
                                        rw_nb)
        ob, n_conv, n_lru = _lru(pb, s_conv, lru4, P, l, nseq, tlen, lru_tile, lru_seqs)
        oc, n_hgrn = _hgrn(pc, s_hgrn, n_hgrn, P['hgrn_norm_g'], l, nseq, tlen, hg_chunk, hg_nb)
        x = _merge(x, P['mix_pre_g'], P['w_in'], sum(P['w_in_sections']), oa, ob, oc, P['proj_a'],
                   P['proj_b'], P['proj_c'], P['w_out'], P['mix_post_g'], l)
        x = _ffn(x, P['ffn2_pre_g'], P['ffn2_wg'], P['ffn2_wu'], P['ffn2_wd'], P['ffn2_post_g'], l)
        for lst, t in zip(outs, (n_shift, n_lru[:, 0], n_conv)):
            lst.append(t)
    n_shift, n_lru, n_conv = (jnp.stack(lst, axis=0) for lst in outs)
    return x.reshape(nseq, tlen, d), (n_rwkv, n_shift, n_lru, n_conv, n_hgrn)


def _block_diag(w):
    depth, g, i, j = w.shape
    eye = jnp.eye(g, dtype=w.dtype)
    return jnp.einsum('lgij,gh->lgihj', w, eye).reshape(depth, g * i, g * j)


def _prepare_params(raw, a_proj, b_width, c_kwidth, c_width):
    P = {}
    for n in ('ffn1_wg', 'ffn1_wu', 'ffn1_wd', 'ffn2_wg', 'ffn2_wu', 'ffn2_wd', 'proj_a', 'proj_b',
              'proj_c', 'w_out', 'w_in', 'rwkv_w2', 'rwkv_a2', 'rwkv_g2'):
        P[n] = raw[n].astype(BF16)
    P['w_in_sections'] = (a_proj, 2 * b_width, 2 * c_kwidth + 2 * c_width)
    for n in ('ffn1_pre_g', 'ffn1_post_g', 'mix_pre_g', 'mix_post_g', 'ffn2_pre_g', 'ffn2_post_g',
              'rwkv_mu', 'rwkv_w0', 'rwkv_a0', 'rwkv_k_k', 'rwkv_k_a', 'rwkv_ln_g', 'rwkv_ln_b',
              'lru_conv_b', 'lru_ba', 'lru_bx', 'lru_lam', 'hgrn_norm_g'):
        P[n] = raw[n][:, None, :]
    depth = raw['rwkv_r_k'].shape[0]
    P['rwkv_r_k'] = raw['rwkv_r_k'].reshape(depth, 1, -1)
    P['lru_conv_w'] = raw['lru_conv_w']
    P['lru_wa_bd'] = _block_diag(raw['lru_wa']).astype(BF16)
    P['lru_wx_bd'] = _block_diag(raw['lru_wx']).astype(BF16)
    lb_cum = jnp.cumsum(jax.nn.softmax(raw['hgrn_lb_logits'].astype(F32), axis=0), axis=0)
    P['hgrn_lb'] = (lb_cum - lb_cum[0])[:, None, :]
    return P


def kernel(x_prompt, x_sample, state_rwkv, state_shift, state_lru, state_conv, state_hgrn, ffn1_pre_g, ffn1_post_g, ffn1_wg, ffn1_wu, ffn1_wd, mix_pre_g, mix_post_g, w_in, rwkv_mu, rwkv_w0, rwkv_w2, rwkv_a0, rwkv_a2, rwkv_g2, rwkv_k_k, rwkv_k_a, rwkv_r_k, rwkv_ln_g, rwkv_ln_b, lru_conv_w, lru_conv_b, lru_wa, lru_ba, lru_wx, lru_bx, lru_lam, hgrn_lb_logits, hgrn_norm_g, proj_a, proj_b, proj_c, w_out, ffn2_pre_g, ffn2_post_g, ffn2_wg, ffn2_wu, ffn2_wd):
    raw = dict(
        ffn1_pre_g=ffn1_pre_g, ffn1_post_g=ffn1_post_g, ffn1_wg=ffn1_wg, ffn1_wu=ffn1_wu,
        ffn1_wd=ffn1_wd, mix_pre_g=mix_pre_g, mix_post_g=mix_post_g, w_in=w_in, rwkv_mu=rwkv_mu,
        rwkv_w0=rwkv_w0, rwkv_w2=rwkv_w2, rwkv_a0=rwkv_a0, rwkv_a2=rwkv_a2, rwkv_g2=rwkv_g2,
        rwkv_k_k=rwkv_k_k, rwkv_k_a=rwkv_k_a, rwkv_r_k=rwkv_r_k, rwkv_ln_g=rwkv_ln_g,
        rwkv_ln_b=rwkv_ln_b, lru_conv_w=lru_conv_w, lru_conv_b=lru_conv_b, lru_wa=lru_wa,
        lru_ba=lru_ba, lru_wx=lru_wx, lru_bx=lru_bx, lru_lam=lru_lam,
        hgrn_lb_logits=hgrn_lb_logits, hgrn_norm_g=hgrn_norm_g, proj_a=proj_a, proj_b=proj_b,
        proj_c=proj_c, w_out=w_out, ffn2_pre_g=ffn2_pre_g, ffn2_post_g=ffn2_post_g,
        ffn2_wg=ffn2_wg, ffn2_wu=ffn2_wu, ffn2_wd=ffn2_wd)
    depth, _, heads, hd, _ = state_rwkv.shape
    a_proj = state_shift.shape[-1]
    b_width = state_lru.shape[-1]
    _, _, c_heads, c_kd, c_vd = state_hgrn.shape
    P = _prepare_params(raw, a_proj, b_width, c_heads * c_kd, c_heads * c_vd)
    dt = state_rwkv.dtype
    bp = x_prompt.shape[0]
    y_prompt, p_st = _run_trunk(
        x_prompt,
        jnp.zeros((depth, bp, heads, hd, hd), dt),
        jnp.zeros((depth, bp, a_proj), dt),
        jnp.zeros((depth, bp, b_width), dt),
        jnp.zeros((depth, bp) + state_conv.shape[2:], dt),
        jnp.zeros((depth, bp, c_heads, c_kd, c_vd), dt),
        P)
    y_sample, s_st = _run_trunk(x_sample, state_rwkv, state_shift, state_lru, state_conv,
                                state_hgrn, P)
    return (y_prompt, y_sample) + tuple(t.astype(dt) for t in p_st) + tuple(t.astype(dt) for t in s_st)
```

```python
import functools

import jax
import jax.numpy as jnp
from jax import lax
from jax.experimental import pallas as pl
from jax.experimental.pallas import tpu as pltpu

F32 = jnp.float32
BF16 = jnp.bfloat16
RMS_EPS = 1e-6
GN_EPS = 64e-5
LRU_C = 8.0
KK_EPS = 1e-12

VMEM_LIMIT_BYTES = 48 * 1024 * 1024
SUBLANES = 8
LANES = 128
HGRN_SUB = 8
RWKV_CHUNK = 64
HGRN_CHUNK = 64
RWKV_SEQS = 4
HGRN_SEQS = 4
LRU_TILE = 512
LRU_SEQS = 16
IN_PROJ_ROWS = 512
MERGE_ROWS = 512
FFN_ROWS = 512
FFN_CHUNK = 256


def _cparams(*sem):
    return pltpu.CompilerParams(dimension_semantics=sem, vmem_limit_bytes=VMEM_LIMIT_BYTES)


def _mm(a, b):
    return jnp.dot(a.astype(BF16), b.astype(BF16), preferred_element_type=F32)


def _mm_nt(a, b):
    return lax.dot_general(a.astype(BF16), b.astype(BF16), (((1,), (1,)), ((), ())),
                           preferred_element_type=F32)


def _mm_tn(a, b):
    return lax.dot_general(a.astype(BF16), b.astype(BF16), (((0,), (0,)), ((), ())),
                           preferred_element_type=F32)


def _split3(x):
    hi = x.astype(BF16)
    r1 = x - hi.astype(F32)
    mid = r1.astype(BF16)
    lo = (r1 - mid.astype(F32)).astype(BF16)
    return hi, mid, lo


def _cumsum_rows(x, seg=None):
    c = x.shape[0]
    row = lax.broadcasted_iota(jnp.int32, (c, c), 0)
    col = lax.broadcasted_iota(jnp.int32, (c, c), 1)
    tri = col <= row
    if seg is not None and seg < c:
        sh = seg.bit_length() - 1
        tri = tri & (lax.shift_right_logical(row, sh) == lax.shift_right_logical(col, sh))
    tri = tri.astype(BF16)
    hi, mid, lo = _split3(x)
    out = jnp.dot(tri, lo, preferred_element_type=F32)
    out = out + jnp.dot(tri, mid, preferred_element_type=F32)
    return out + jnp.dot(tri, hi, preferred_element_type=F32)


def _rms(x, g):
    return x * lax.rsqrt(jnp.mean(x * x, axis=-1, keepdims=True) + RMS_EPS) * g


def _softplus(x):
    return jnp.maximum(x, 0.0) + jnp.log1p(jnp.exp(-jnp.abs(x)))


def _sigmoid(x):
    return 0.5 * jnp.tanh(0.5 * x) + 0.5


def _silu(x):
    return x * _sigmoid(x)


def _lane_block(rows, width, size):
    lane = lax.broadcasted_iota(jnp.int32, (rows, width), 1)
    return lax.shift_right_logical(lane, size.bit_length() - 1)


def _head_sums(x, hd):
    rows, width = x.shape
    per = LANES // hd
    blk = _lane_block(rows, LANES, hd)
    out = []
    for g in range(width // LANES):
        xg = x[:, g * LANES:(g + 1) * LANES]
        acc = None
        for i in reversed(range(per)):
            col = jnp.sum(jnp.where(blk == i, xg, 0.0), axis=-1, keepdims=True)
            acc = jnp.broadcast_to(col, (rows, LANES)) if acc is None else jnp.where(blk == i, col, acc)
        out.append(acc)
    return out[0] if len(out) == 1 else jnp.concatenate(out, axis=1)


def _blockdiag(x, size):
    r, width = x.shape
    blk = _lane_block(r, width, size)
    return jnp.concatenate([jnp.where(blk == i, x, 0.0) for i in range(width // size)], axis=0)


def _run_streams(*gens):
    live = list(gens)
    while live:
        for g in list(live):
            try:
                next(g)
            except StopIteration:
                live.remove(g)


def _tile(n, pref):
    t = min(n, pref)
    while n % t:
        t //= 2
    return t


def _in_proj_kernel(sections, bw, kw, vw, x_ref, g_ref, w_ref, lb_ref, pa_ref, pb_ref, pc_ref):
    xn = _rms(x_ref[...], g_ref[...]).astype(BF16)

    def proj(start, width):
        return jnp.dot(xn, w_ref[:, start:start + width], preferred_element_type=F32)

    o_b = sections[0]
    o_c = o_b + sections[1]
    pa_ref[...] = proj(0, sections[0])
    pb_ref[:, :bw] = proj(o_b, bw)
    pb_ref[:, bw:] = jax.nn.gelu(proj(o_b + bw, bw))
    lb = lb_ref[...]
    pc_ref[:, :kw] = _silu(proj(o_c, kw))
    pc_ref[:, kw:2 * kw] = lb + (1.0 - lb) * _sigmoid(proj(o_c + kw, kw))
    pc_ref[:, 2 * kw:2 * kw + vw] = proj(o_c + 2 * kw, vw)
    pc_ref[:, 2 * kw + vw:] = _silu(proj(o_c + 2 * kw + vw, vw))


def _in_proj(x, g, w, lb, sections, bw, kw, vw, l):
    m, d = x.shape
    tm = _tile(m, IN_PROJ_ROWS)

    def resident(a, cols):
        return pl.BlockSpec((None, a.shape[1], cols), lambda i: (l, 0, 0),
                            pipeline_mode=pl.Buffered(1))

    return pl.pallas_call(
        functools.partial(_in_proj_kernel, sections, bw, kw, vw),
        grid=(m // tm,),
        in_specs=[pl.BlockSpec((tm, d), lambda i: (i, 0)), resident(g, d),
                  resident(w, sum(sections)), resident(lb, kw)],
        out_specs=[pl.BlockSpec((tm, n), lambda i: (i, 0)) for n in sections],
        out_shape=[jax.ShapeDtypeStruct((m, n), F32) for n in sections],
        compiler_params=_cparams("parallel"),
        name="in_proj",
    )(x, g, w, lb)


def _ffn_kernel(cw, x_ref, gpre_ref, wg_ref, wu_ref, wd_ref, gpost_ref, o_ref, h_ref):
    ff = wg_ref.shape[-1]
    xn = _rms(x_ref[...], gpre_ref[...]).astype(BF16)

    def gate_up(i):
        cs = slice(i * cw, (i + 1) * cw)
        return (jnp.dot(xn, wg_ref[:, cs], preferred_element_type=F32),
                jnp.dot(xn, wu_ref[:, cs], preferred_element_type=F32))

    nchunk = ff // cw
    cur = gate_up(0)
    for i in range(nchunk):
        nxt = gate_up(i + 1) if i + 1 < nchunk else None
        h_ref[:, i * cw:(i + 1) * cw] = (_silu(cur[0]) * cur[1]).astype(BF16)
        cur = nxt
    y = jnp.dot(h_ref[...], wd_ref[...], preferred_element_type=F32)
    o_ref[...] = x_ref[...] + 0.5 * _rms(y, gpost_ref[...])


def _ffn(x, gpre, wg, wu, wd, gpost, l):
    m, d = x.shape
    ff = wg.shape[-1]
    tm = _tile(m, FFN_ROWS)
    cw = FFN_CHUNK if ff % FFN_CHUNK == 0 else ff

    def resident(a):
        return pl.BlockSpec((None,) + a.shape[1:], lambda i: (l,) + (0,) * (a.ndim - 1),
                            pipeline_mode=pl.Buffered(1))

    return pl.pallas_call(
        functools.partial(_ffn_kernel, cw),
        grid=(m // tm,),
        in_specs=[pl.BlockSpec((tm, d), lambda i: (i, 0)), resident(gpre), resident(wg),
                  resident(wu), resident(wd), resident(gpost)],
        out_specs=pl.BlockSpec((tm, d), lambda i: (i, 0)),
        out_shape=jax.ShapeDtypeStruct((m, d), F32),
        scratch_shapes=[pltpu.VMEM((tm, ff), BF16)],
        compiler_params=_cparams("parallel"),
        name="ffn",
    )(x, gpre, wg, wu, wd, gpost)


def _merge_kernel(goff, x_ref, gpre_ref, win_ref, oa_ref, ob_ref, oc_ref, pa_ref, pb_ref, pc_ref,
                  wo_ref, g_ref, o_ref):
    d = x_ref.shape[-1]
    x = x_ref[...]
    xn = _rms(x, gpre_ref[...]).astype(BF16)
    merged = None
    for i, (o_b, p_b) in enumerate(((oa_ref, pa_ref), (ob_ref, pb_ref), (oc_ref, pc_ref))):
        gate = jnp.dot(xn, win_ref[:, goff + i * d:goff + (i + 1) * d], preferred_element_type=F32)
        term = _sigmoid(gate) * _mm(o_b[...], p_b[...])
        merged = term if merged is None else merged + term
    y = _mm(merged, wo_ref[...])
    o_ref[...] = x + _rms(y, g_ref[...])


def _merge(x, gpre, w_in, goff, oa, ob, oc, pa, pb, pc, wo, g, l):
    m, d = x.shape
    tm = _tile(m, MERGE_ROWS)

    def rows(w):
        return pl.BlockSpec((tm, w), lambda i: (i, 0))

    def whole(a):
        return pl.BlockSpec((None,) + a.shape[1:], lambda i: (l,) + (0,) * (a.ndim - 1),
                            pipeline_mode=pl.Buffered(1))

    return pl.pallas_call(
        functools.partial(_merge_kernel, goff),
        grid=(m // tm,),
        in_specs=[rows(d), whole(gpre), whole(w_in), rows(oa.shape[1]), rows(ob.shape[1]),
                  rows(oc.shape[1]), whole(pa), whole(pb), whole(pc), whole(wo), whole(g)],
        out_specs=rows(d),
        out_shape=jax.ShapeDtypeStruct((m, d), F32),
        compiler_params=_cparams("parallel"),
        name="merge_out",
    )(x, gpre, w_in, oa, ob, oc, pa, pb, pc, wo, g)


def _rwkv_prep(heads, hd, dw, da, nb, clen, p_ref, r0, prm, prev_ref, pf_ref, ops):
    (mu_ref, w0_ref, w2_ref, a0_ref, a2_ref, g2_ref, kk_ref, ka_ref, rk_ref) = prm
    l2_ref, bt_ref, kt_ref, h2_ref, v_ref, bv_ref, g_ref, egl_ref = ops
    aw = heads * hd
    seq = [slice(j * clen, (j + 1) * clen) for j in range(nb)]

    p = jnp.concatenate([p_ref[j, r0:r0 + clen, :] for j in range(nb)], axis=0)
    pf_ref[...] = pltpu.roll(p, 1, 0)
    for j in range(nb):
        pf_ref[j * clen:j * clen + 1, :] = prev_ref[j:j + 1, :]
        prev_ref[j:j + 1, :] = p_ref[j, r0 + clen - 1:r0 + clen, :]
    xs = p + mu_ref[...] * (pf_ref[...] - p)
    yield
    r = xs[:, :aw]
    k = xs[:, aw:2 * aw]
    v = xs[:, 2 * aw:3 * aw]
    w1 = xs[:, 3 * aw:3 * aw + dw]
    a1 = xs[:, 3 * aw + dw:3 * aw + dw + da]
    g1 = xs[:, 3 * aw + dw + da:]
    w = -_softplus(-(w0_ref[...] + _mm(jnp.tanh(w1), w2_ref[...]))) - 0.5
    logw = -jnp.exp(w)
    yield
    a_icl = _sigmoid(a0_ref[...] + _mm(a1, a2_ref[...]))
    gate = _mm(_sigmoid(g1), g2_ref[...])
    for j in range(nb):
        g_ref[j] = gate[seq[j]]
    yield
    kkraw = k * kk_ref[...]
    k2 = k * (1.0 + (a_icl - 1.0) * ka_ref[...])
    rkk = r * k2 * rk_ref[...]
    ginc = _cumsum_rows(logw, clen)
    yield
    eg = jnp.exp(ginc)
    egx = jnp.exp(ginc - logw)
    einv = jnp.exp(-ginc)
    yield
    scale = lax.rsqrt(jnp.maximum(_head_sums(kkraw * kkraw, hd), KK_EPS * KK_EPS))
    yield
    bonus = _head_sums(rkk, hd)
    kk = kkraw * scale
    bvec = kk * a_icl
    yield
    for j in range(nb):
        l2_ref[j, :clen, :] = -kk[seq[j]] * egx[seq[j]]
        l2_ref[j, clen:, :] = r[seq[j]] * eg[seq[j]]
    yield
    for j in range(nb):
        bt_ref[j] = bvec[seq[j]] * einv[seq[j]]
        kt_ref[j] = k2[seq[j]] * einv[seq[j]]
    yield
    for j in range(nb):
        glast = ginc[(j + 1) * clen - 1:(j + 1) * clen, :]
        ehat = jnp.exp(glast - ginc[seq[j]])
        h2_ref[j, :clen, :] = bvec[seq[j]] * ehat
        h2_ref[j, clen:, :] = k2[seq[j]] * ehat
        egl_ref[j] = jnp.exp(glast)
    yield
    for j in range(nb):
        v_ref[j] = v[seq[j]]
        bv_ref[j] = bonus[seq[j]] * v[seq[j]]
    yield


def _rwkv_chain(heads, hd, nb, clen, ops, st_ref, lng_ref, lnb_ref, o_ref, o_r0):
    l2_ref, bt_ref, kt_ref, h2_ref, v_ref, bv_ref, g_ref, egl_ref = ops
    per = LANES // hd
    cw = per * clen
    units = [(j, g) for j in range(nb) for g in range(heads // per)]

    def ld(ref):
        return [ref[j, :, g * LANES:(g + 1) * LANES] for j, g in units]

    ri = lax.broadcasted_iota(jnp.int32, (2 * clen, cw), 0)
    cpos = lax.broadcasted_iota(jnp.int32, (2 * clen, cw), 1) & (clen - 1)
    mask2 = cpos < jnp.where(ri < clen, ri, ri - (clen - 1))
    eye = ((lax.broadcasted_iota(jnp.int32, (clen, cw), 1) & (clen - 1))
           == lax.broadcasted_iota(jnp.int32, (clen, cw), 0)).astype(F32)
    blk = _lane_block(hd, LANES, hd)

    lhs2 = ld(l2_ref)
    ab2 = [jnp.where(mask2, _mm_nt(x, _blockdiag(y, hd)), 0.0) for x, y in zip(lhs2, ld(bt_ref))]
    yield
    ak2 = [jnp.where(mask2, _mm_nt(x, _blockdiag(y, hd)), 0.0) for x, y in zip(lhs2, ld(kt_ref))]
    yield
    s0 = [st_ref[j, g] for j, g in units]
    vs = ld(v_ref)
    xy = [_mm_nt(x, _blockdiag(s, hd)) + _mm(a, _blockdiag(vv, hd))
          for x, s, a, vv in zip(lhs2, s0, ak2, vs)]
    yield
    pw = [m[:clen] for m in ab2]
    tinv = [eye + m for m in pw]
    span = 2
    while span < clen:
        pwd = [_blockdiag(m, clen) for m in pw]
        pw = [_mm(m, md) for m, md in zip(pw, pwd)]
        yield
        pwd = [_blockdiag(m, clen) for m in pw]
        tinv = [t + _mm(t, md) for t, md in zip(tinv, pwd)]
        yield
        span *= 2
    u = [_mm(t, _blockdiag(m[:clen], hd)) for t, m in zip(tinv, xy)]
    yield
    y = [m[clen:] + _mm(a[clen:], _blockdiag(uu, hd)) for m, a, uu in zip(xy, ab2, u)]
    yield
    full = [_mm_tn(jnp.concatenate([uu, vv], axis=0), hh) for uu, vv, hh in zip(u, vs, ld(h2_ref))]
    for (j, g), s, f in zip(units, s0, full):
        own = f[(per - 1) * hd:, :]
        for i in reversed(range(per - 1)):
            own = jnp.where(blk == i, f[i * hd:(i + 1) * hd, :], own)
        st_ref[j, g] = s * egl_ref[j, :, g * LANES:(g + 1) * LANES] + own
    yield
    bvs = ld(bv_ref)
    gates = ld(g_ref)
    for i, (j, g) in enumerate(units):
        sl = slice(g * LANES, (g + 1) * LANES)
        yc = y[i] - _head_sums(y[i], hd) * (1.0 / hd)
        var = _head_sums(yc * yc, hd) * (1.0 / hd)
        yy = yc * lax.rsqrt(var + GN_EPS) * lng_ref[:, sl] + lnb_ref[:, sl]
        o_ref[j, o_r0:o_r0 + clen, sl] = (yy + bvs[i]) * gates[i]
        if i % (heads // per) == heads // per - 1:
            yield


def _rwkv_step_kernel(heads, hd, dw, da, nb, clen, nsub, *refs):
    refs = list(refs)
    p_ref = refs.pop(0)
    pn_ref = refs.pop(0) if nsub == 2 else None
    shift_ref, s0_ref = refs[:2]
    prm = refs[2:11]
    lng_ref, lnb_ref = refs[11:13]
    rest = refs[14:]
    o_ref, nshift_ref, ns_ref, st_ref, prev_ref, pf_ref = rest[:6]
    ops_a = rest[6:14]
    ops_b = rest[14:22]
    c = pl.program_id(1)
    per = LANES // hd
    groups = [(j, g) for j in range(nb) for g in range(heads // per)]

    @pl.when(c == 0)
    def _():
        prev_ref[...] = shift_ref[...]
        for j, g in groups:
            st_ref[j, g] = jnp.concatenate([s0_ref[j, g * per + i] for i in range(per)], axis=1)

    def prep(src, r0, ops):
        return _rwkv_prep(heads, hd, dw, da, nb, clen, src, r0, prm, prev_ref, pf_ref, ops)

    def chain(ops, o_r0):
        return _rwkv_chain(heads, hd, nb, clen, ops, st_ref, lng_ref, lnb_ref, o_ref, o_r0)

    if nsub == 1:
        _run_streams(prep(p_ref, 0, ops_a))
        _run_streams(chain(ops_a, 0))
    else:
        @pl.when(c == 0)
        def _():
            _run_streams(prep(p_ref, 0, ops_a))

        _run_streams(chain(ops_a, 0), prep(p_ref, clen, ops_b))
        _run_streams(chain(ops_b, clen), prep(pn_ref, 0, ops_a))

    @pl.when(c == pl.num_programs(1) - 1)
    def _():
        nshift_ref[...] = prev_ref[...]
        for j, g in groups:
            for i in range(per):
                ns_ref[j, g * per + i] = st_ref[j, g, :, i * hd:(i + 1) * hd]


def _rwkv_mix(pa, shift, s0, stacked, P, l, nseq, tlen, chunk, nb):
    ap = pa.shape[1]
    depth, _, heads, hd, _ = s0.shape
    aw = heads * hd
    dw = P['rwkv_w2'].shape[1]
    da = P['rwkv_a2'].shape[1]
    nch = tlen // chunk
    nsub = 2 if nch % 2 == 0 else 1
    assert nsub == 2 or nch == 1
    nstep = nch // nsub
    ngrp = nseq // nb
    shift = shift.reshape(depth, ngrp, nb, ap)
    pa = pa.reshape(nseq, tlen, ap)

    def vec(a):
        return pl.BlockSpec((None,) + a.shape[1:], lambda b, c: (l,) + (0,) * (a.ndim - 1))

    names = ['rwkv_mu', 'rwkv_w0', 'rwkv_w2', 'rwkv_a0', 'rwkv_a2', 'rwkv_g2', 'rwkv_k_k',
             'rwkv_k_a', 'rwkv_r_k', 'rwkv_ln_g', 'rwkv_ln_b']
    params = [P[n] for n in names]
    args = [pa]
    in_specs = [pl.BlockSpec((nb, nsub * chunk, ap), lambda b, c: (b, c, 0))]
    if nsub == 2:
        args.append(pa)
        in_specs.append(pl.BlockSpec(
            (nb, chunk, ap), lambda b, c: (b, jnp.minimum(2 * c + 2, nch - 1), 0)))
    args += [shift, s0] + params
    in_specs += [
        pl.BlockSpec((None, None, nb, ap), lambda b, c: (l, b, 0, 0)),
        pl.BlockSpec((None, nb, heads, hd, hd), lambda b, c: (l, b, 0, 0, 0)),
    ] + [vec(a) for a in params]
    aliases = {len(args): 2}
    args.append(stacked)
    in_specs.append(pl.BlockSpec(memory_space=pl.ANY))

    def operand_set():
        f = lambda r: pltpu.VMEM((nb, r, aw), F32)
        return [f(2 * chunk), f(chunk), f(chunk), f(2 * chunk), f(chunk), f(chunk), f(chunk), f(1)]

    o, n_shift, stacked = pl.pallas_call(
        functools.partial(_rwkv_step_kernel, heads, hd, dw, da, nb, chunk, nsub),
        grid=(ngrp, nstep),
        in_specs=in_specs,
        out_specs=[
            pl.BlockSpec((nb, nsub * chunk, aw), lambda b, c: (b, c, 0)),
            pl.BlockSpec((None, nb, ap), lambda b, c: (b, 0, 0)),
            pl.BlockSpec((None, nb, heads, hd, hd), lambda b, c: (l, b, 0, 0, 0)),
        ],
        out_shape=[
            jax.ShapeDtypeStruct((nseq, tlen, aw), F32),
            jax.ShapeDtypeStruct((ngrp, nb, ap), F32),
            jax.ShapeDtypeStruct((depth, nseq, heads, hd, hd), F32),
        ],
        scratch_shapes=[pltpu.VMEM((nb, aw // LANES, hd, LANES), F32), pltpu.VMEM((nb, ap), F32),
                        pltpu.VMEM((nb * chunk, ap), F32)] + operand_set() + operand_set(),
        input_output_aliases=aliases,
        compiler_params=_cparams("parallel", "arbitrary"),
        name="rwkv7",
    )(*args)
    return o.reshape(nseq * tlen, aw), n_shift.reshape(nseq, ap), stacked


def _lru_kernel(bw, seqs, p_ref, cs_ref, h0_ref, cw_ref, cb_ref, wa_ref, ba_ref, wx_ref, bx_ref,
                lam_ref, o_ref, nconv_ref, nh_ref, tail_ref, h_ref):
    c = pl.program_id(1)
    rows = p_ref.shape[0]
    ncv = cs_ref.shape[1]
    nseg = rows // SUBLANES

    if seqs == 1:
        @pl.when(c == 0)
        def _():
            tail_ref[...] = jnp.zeros_like(tail_ref)
            tail_ref[SUBLANES - ncv:, :] = cs_ref[0]
            h_ref[...] = h0_ref[0]
    else:
        tail_ref[...] = jnp.zeros_like(tail_ref)
        for j in range(seqs):
            tail_ref[(j + 1) * SUBLANES - ncv:(j + 1) * SUBLANES, :] = cs_ref[j]

    xb = p_ref[:, :bw]
    gb = p_ref[:, bw:]
    rowi = lax.broadcasted_iota(jnp.int32, (rows, 1), 0)
    pos = rowi & (SUBLANES - 1)

    def seg_roll(x, d):
        return pltpu.roll(x.reshape(nseg, SUBLANES, bw), d, 1).reshape(rows, bw)

    if seqs == 1 and rows > SUBLANES:
        before = jnp.concatenate([tail_ref[...], xb[:rows - SUBLANES, :]], axis=0)
    else:
        before = tail_ref[...]
    xc = cb_ref[...] + xb * cw_ref[ncv:ncv + 1, :]
    for d in range(1, ncv + 1):
        sh = jnp.where(pos >= d, seg_roll(xb, d), seg_roll(before, d))
        xc = xc + sh * cw_ref[ncv - d:ncv - d + 1, :]
    if seqs == 1:
        tail_ref[...] = xb[rows - SUBLANES:, :]

    rg = _sigmoid(_mm(xc, wa_ref[...]) + ba_ref[...])
    ig = _sigmoid(_mm(xc, wx_ref[...]) + bx_ref[...])
    log_a = -LRU_C * rg * _softplus(-lam_ref[...])
    a = jnp.exp(log_a)
    b = jnp.sqrt(-jnp.tanh(log_a) * (1.0 + a * a)) * ig * xc

    d = 1
    while d < SUBLANES:
        keep = pos >= d
        a_sh = jnp.where(keep, seg_roll(a, d), 1.0)
        b_sh = jnp.where(keep, seg_roll(b, d), 0.0)
        b = a * b_sh + b
        a = a_sh * a
        d *= 2
    pieces = []
    hc = h_ref[...] if seqs == 1 else None
    for s in range(nseg):
        rs = slice(s * SUBLANES, (s + 1) * SUBLANES)
        h_s = a[rs] * (hc if seqs == 1 else h0_ref[s]) + b[rs]
        hc = h_s[SUBLANES - 1:, :]
        pieces.append(h_s)
    h = pieces[0] if nseg == 1 else jnp.concatenate(pieces, axis=0)
    o_ref[...] = h * gb

    if seqs == 1:
        h_ref[...] = hc

        @pl.when(c == pl.num_programs(1) - 1)
        def _():
            nconv_ref[0] = xb[rows - ncv:, :]
            nh_ref[0] = hc
    else:
        for s in range(seqs):
            nconv_ref[s] = xb[(s + 1) * SUBLANES - ncv:(s + 1) * SUBLANES, :]
            nh_ref[s] = pieces[s][SUBLANES - 1:, :]


def _lru(pb, conv, h0, P, l, nseq, tlen, tile, seqs):
    bw = h0.shape[-1]
    ncv = conv.shape[2]
    nt = tlen // tile
    assert seqs == 1 or (nt == 1 and tile == SUBLANES)
    rows = seqs * tile

    def vec(a):
        return pl.BlockSpec((None,) + a.shape[1:], lambda b, c: (l,) + (0,) * (a.ndim - 1))

    names = ['lru_conv_w', 'lru_conv_b', 'lru_wa_bd', 'lru_ba', 'lru_wx_bd', 'lru_bx', 'lru_lam']
    params = [P[n] for n in names]
    return pl.pallas_call(
        functools.partial(_lru_kernel, bw, seqs),
        grid=(nseq // seqs, nt),
        in_specs=[
            pl.BlockSpec((rows, 2 * bw), lambda b, c: (b * nt + c, 0)),
            pl.BlockSpec((None, seqs, ncv, bw), lambda b, c: (l, b, 0, 0)),
            pl.BlockSpec((None, seqs, 1, bw), lambda b, c: (l, b, 0, 0)),
        ] + [vec(a) for a in params],
        out_specs=[
            pl.BlockSpec((rows, bw), lambda b, c: (b * nt + c, 0)),
            pl.BlockSpec((seqs, ncv, bw), lambda b, c: (b, 0, 0)),
            pl.BlockSpec((seqs, 1, bw), lambda b, c: (b, 0, 0)),
        ],
        out_shape=[
            jax.ShapeDtypeStruct((nseq * tlen, bw), F32),
            jax.ShapeDtypeStruct((nseq, ncv, bw), F32),
            jax.ShapeDtypeStruct((nseq, 1, bw), F32),
        ],
        scratch_shapes=[pltpu.VMEM((SUBLANES if seqs == 1 else rows, bw), F32),
                        pltpu.VMEM((1, bw), F32)],
        compiler_params=_cparams("parallel", "arbitrary"),
        name="rglru",
    )(pb, conv, h0, *params)


def _hgrn_kernel(heads, kd, vd, sub, nb, clen, p_ref, s0_ref, ng_ref, *rest):
    o_ref, ns_ref, st_ref = rest[-3:]
    c = pl.program_id(1)
    kw = heads * kd
    vw = heads * vd
    pairs = [(j, h) for j in range(nb) for h in range(heads)]

    @pl.when(c == 0)
    def _():
        for j, h in pairs:
            st_ref[j, h] = s0_ref[j, h].T

    p = p_ref[0] if nb == 1 else jnp.concatenate([p_ref[j] for j in range(nb)], axis=0)
    q = p[:, :kw]
    f = p[:, kw:2 * kw]
    logf = jnp.log(f)
    kf = 1.0 - f
    v = p[:, 2 * kw:2 * kw + vw]
    gzs = p[:, 2 * kw + vw:]

    g = _cumsum_rows(logf, clen)
    qg = q * jnp.exp(g)
    glast = [g[(j + 1) * clen - 1:(j + 1) * clen, :] for j in range(nb)]
    khat = [kf[j * clen:(j + 1) * clen, :] * jnp.exp(glast[j] - g[j * clen:(j + 1) * clen, :])
            for j in range(nb)]
    eglast = [jnp.exp(x) for x in glast]

    rows = nb * clen
    nblk = clen // sub
    rowi = lax.broadcasted_iota(jnp.int32, (rows, 1), 0)

    def usl(x, j, h, width):
        return x[j * clen:(j + 1) * clen, h * width:(h + 1) * width]

    st = [st_ref[j, h] for j, h in pairs]
    vs = [usl(v, j, h, vd) for j, h in pairs]
    o = [_mm_nt(usl(qg, j, h, kd), s) for (j, h), s in zip(pairs, st)]
    st_new = [s * eglast[j][:, h * kd:(h + 1) * kd] + _mm_tn(vv, khat[j][:, h * kd:(h + 1) * kd])
              for (j, h), s, vv in zip(pairs, st, vs)]
    for (j, h), s in zip(pairs, st_new):
        st_ref[j, h] = s

    ri = lax.broadcasted_iota(jnp.int32, (clen, clen), 0)
    ci = lax.broadcasted_iota(jnp.int32, (clen, clen), 1)
    att = None
    hs = sub
    while hs < clen:
        later = (rowi & hs) != 0
        gref = jnp.concatenate(
            [jnp.broadcast_to(g[r0 + hs - 1:r0 + hs, :], (2 * hs, kw)) for r0 in range(0, rows, 2 * hs)],
            axis=0)
        q_rel = q * jnp.exp(jnp.where(later, g - gref, -jnp.inf))
        k_rel = kf * jnp.exp(jnp.where(later, -jnp.inf, gref - g))
        sh = (2 * hs).bit_length() - 1
        same = lax.shift_right_logical(ri, sh) == lax.shift_right_logical(ci, sh)
        part = [jnp.where(same, _mm_nt(usl(q_rel, j, h, kd), usl(k_rel, j, h, kd)), 0.0)
                for j, h in pairs]
        att = part if att is None else [a + b for a, b in zip(att, part)]
        hs *= 2
    if att is not None:
        o = [oo + _mm(a, vv) for oo, a, vv in zip(o, att, vs)]

    def blocks3(x, j, h):
        return usl(x, j, h, kd).reshape(nblk, sub, kd)

    f3 = [blocks3(f, j, h) for j, h in pairs]
    q3 = [blocks3(q, j, h) for j, h in pairs]
    k3 = [blocks3(kf, j, h) for j, h in pairs]
    srow = lax.broadcasted_iota(jnp.int32, (nblk, sub, 1), 1)
    tcol = (lax.broadcasted_iota(jnp.int32, (nblk, sub, LANES), 2)
            - sub * lax.broadcasted_iota(jnp.int32, (nblk, sub, LANES), 0))
    att_t = [jnp.zeros((nblk, sub, LANES), F32) for _ in pairs]
    dec = [jnp.zeros((nblk, sub, kd), F32) for _ in pairs]
    for t in range(sub):
        dec = [jnp.where(srow == t, kk, d * ff[:, t:t + 1, :]) for d, ff, kk in zip(dec, f3, k3)]
        cols = [jnp.sum(qq[:, t:t + 1, :] * d, axis=-1, keepdims=True) for d, qq in zip(dec, q3)]
        att_t = [jnp.where(tcol == t, col, a) for col, a in zip(cols, att_t)]
    o = [oo + _mm_tn(a.reshape(clen, LANES)[:, :clen], vv) for oo, a, vv in zip(o, att_t, vs)]

    for (j, h), oo in zip(pairs, o):
        hv = slice(h * vd, (h + 1) * vd)
        oo = oo * lax.rsqrt(jnp.mean(oo * oo, axis=-1, keepdims=True) + RMS_EPS) * ng_ref[:, hv]
        o_ref[j, :, hv] = oo * gzs[j * clen:(j + 1) * clen, hv]

    @pl.when(c == pl.num_programs(1) - 1)
    def _():
        for j, h in pairs:
            ns_ref[j, h] = st_ref[j, h].T


def _hgrn(pc, s0, stacked, ng, l, nseq, tlen, chunk, nb):
    depth, _, heads, kd, vd = s0.shape
    width = pc.shape[1]
    nch = tlen // chunk
    sub = min(HGRN_SUB, chunk)
    args = [pc.reshape(nseq, tlen, width), s0, ng]
    in_specs = [
        pl.BlockSpec((nb, chunk, width), lambda b, c: (b, c, 0)),
        pl.BlockSpec((None, nb, heads, kd, vd), lambda b, c: (l, b, 0, 0, 0)),
        pl.BlockSpec((None, 1, heads * vd), lambda b, c: (l, 0, 0)),
    ]
    aliases = {len(args): 1}
    args.append(stacked)
    in_specs.append(pl.BlockSpec(memory_space=pl.ANY))
    o, stacked = pl.pallas_call(
        functools.partial(_hgrn_kernel, heads, kd, vd, sub, nb, chunk),
        grid=(nseq // nb, nch),
        in_specs=in_specs,
        out_specs=[
            pl.BlockSpec((nb, chunk, heads * vd), lambda b, c: (b, c, 0)),
            pl.BlockSpec((None, nb, heads, kd, vd), lambda b, c: (l, b, 0, 0, 0)),
        ],
        out_shape=[
            jax.ShapeDtypeStruct((nseq, tlen, heads * vd), F32),
            jax.ShapeDtypeStruct((depth, nseq, heads, kd, vd), F32),
        ],
        scratch_shapes=[pltpu.VMEM((nb, heads, vd, kd), F32)],
        input_output_aliases=aliases,
        compiler_params=_cparams("parallel", "arbitrary"),
        name="hgrn2",
    )(*args)
    return o.reshape(nseq * tlen, heads * vd), stacked


def _run_trunk(x, s_rwkv, s_shift, s_lru, s_conv, s_hgrn, P):
    nseq, tlen, d = x.shape
    depth = s_rwkv.shape[0]
    x = x.reshape(nseq * tlen, d)
    lru4 = s_lru[:, :, None, :]
    rw_chunk = min(RWKV_CHUNK, tlen)
    rw_nb = _tile(nseq, max(RWKV_SEQS, RWKV_CHUNK // rw_chunk))
    hg_chunk = min(HGRN_CHUNK, tlen)
    hg_nb = _tile(nseq, max(HGRN_SEQS, HGRN_CHUNK // hg_chunk))
    lru_tile = min(LRU_TILE, tlen)
    lru_seqs = _tile(nseq, LRU_SEQS) if lru_tile == tlen == SUBLANES else 1
    outs = ([], [], [])
    n_rwkv = jnp.zeros(s_rwkv.shape, F32)
    n_hgrn = jnp.zeros(s_hgrn.shape, F32)
    for l in range(depth):
        x = _ffn(x, P['ffn1_pre_g'], P['ffn1_wg'], P['ffn1_wu'], P['ffn1_wd'], P['ffn1_post_g'], l)
        pa, pb, pc = _in_proj(x, P['mix_pre_g'], P['w_in'], P['hgrn_lb'], P['w_in_sections'],
                              s_lru.shape[-1], s_hgrn.shape[2] * s_hgrn.shape[3],
                              s_hgrn.shape[2] * s_hgrn.shape[4], l)
        oa, n_shift, n_rwkv = _rwkv_mix(pa, s_shift, s_rwkv, n_rwkv, P, l, nseq, tlen, rw_chunk,
                                        rw_nb)
        ob, n_conv, n_lru = _lru(pb, s_conv, lru4, P, l, nseq, tlen, lru_tile, lru_seqs)
        oc, n_hgrn = _hgrn(pc, s_hgrn, n_hgrn, P['hgrn_norm_g'], l, nseq, tlen, hg_chunk, hg_nb)
        x = _merge(x, P['mix_pre_g'], P['w_in'], sum(P['w_in_sections']), oa, ob, oc, P['proj_a'],
                   P['proj_b'], P['proj_c'], P['w_out'], P['mix_post_g'], l)
        x = _ffn(x, P['ffn2_pre_g'], P['ffn2_wg'], P['ffn2_wu'], P['ffn2_wd'], P['ffn2_post_g'], l)
        for lst, t in zip(outs, (n_shift, n_lru[:, 0], n_conv)):
            lst.append(t)
    n_shift, n_lru, n_conv = (jnp.stack(lst, axis=0) for lst in outs)
    return x.reshape(nseq, tlen, d), (n_rwkv, n_shift, n_lru, n_conv, n_hgrn)


def _block_diag(w):
    depth, g, i, j = w.shape
    eye = jnp.eye(g, dtype=w.dtype)
    return jnp.einsum('lgij,gh->lgihj', w, eye).reshape(depth, g * i, g * j)


def _prepare_params(raw, a_proj, b_width, c_kwidth, c_width):
    P = {}
    for n in ('ffn1_wg', 'ffn1_wu', 'ffn1_wd', 'ffn2_wg', 'ffn2_wu', 'ffn2_wd', 'proj_a', 'proj_b',
              'proj_c', 'w_out', 'w_in', 'rwkv_w2', 'rwkv_a2', 'rwkv_g2'):
        P[n] = raw[n].astype(BF16)
    P['w_in_sections'] = (a_proj, 2 * b_width, 2 * c_kwidth + 2 * c_width)
    for n in ('ffn1_pre_g', 'ffn1_post_g', 'mix_pre_g', 'mix_post_g', 'ffn2_pre_g', 'ffn2_post_g',
              'rwkv_mu', 'rwkv_w0', 'rwkv_a0', 'rwkv_k_k', 'rwkv_k_a', 'rwkv_ln_g', 'rwkv_ln_b',
              'lru_conv_b', 'lru_ba', 'lru_bx', 'lru_lam', 'hgrn_norm_g'):
        P[n] = raw[n][:, None, :]
    depth = raw['rwkv_r_k'].shape[0]
    P['rwkv_r_k'] = raw['rwkv_r_k'].reshape(depth, 1, -1)
    P['lru_conv_w'] = raw['lru_conv_w']
    P['lru_wa_bd'] = _block_diag(raw['lru_wa']).astype(BF16)
    P['lru_wx_bd'] = _block_diag(raw['lru_wx']).astype(BF16)
    lb_cum = jnp.cumsum(jax.nn.softmax(raw['hgrn_lb_logits'].astype(F32), axis=0), axis=0)
    P['hgrn_lb'] = (lb_cum - lb_cum[0])[:, None, :]
    return P


def kernel(x_prompt, x_sample, state_rwkv, state_shift, state_lru, state_conv, state_hgrn, ffn1_pre_g, ffn1_post_g, ffn1_wg, ffn1_wu, ffn1_wd, mix_pre_g, mix_post_g, w_in, rwkv_mu, rwkv_w0, rwkv_w2, rwkv_a0, rwkv_a2, rwkv_g2, rwkv_k_k, rwkv_k_a, rwkv_r_k, rwkv_ln_g, rwkv_ln_b, lru_conv_w, lru_conv_b, lru_wa, lru_ba, lru_wx, lru_bx, lru_lam, hgrn_lb_logits, hgrn_norm_g, proj_a, proj_b, proj_c, w_out, ffn2_pre_g, ffn2_post_g, ffn2_wg, ffn2_wu, ffn2_wd):
    raw = dict(
        ffn1_pre_g=ffn1_pre_g, ffn1_post_g=ffn1_post_g, ffn1_wg=ffn1_wg, ffn1_wu=ffn1_wu,
        ffn1_wd=ffn1_wd, mix_pre_g=mix_pre_g, mix_post_g=mix_post_g, w_in=w_in, rwkv_mu=rwkv_mu,
        rwkv_w0=rwkv_w0, rwkv_w2=rwkv_w2, rwkv_a0=rwkv_a0, rwkv_a2=rwkv_a2, rwkv_g2=rwkv_g2,
        rwkv_k_k=rwkv_k_k, rwkv_k_a=rwkv_k_a, rwkv_r_k=rwkv_r_k, rwkv_ln_g=rwkv_ln_g,
        rwkv_ln_b=rwkv_ln_b, lru_conv_w=lru_conv_w, lru_conv_b=lru_conv_b, lru_wa=lru_wa,
        lru_ba=lru_ba, lru_wx=lru_wx, lru_bx=lru_bx, lru_lam=lru_lam,
        hgrn_lb_logits=hgrn_lb_logits, hgrn_norm_g=hgrn_norm_g, proj_a=proj_a, proj_b=proj_b,
        proj_c=proj_c, w_out=w_out, ffn2_pre_g=ffn2_pre_g, ffn2_post_g=ffn2_post_g,
        ffn2_wg=ffn2_wg, ffn2_wu=ffn2_wu, ffn2_wd=ffn2_wd)
    depth, _, heads, hd, _ = state_rwkv.shape
    a_proj = state_shift.shape[-1]
    b_width = state_lru.shape[-1]
    _, _, c_heads, c_kd, c_vd = state_hgrn.shape
    P = _prepare_params(raw, a_proj, b_width, c_heads * c_kd, c_heads * c_vd)
    dt = state_rwkv.dtype
    bp = x_prompt.shape[0]
    y_prompt, p_st = _run_trunk(
        x_prompt,
        jnp.zeros((depth, bp, heads, hd, hd), dt),
        jnp.zeros((depth, bp, a_proj), dt),
        jnp.zeros((depth, bp, b_width), dt),
        jnp.zeros((depth, bp) + state_conv.shape[2:], dt),
        jnp.zeros((depth, bp, c_heads, c_kd, c_vd), dt),
        P)
    y_sample, s_st = _run_trunk(x_sample, state_rwkv, state_shift, state_lru, state_conv,
                                state_hgrn, P)
    return (y_prompt, y_sample) + tuple(t.astype(dt) for t in p_st) + tuple(t.astype(dt) for t in s_st)
```

```python
import functools

import jax
import jax.numpy as jnp
from jax import lax
from jax.experimental import pallas as pl
from jax.experimental.pallas import tpu as pltpu

F32 = jnp.float32
BF16 = jnp.bfloat16
RMS_EPS = 1e-6
GN_EPS = 64e-5
LRU_C = 8.0
KK_EPS = 1e-12

VMEM_LIMIT_BYTES = 48 * 1024 * 1024
SUBLANES = 8
LANES = 128
HGRN_SUB = 8
RWKV_CHUNK = 64
HGRN_CHUNK = 64
RWKV_SEQS = 4
HGRN_SEQS = 4
LRU_TILE = 512
LRU_SEQS = 16
IN_PROJ_ROWS = 512
MERGE_ROWS = 512
FFN_ROWS = 512
FFN_CHUNK = 256


def _cparams(*sem):
    return pltpu.CompilerParams(dimension_semantics=sem, vmem_limit_bytes=VMEM_LIMIT_BYTES)


def _mm(a, b):
    return jnp.dot(a.astype(BF16), b.astype(BF16), preferred_element_type=F32)


def _mm_nt(a, b):
    return lax.dot_general(a.astype(BF16), b.astype(BF16), (((1,), (1,)), ((), ())),
                           preferred_element_type=F32)


def _mm_tn(a, b):
    return lax.dot_general(a.astype(BF16), b.astype(BF16), (((0,), (0,)), ((), ())),
                           preferred_element_type=F32)


def _split3(x):
    hi = x.astype(BF16)
    r1 = x - hi.astype(F32)
    mid = r1.astype(BF16)
    lo = (r1 - mid.astype(F32)).astype(BF16)
    return hi, mid, lo


def _cumsum_rows(x, seg=None):
    c = x.shape[0]
    row = lax.broadcasted_iota(jnp.int32, (c, c), 0)
    col = lax.broadcasted_iota(jnp.int32, (c, c), 1)
    tri = col <= row
    if seg is not None and seg < c:
        sh = seg.bit_length() - 1
        tri = tri & (lax.shift_right_logical(row, sh) == lax.shift_right_logical(col, sh))
    tri = tri.astype(BF16)
    hi, mid, lo = _split3(x)
    out = jnp.dot(tri, lo, preferred_element_type=F32)
    out = out + jnp.dot(tri, mid, preferred_element_type=F32)
    return out + jnp.dot(tri, hi, preferred_element_type=F32)


def _rms(x, g):
    return x * lax.rsqrt(jnp.mean(x * x, axis=-1, keepdims=True) + RMS_EPS) * g


def _softplus(x):
    return jnp.maximum(x, 0.0) + jnp.log1p(jnp.exp(-jnp.abs(x)))


def _sigmoid(x):
    return 0.5 * jnp.tanh(0.5 * x) + 0.5


def _silu(x):
    return x * _sigmoid(x)


def _lane_block(rows, width, size):
    lane = lax.broadcasted_iota(jnp.int32, (rows, width), 1)
    return lax.shift_right_logical(lane, size.bit_length() - 1)


def _head_sums(x, hd):
    rows, width = x.shape
    per = LANES // hd
    blk = _lane_block(rows, LANES, hd)
    out = []
    for g in range(width // LANES):
        xg = x[:, g * LANES:(g + 1) * LANES]
        acc = None
        for i in reversed(range(per)):
            col = jnp.sum(jnp.where(blk == i, xg, 0.0), axis=-1, keepdims=True)
            acc = jnp.broadcast_to(col, (rows, LANES)) if acc is None else jnp.where(blk == i, col, acc)
        out.append(acc)
    return out[0] if len(out) == 1 else jnp.concatenate(out, axis=1)


def _blockdiag(x, size):
    r, width = x.shape
    blk = _lane_block(r, width, size)
    return jnp.concatenate([jnp.where(blk == i, x, 0.0) for i in range(width // size)], axis=0)


def _run_streams(*gens):
    live = list(gens)
    while live:
        for g in list(live):
            try:
                next(g)
            except StopIteration:
                live.remove(g)


def _tile(n, pref):
    t = min(n, pref)
    while n % t:
        t //= 2
    return t


def _in_proj_kernel(sections, bw, kw, vw, x_ref, g_ref, w_ref, lb_ref, pa_ref, pb_ref, pc_ref):
    xn = _rms(x_ref[...], g_ref[...]).astype(BF16)

    def proj(start, width):
        return jnp.dot(xn, w_ref[:, start:start + width], preferred_element_type=F32)

    o_b = sections[0]
    o_c = o_b + sections[1]
    pa_ref[...] = proj(0, sections[0])
    pb_ref[:, :bw] = proj(o_b, bw)
    pb_ref[:, bw:] = jax.nn.gelu(proj(o_b + bw, bw))
    lb = lb_ref[...]
    pc_ref[:, :kw] = _silu(proj(o_c, kw))
    pc_ref[:, kw:2 * kw] = lb + (1.0 - lb) * _sigmoid(proj(o_c + kw, kw))
    pc_ref[:, 2 * kw:2 * kw + vw] = proj(o_c + 2 * kw, vw)
    pc_ref[:, 2 * kw + vw:] = _silu(proj(o_c + 2 * kw + vw, vw))


def _in_proj(x, g, w, lb, sections, bw, kw, vw, l):
    m, d = x.shape
    tm = _tile(m, IN_PROJ_ROWS)

    def resident(a, cols):
        return pl.BlockSpec((None, a.shape[1], cols), lambda i: (l, 0, 0),
                            pipeline_mode=pl.Buffered(1))

    return pl.pallas_call(
        functools.partial(_in_proj_kernel, sections, bw, kw, vw),
        grid=(m // tm,),
        in_specs=[pl.BlockSpec((tm, d), lambda i: (i, 0)), resident(g, d),
                  resident(w, sum(sections)), resident(lb, kw)],
        out_specs=[pl.BlockSpec((tm, n), lambda i: (i, 0)) for n in sections],
        out_shape=[jax.ShapeDtypeStruct((m, n), F32) for n in sections],
        compiler_params=_cparams("parallel"),
        name="in_proj",
    )(x, g, w, lb)


def _ffn_kernel(cw, x_ref, gpre_ref, wg_ref, wu_ref, wd_ref, gpost_ref, o_ref, h_ref):
    ff = wg_ref.shape[-1]
    xn = _rms(x_ref[...], gpre_ref[...]).astype(BF16)

    def gate_up(i):
        cs = slice(i * cw, (i + 1) * cw)
        return (jnp.dot(xn, wg_ref[:, cs], preferred_element_type=F32),
                jnp.dot(xn, wu_ref[:, cs], preferred_element_type=F32))

    nchunk = ff // cw
    cur = gate_up(0)
    for i in range(nchunk):
        nxt = gate_up(i + 1) if i + 1 < nchunk else None
        h_ref[:, i * cw:(i + 1) * cw] = (_silu(cur[0]) * cur[1]).astype(BF16)
        cur = nxt
    y = jnp.dot(h_ref[...], wd_ref[...], preferred_element_type=F32)
    o_ref[...] = x_ref[...] + 0.5 * _rms(y, gpost_ref[...])


def _ffn(x, gpre, wg, wu, wd, gpost, l):
    m, d = x.shape
    ff = wg.shape[-1]
    tm = _tile(m, FFN_ROWS)
    cw = FFN_CHUNK if ff % FFN_CHUNK == 0 else ff

    def resident(a):
        return pl.BlockSpec((None,) + a.shape[1:], lambda i: (l,) + (0,) * (a.ndim - 1),
                            pipeline_mode=pl.Buffered(1))

    return pl.pallas_call(
        functools.partial(_ffn_kernel, cw),
        grid=(m // tm,),
        in_specs=[pl.BlockSpec((tm, d), lambda i: (i, 0)), resident(gpre), resident(wg),
                  resident(wu), resident(wd), resident(gpost)],
        out_specs=pl.BlockSpec((tm, d), lambda i: (i, 0)),
        out_shape=jax.ShapeDtypeStruct((m, d), F32),
        scratch_shapes=[pltpu.VMEM((tm, ff), BF16)],
        compiler_params=_cparams("parallel"),
        name="ffn",
    )(x, gpre, wg, wu, wd, gpost)


def _merge_kernel(goff, x_ref, gpre_ref, win_ref, oa_ref, ob_ref, oc_ref, pa_ref, pb_ref, pc_ref,
                  wo_ref, g_ref, o_ref):
    d = x_ref.shape[-1]
    x = x_ref[...]
    xn = _rms(x, gpre_ref[...]).astype(BF16)
    merged = None
    for i, (o_b, p_b) in enumerate(((oa_ref, pa_ref), (ob_ref, pb_ref), (oc_ref, pc_ref))):
        gate = jnp.dot(xn, win_ref[:, goff + i * d:goff + (i + 1) * d], preferred_element_type=F32)
        term = _sigmoid(gate) * _mm(o_b[...], p_b[...])
        merged = term if merged is None else merged + term
    y = _mm(merged, wo_ref[...])
    o_ref[...] = x + _rms(y, g_ref[...])


def _merge(x, gpre, w_in, goff, oa, ob, oc, pa, pb, pc, wo, g, l):
    m, d = x.shape
    tm = _tile(m, MERGE_ROWS)

    def rows(w):
        return pl.BlockSpec((tm, w), lambda i: (i, 0))

    def whole(a):
        return pl.BlockSpec((None,) + a.shape[1:], lambda i: (l,) + (0,) * (a.ndim - 1),
                            pipeline_mode=pl.Buffered(1))

    return pl.pallas_call(
        functools.partial(_merge_kernel, goff),
        grid=(m // tm,),
        in_specs=[rows(d), whole(gpre), whole(w_in), rows(oa.shape[1]), rows(ob.shape[1]),
                  rows(oc.shape[1]), whole(pa), whole(pb), whole(pc), whole(wo), whole(g)],
        out_specs=rows(d),
        out_shape=jax.ShapeDtypeStruct((m, d), F32),
        compiler_params=_cparams("parallel"),
        name="merge_out",
    )(x, gpre, w_in, oa, ob, oc, pa, pb, pc, wo, g)


def _rwkv_prep(heads, hd, dw, da, nb, clen, p_ref, r0, prm, prev_ref, pf_ref, ops):
    (mu_ref, w0_ref, w2_ref, a0_ref, a2_ref, g2_ref, kk_ref, ka_ref, rk_ref) = prm
    l2_ref, bt_ref, kt_ref, h2_ref, v_ref, bv_ref, g_ref, egl_ref = ops
    aw = heads * hd
    seq = [slice(j * clen, (j + 1) * clen) for j in range(nb)]

    p = jnp.concatenate([p_ref[j, r0:r0 + clen, :] for j in range(nb)], axis=0)
    pf_ref[...] = pltpu.roll(p, 1, 0)
    for j in range(nb):
        pf_ref[j * clen:j * clen + 1, :] = prev_ref[j:j + 1, :]
        prev_ref[j:j + 1, :] = p_ref[j, r0 + clen - 1:r0 + clen, :]
    xs = p + mu_ref[...] * (pf_ref[...] - p)
    yield
    r = xs[:, :aw]
    k = xs[:, aw:2 * aw]
    v = xs[:, 2 * aw:3 * aw]
    w1 = xs[:, 3 * aw:3 * aw + dw]
    a1 = xs[:, 3 * aw + dw:3 * aw + dw + da]
    g1 = xs[:, 3 * aw + dw + da:]
    w = -_softplus(-(w0_ref[...] + _mm(jnp.tanh(w1), w2_ref[...]))) - 0.5
    logw = -jnp.exp(w)
    yield
    a_icl = _sigmoid(a0_ref[...] + _mm(a1, a2_ref[...]))
    gate = _mm(_sigmoid(g1), g2_ref[...])
    for j in range(nb):
        g_ref[j] = gate[seq[j]]
    yield
    kkraw = k * kk_ref[...]
    k2 = k * (1.0 + (a_icl - 1.0) * ka_ref[...])
    rkk = r * k2 * rk_ref[...]
    ginc = _cumsum_rows(logw, clen)
    yield
    eg = jnp.exp(ginc)
    egx = jnp.exp(ginc - logw)
    einv = jnp.exp(-ginc)
    yield
    scale = lax.rsqrt(jnp.maximum(_head_sums(kkraw * kkraw, hd), KK_EPS * KK_EPS))
    yield
    bonus = _head_sums(rkk, hd)
    kk = kkraw * scale
    bvec = kk * a_icl
    yield
    for j in range(nb):
        l2_ref[j, :clen, :] = -kk[seq[j]] * egx[seq[j]]
        l2_ref[j, clen:, :] = r[seq[j]] * eg[seq[j]]
    yield
    for j in range(nb):
        bt_ref[j] = bvec[seq[j]] * einv[seq[j]]
        kt_ref[j] = k2[seq[j]] * einv[seq[j]]
    yield
    for j in range(nb):
        glast = ginc[(j + 1) * clen - 1:(j + 1) * clen, :]
        ehat = jnp.exp(glast - ginc[seq[j]])
        h2_ref[j, :clen, :] = bvec[seq[j]] * ehat
        h2_ref[j, clen:, :] = k2[seq[j]] * ehat
        egl_ref[j] = jnp.exp(glast)
    yield
    for j in range(nb):
        v_ref[j] = v[seq[j]]
        bv_ref[j] = bonus[seq[j]] * v[seq[j]]
    yield


def _rwkv_chain(heads, hd, nb, clen, ops, st_ref, lng_ref, lnb_ref, o_ref, o_r0):
    l2_ref, bt_ref, kt_ref, h2_ref, v_ref, bv_ref, g_ref, egl_ref = ops
    per = LANES // hd
    cw = per * clen
    units = [(j, g) for j in range(nb) for g in range(heads // per)]

    def ld(ref):
        return [ref[j, :, g * LANES:(g + 1) * LANES] for j, g in units]

    ri = lax.broadcasted_iota(jnp.int32, (2 * clen, cw), 0)
    cpos = lax.broadcasted_iota(jnp.int32, (2 * clen, cw), 1) & (clen - 1)
    mask2 = cpos < jnp.where(ri < clen, ri, ri - (clen - 1))
    eye = ((lax.broadcasted_iota(jnp.int32, (clen, cw), 1) & (clen - 1))
           == lax.broadcasted_iota(jnp.int32, (clen, cw), 0)).astype(F32)
    blk = _lane_block(hd, LANES, hd)

    lhs2 = ld(l2_ref)
    ab2 = [jnp.where(mask2, _mm_nt(x, _blockdiag(y, hd)), 0.0) for x, y in zip(lhs2, ld(bt_ref))]
    yield
    ak2 = [jnp.where(mask2, _mm_nt(x, _blockdiag(y, hd)), 0.0) for x, y in zip(lhs2, ld(kt_ref))]
    yield
    s0 = [st_ref[j, g] for j, g in units]
    vs = ld(v_ref)
    xy = [_mm_nt(x, _blockdiag(s, hd)) + _mm(a, _blockdiag(vv, hd))
          for x, s, a, vv in zip(lhs2, s0, ak2, vs)]
    yield
    pw = [m[:clen] for m in ab2]
    tinv = [eye + m for m in pw]
    span = 2
    while span < clen:
        pwd = [_blockdiag(m, clen) for m in pw]
        pw = [_mm(m, md) for m, md in zip(pw, pwd)]
        yield
        pwd = [_blockdiag(m, clen) for m in pw]
        tinv = [t + _mm(t, md) for t, md in zip(tinv, pwd)]
        yield
        span *= 2
    u = [_mm(t, _blockdiag(m[:clen], hd)) for t, m in zip(tinv, xy)]
    yield
    y = [m[clen:] + _mm(a[clen:], _blockdiag(uu, hd)) for m, a, uu in zip(xy, ab2, u)]
    yield
    full = [_mm_tn(jnp.concatenate([uu, vv], axis=0), hh) for uu, vv, hh in zip(u, vs, ld(h2_ref))]
    for (j, g), s, f in zip(units, s0, full):
        own = f[(per - 1) * hd:, :]
        for i in reversed(range(per - 1)):
            own = jnp.where(blk == i, f[i * hd:(i + 1) * hd, :], own)
        st_ref[j, g] = s * egl_ref[j, :, g * LANES:(g + 1) * LANES] + own
    yield
    bvs = ld(bv_ref)
    gates = ld(g_ref)
    for i, (j, g) in enumerate(units):
        sl = slice(g * LANES, (g + 1) * LANES)
        yc = y[i] - _head_sums(y[i], hd) * (1.0 / hd)
        var = _head_sums(yc * yc, hd) * (1.0 / hd)
        yy = yc * lax.rsqrt(var + GN_EPS) * lng_ref[:, sl] + lnb_ref[:, sl]
        o_ref[j, o_r0:o_r0 + clen, sl] = (yy + bvs[i]) * gates[i]
        if i % (heads // per) == heads // per - 1:
            yield


def _rwkv_step_kernel(heads, hd, dw, da, nb, clen, nsub, first, *refs):
    refs = list(refs)
    p_ref = refs.pop(0)
    pn_ref = refs.pop(0) if nsub == 2 else None
    shift_ref, s0_ref = refs[:2]
    prm = refs[2:11]
    lng_ref, lnb_ref = refs[11:13]
    rest = refs[13:] if first else refs[14:]
    o_ref, nshift_ref, ns_ref, st_ref, prev_ref, pf_ref = rest[:6]
    ops_a = rest[6:14]
    ops_b = rest[14:22]
    c = pl.program_id(1)
    per = LANES // hd
    groups = [(j, g) for j in range(nb) for g in range(heads // per)]

    @pl.when(c == 0)
    def _():
        prev_ref[...] = shift_ref[...]
        for j, g in groups:
            st_ref[j, g] = jnp.concatenate([s0_ref[j, g * per + i] for i in range(per)], axis=1)

    def prep(src, r0, ops):
        return _rwkv_prep(heads, hd, dw, da, nb, clen, src, r0, prm, prev_ref, pf_ref, ops)

    def chain(ops, o_r0):
        return _rwkv_chain(heads, hd, nb, clen, ops, st_ref, lng_ref, lnb_ref, o_ref, o_r0)

    if nsub == 1:
        _run_streams(prep(p_ref, 0, ops_a))
        _run_streams(chain(ops_a, 0))
    else:
        @pl.when(c == 0)
        def _():
            _run_streams(prep(p_ref, 0, ops_a))

        _run_streams(chain(ops_a, 0), prep(p_ref, clen, ops_b))
        _run_streams(chain(ops_b, clen), prep(pn_ref, 0, ops_a))

    @pl.when(c == pl.num_programs(1) - 1)
    def _():
        nshift_ref[...] = prev_ref[...]
        if first:
            ns_ref[1:] = jnp.zeros((ns_ref.shape[0] - 1,) + ns_ref.shape[1:], F32)
        for j, g in groups:
            for i in range(per):
                own = st_ref[j, g, :, i * hd:(i + 1) * hd]
                if first:
                    ns_ref[0, j, g * per + i] = own
                else:
                    ns_ref[j, g * per + i] = own


def _rwkv_mix(pa, shift, s0, stacked, P, l, nseq, tlen, chunk, nb):
    ap = pa.shape[1]
    depth, _, heads, hd, _ = s0.shape
    aw = heads * hd
    dw = P['rwkv_w2'].shape[1]
    da = P['rwkv_a2'].shape[1]
    nch = tlen // chunk
    nsub = 2 if nch % 2 == 0 else 1
    assert nsub == 2 or nch == 1
    nstep = nch // nsub
    ngrp = nseq // nb
    shift = shift.reshape(depth, ngrp, nb, ap)
    pa = pa.reshape(nseq, tlen, ap)

    def vec(a):
        return pl.BlockSpec((None,) + a.shape[1:], lambda b, c: (l,) + (0,) * (a.ndim - 1))

    names = ['rwkv_mu', 'rwkv_w0', 'rwkv_w2', 'rwkv_a0', 'rwkv_a2', 'rwkv_g2', 'rwkv_k_k',
             'rwkv_k_a', 'rwkv_r_k', 'rwkv_ln_g', 'rwkv_ln_b']
    params = [P[n] for n in names]
    args = [pa]
    in_specs = [pl.BlockSpec((nb, nsub * chunk, ap), lambda b, c: (b, c, 0))]
    if nsub == 2:
        args.append(pa)
        in_specs.append(pl.BlockSpec(
            (nb, chunk, ap), lambda b, c: (b, jnp.minimum(2 * c + 2, nch - 1), 0)))
    args += [shift, s0] + params
    in_specs += [
        pl.BlockSpec((None, None, nb, ap), lambda b, c: (l, b, 0, 0)),
        pl.BlockSpec((None, nb, heads, hd, hd), lambda b, c: (l, b, 0, 0, 0)),
    ] + [vec(a) for a in params]
    first = stacked is None
    if first:
        aliases = {}
        state_spec = pl.BlockSpec((depth, nb, heads, hd, hd), lambda b, c: (0, b, 0, 0, 0))
    else:
        aliases = {len(args): 2}
        args.append(stacked)
        in_specs.append(pl.BlockSpec(memory_space=pl.ANY))
        state_spec = pl.BlockSpec((None, nb, heads, hd, hd), lambda b, c: (l, b, 0, 0, 0))

    def operand_set():
        f = lambda r: pltpu.VMEM((nb, r, aw), F32)
        return [f(2 * chunk), f(chunk), f(chunk), f(2 * chunk), f(chunk), f(chunk), f(chunk), f(1)]

    o, n_shift, stacked = pl.pallas_call(
        functools.partial(_rwkv_step_kernel, heads, hd, dw, da, nb, chunk, nsub, first),
        grid=(ngrp, nstep),
        in_specs=in_specs,
        out_specs=[
            pl.BlockSpec((nb, nsub * chunk, aw), lambda b, c: (b, c, 0)),
            pl.BlockSpec((None, nb, ap), lambda b, c: (b, 0, 0)),
            state_spec,
        ],
        out_shape=[
            jax.ShapeDtypeStruct((nseq, tlen, aw), F32),
            jax.ShapeDtypeStruct((ngrp, nb, ap), F32),
            jax.ShapeDtypeStruct((depth, nseq, heads, hd, hd), F32),
        ],
        scratch_shapes=[pltpu.VMEM((nb, aw // LANES, hd, LANES), F32), pltpu.VMEM((nb, ap), F32),
                        pltpu.VMEM((nb * chunk, ap), F32)] + operand_set() + operand_set(),
        input_output_aliases=aliases,
        compiler_params=_cparams("parallel", "arbitrary"),
        name="rwkv7",
    )(*args)
    return o.reshape(nseq * tlen, aw), n_shift.reshape(nseq, ap), stacked


def _lru_kernel(bw, seqs, p_ref, cs_ref, h0_ref, cw_ref, cb_ref, wa_ref, ba_ref, wx_ref, bx_ref,
                lam_ref, o_ref, nconv_ref, nh_ref, tail_ref, h_ref):
    c = pl.program_id(1)
    rows = p_ref.shape[0]
    ncv = cs_ref.shape[1]
    nseg = rows // SUBLANES

    if seqs == 1:
        @pl.when(c == 0)
        def _():
            tail_ref[...] = jnp.zeros_like(tail_ref)
            tail_ref[SUBLANES - ncv:, :] = cs_ref[0]
            h_ref[...] = h0_ref[0]
    else:
        tail_ref[...] = jnp.zeros_like(tail_ref)
        for j in range(seqs):
            tail_ref[(j + 1) * SUBLANES - ncv:(j + 1) * SUBLANES, :] = cs_ref[j]

    xb = p_ref[:, :bw]
    gb = p_ref[:, bw:]
    rowi = lax.broadcasted_iota(jnp.int32, (rows, 1), 0)
    pos = rowi & (SUBLANES - 1)

    def seg_roll(x, d):
        return pltpu.roll(x.reshape(nseg, SUBLANES, bw), d, 1).reshape(rows, bw)

    if seqs == 1 and rows > SUBLANES:
        before = jnp.concatenate([tail_ref[...], xb[:rows - SUBLANES, :]], axis=0)
    else:
        before = tail_ref[...]
    xc = cb_ref[...] + xb * cw_ref[ncv:ncv + 1, :]
    for d in range(1, ncv + 1):
        sh = jnp.where(pos >= d, seg_roll(xb, d), seg_roll(before, d))
        xc = xc + sh * cw_ref[ncv - d:ncv - d + 1, :]
    if seqs == 1:
        tail_ref[...] = xb[rows - SUBLANES:, :]

    rg = _sigmoid(_mm(xc, wa_ref[...]) + ba_ref[...])
    ig = _sigmoid(_mm(xc, wx_ref[...]) + bx_ref[...])
    log_a = -LRU_C * rg * _softplus(-lam_ref[...])
    a = jnp.exp(log_a)
    b = jnp.sqrt(-jnp.tanh(log_a) * (1.0 + a * a)) * ig * xc

    d = 1
    while d < SUBLANES:
        keep = pos >= d
        a_sh = jnp.where(keep, seg_roll(a, d), 1.0)
        b_sh = jnp.where(keep, seg_roll(b, d), 0.0)
        b = a * b_sh + b
        a = a_sh * a
        d *= 2
    pieces = []
    hc = h_ref[...] if seqs == 1 else None
    for s in range(nseg):
        rs = slice(s * SUBLANES, (s + 1) * SUBLANES)
        h_s = a[rs] * (hc if seqs == 1 else h0_ref[s]) + b[rs]
        hc = h_s[SUBLANES - 1:, :]
        pieces.append(h_s)
    h = pieces[0] if nseg == 1 else jnp.concatenate(pieces, axis=0)
    o_ref[...] = h * gb

    if seqs == 1:
        h_ref[...] = hc

        @pl.when(c == pl.num_programs(1) - 1)
        def _():
            nconv_ref[0] = xb[rows - ncv:, :]
            nh_ref[0] = hc
    else:
        for s in range(seqs):
            nconv_ref[s] = xb[(s + 1) * SUBLANES - ncv:(s + 1) * SUBLANES, :]
            nh_ref[s] = pieces[s][SUBLANES - 1:, :]


def _lru(pb, conv, h0, P, l, nseq, tlen, tile, seqs):
    bw = h0.shape[-1]
    ncv = conv.shape[2]
    nt = tlen // tile
    assert seqs == 1 or (nt == 1 and tile == SUBLANES)
    rows = seqs * tile

    def vec(a):
        return pl.BlockSpec((None,) + a.shape[1:], lambda b, c: (l,) + (0,) * (a.ndim - 1))

    names = ['lru_conv_w', 'lru_conv_b', 'lru_wa_bd', 'lru_ba', 'lru_wx_bd', 'lru_bx', 'lru_lam']
    params = [P[n] for n in names]
    return pl.pallas_call(
        functools.partial(_lru_kernel, bw, seqs),
        grid=(nseq // seqs, nt),
        in_specs=[
            pl.BlockSpec((rows, 2 * bw), lambda b, c: (b * nt + c, 0)),
            pl.BlockSpec((None, seqs, ncv, bw), lambda b, c: (l, b, 0, 0)),
            pl.BlockSpec((None, seqs, 1, bw), lambda b, c: (l, b, 0, 0)),
        ] + [vec(a) for a in params],
        out_specs=[
            pl.BlockSpec((rows, bw), lambda b, c: (b * nt + c, 0)),
            pl.BlockSpec((seqs, ncv, bw), lambda b, c: (b, 0, 0)),
            pl.BlockSpec((seqs, 1, bw), lambda b, c: (b, 0, 0)),
        ],
        out_shape=[
            jax.ShapeDtypeStruct((nseq * tlen, bw), F32),
            jax.ShapeDtypeStruct((nseq, ncv, bw), F32),
            jax.ShapeDtypeStruct((nseq, 1, bw), F32),
        ],
        scratch_shapes=[pltpu.VMEM((SUBLANES if seqs == 1 else rows, bw), F32),
                        pltpu.VMEM((1, bw), F32)],
        compiler_params=_cparams("parallel", "arbitrary"),
        name="rglru",
    )(pb, conv, h0, *params)


def _hgrn_kernel(heads, kd, vd, sub, nb, clen, first, p_ref, s0_ref, ng_ref, *rest):
    o_ref, ns_ref, st_ref = rest[-3:]
    c = pl.program_id(1)
    kw = heads * kd
    vw = heads * vd
    pairs = [(j, h) for j in range(nb) for h in range(heads)]

    @pl.when(c == 0)
    def _():
        for j, h in pairs:
            st_ref[j, h] = s0_ref[j, h].T

    p = p_ref[0] if nb == 1 else jnp.concatenate([p_ref[j] for j in range(nb)], axis=0)
    q = p[:, :kw]
    f = p[:, kw:2 * kw]
    logf = jnp.log(f)
    kf = 1.0 - f
    v = p[:, 2 * kw:2 * kw + vw]
    gzs = p[:, 2 * kw + vw:]

    g = _cumsum_rows(logf, clen)
    qg = q * jnp.exp(g)
    glast = [g[(j + 1) * clen - 1:(j + 1) * clen, :] for j in range(nb)]
    khat = [kf[j * clen:(j + 1) * clen, :] * jnp.exp(glast[j] - g[j * clen:(j + 1) * clen, :])
            for j in range(nb)]
    eglast = [jnp.exp(x) for x in glast]

    rows = nb * clen
    nblk = clen // sub
    rowi = lax.broadcasted_iota(jnp.int32, (rows, 1), 0)

    def usl(x, j, h, width):
        return x[j * clen:(j + 1) * clen, h * width:(h + 1) * width]

    st = [st_ref[j, h] for j, h in pairs]
    vs = [usl(v, j, h, vd) for j, h in pairs]
    o = [_mm_nt(usl(qg, j, h, kd), s) for (j, h), s in zip(pairs, st)]
    st_new = [s * eglast[j][:, h * kd:(h + 1) * kd] + _mm_tn(vv, khat[j][:, h * kd:(h + 1) * kd])
              for (j, h), s, vv in zip(pairs, st, vs)]
    for (j, h), s in zip(pairs, st_new):
        st_ref[j, h] = s

    ri = lax.broadcasted_iota(jnp.int32, (clen, clen), 0)
    ci = lax.broadcasted_iota(jnp.int32, (clen, clen), 1)
    att = None
    hs = sub
    while hs < clen:
        later = (rowi & hs) != 0
        gref = jnp.concatenate(
            [jnp.broadcast_to(g[r0 + hs - 1:r0 + hs, :], (2 * hs, kw)) for r0 in range(0, rows, 2 * hs)],
            axis=0)
        q_rel = q * jnp.exp(jnp.where(later, g - gref, -jnp.inf))
        k_rel = kf * jnp.exp(jnp.where(later, -jnp.inf, gref - g))
        sh = (2 * hs).bit_length() - 1
        same = lax.shift_right_logical(ri, sh) == lax.shift_right_logical(ci, sh)
        part = [jnp.where(same, _mm_nt(usl(q_rel, j, h, kd), usl(k_rel, j, h, kd)), 0.0)
                for j, h in pairs]
        att = part if att is None else [a + b for a, b in zip(att, part)]
        hs *= 2
    if att is not None:
        o = [oo + _mm(a, vv) for oo, a, vv in zip(o, att, vs)]

    def blocks3(x, j, h):
        return usl(x, j, h, kd).reshape(nblk, sub, kd)

    f3 = [blocks3(f, j, h) for j, h in pairs]
    q3 = [blocks3(q, j, h) for j, h in pairs]
    k3 = [blocks3(kf, j, h) for j, h in pairs]
    srow = lax.broadcasted_iota(jnp.int32, (nblk, sub, 1), 1)
    tcol = (lax.broadcasted_iota(jnp.int32, (nblk, sub, LANES), 2)
            - sub * lax.broadcasted_iota(jnp.int32, (nblk, sub, LANES), 0))
    att_t = [jnp.zeros((nblk, sub, LANES), F32) for _ in pairs]
    dec = [jnp.zeros((nblk, sub, kd), F32) for _ in pairs]
    for t in range(sub):
        dec = [jnp.where(srow == t, kk, d * ff[:, t:t + 1, :]) for d, ff, kk in zip(dec, f3, k3)]
        cols = [jnp.sum(qq[:, t:t + 1, :] * d, axis=-1, keepdims=True) for d, qq in zip(dec, q3)]
        att_t = [jnp.where(tcol == t, col, a) for col, a in zip(cols, att_t)]
    o = [oo + _mm_tn(a.reshape(clen, LANES)[:, :clen], vv) for oo, a, vv in zip(o, att_t, vs)]

    for (j, h), oo in zip(pairs, o):
        hv = slice(h * vd, (h + 1) * vd)
        oo = oo * lax.rsqrt(jnp.mean(oo * oo, axis=-1, keepdims=True) + RMS_EPS) * ng_ref[:, hv]
        o_ref[j, :, hv] = oo * gzs[j * clen:(j + 1) * clen, hv]

    @pl.when(c == pl.num_programs(1) - 1)
    def _():
        if first:
            ns_ref[1:] = jnp.zeros((ns_ref.shape[0] - 1,) + ns_ref.shape[1:], F32)
        for j, h in pairs:
            if first:
                ns_ref[0, j, h] = st_ref[j, h].T
            else:
                ns_ref[j, h] = st_ref[j, h].T


def _hgrn(pc, s0, stacked, ng, l, nseq, tlen, chunk, nb):
    depth, _, heads, kd, vd = s0.shape
    width = pc.shape[1]
    nch = tlen // chunk
    sub = min(HGRN_SUB, chunk)
    args = [pc.reshape(nseq, tlen, width), s0, ng]
    in_specs = [
        pl.BlockSpec((nb, chunk, width), lambda b, c: (b, c, 0)),
        pl.BlockSpec((None, nb, heads, kd, vd), lambda b, c: (l, b, 0, 0, 0)),
        pl.BlockSpec((None, 1, heads * vd), lambda b, c: (l, 0, 0)),
    ]
    first = stacked is None
    if first:
        aliases = {}
        state_spec = pl.BlockSpec((depth, nb, heads, kd, vd), lambda b, c: (0, b, 0, 0, 0))
    else:
        aliases = {len(args): 1}
        args.append(stacked)
        in_specs.append(pl.BlockSpec(memory_space=pl.ANY))
        state_spec = pl.BlockSpec((None, nb, heads, kd, vd), lambda b, c: (l, b, 0, 0, 0))
    o, stacked = pl.pallas_call(
        functools.partial(_hgrn_kernel, heads, kd, vd, sub, nb, chunk, first),
        grid=(nseq // nb, nch),
        in_specs=in_specs,
        out_specs=[pl.BlockSpec((nb, chunk, heads * vd), lambda b, c: (b, c, 0)), state_spec],
        out_shape=[
            jax.ShapeDtypeStruct((nseq, tlen, heads * vd), F32),
            jax.ShapeDtypeStruct((depth, nseq, heads, kd, vd), F32),
        ],
        scratch_shapes=[pltpu.VMEM((nb, heads, vd, kd), F32)],
        input_output_aliases=aliases,
        compiler_params=_cparams("parallel", "arbitrary"),
        name="hgrn2",
    )(*args)
    return o.reshape(nseq * tlen, heads * vd), stacked


def _run_trunk(x, s_rwkv, s_shift, s_lru, s_conv, s_hgrn, P):
    nseq, tlen, d = x.shape
    depth = s_rwkv.shape[0]
    x = x.reshape(nseq * tlen, d)
    lru4 = s_lru[:, :, None, :]
    rw_chunk = min(RWKV_CHUNK, tlen)
    rw_nb = _tile(nseq, max(RWKV_SEQS, RWKV_CHUNK // rw_chunk))
    hg_chunk = min(HGRN_CHUNK, tlen)
    hg_nb = _tile(nseq, max(HGRN_SEQS, HGRN_CHUNK // hg_chunk))
    lru_tile = min(LRU_TILE, tlen)
    lru_seqs = _tile(nseq, LRU_SEQS) if lru_tile == tlen == SUBLANES else 1
    outs = ([], [], [])
    n_rwkv = n_hgrn = None
    for l in range(depth):
        x = _ffn(x, P['ffn1_pre_g'], P['ffn1_wg'], P['ffn1_wu'], P['ffn1_wd'], P['ffn1_post_g'], l)
        pa, pb, pc = _in_proj(x, P['mix_pre_g'], P['w_in'], P['hgrn_lb'], P['w_in_sections'],
                              s_lru.shape[-1], s_hgrn.shape[2] * s_hgrn.shape[3],
                              s_hgrn.shape[2] * s_hgrn.shape[4], l)
        oa, n_shift, n_rwkv = _rwkv_mix(pa, s_shift, s_rwkv, n_rwkv, P, l, nseq, tlen, rw_chunk,
                                        rw_nb)
        ob, n_conv, n_lru = _lru(pb, s_conv, lru4, P, l, nseq, tlen, lru_tile, lru_seqs)
        oc, n_hgrn = _hgrn(pc, s_hgrn, n_hgrn, P['hgrn_norm_g'], l, nseq, tlen, hg_chunk, hg_nb)
        x = _merge(x, P['mix_pre_g'], P['w_in'], sum(P['w_in_sections']), oa, ob, oc, P['proj_a'],
                   P['proj_b'], P['proj_c'], P['w_out'], P['mix_post_g'], l)
        x = _ffn(x, P['ffn2_pre_g'], P['ffn2_wg'], P['ffn2_wu'], P['ffn2_wd'], P['ffn2_post_g'], l)
        for lst, t in zip(outs, (n_shift, n_lru[:, 0], n_conv)):
            lst.append(t)
    n_shift, n_lru, n_conv = (jnp.stack(lst, axis=0) for lst in outs)
    return x.reshape(nseq, tlen, d), (n_rwkv, n_shift, n_lru, n_conv, n_hgrn)


def _block_diag(w):
    depth, g, i, j = w.shape
    eye = jnp.eye(g, dtype=w.dtype)
    return jnp.einsum('lgij,gh->lgihj', w, eye).reshape(depth, g * i, g * j)


def _prepare_params(raw, a_proj, b_width, c_kwidth, c_width):
    P = {}
    for n in ('ffn1_wg', 'ffn1_wu', 'ffn1_wd', 'ffn2_wg', 'ffn2_wu', 'ffn2_wd', 'proj_a', 'proj_b',
              'proj_c', 'w_out', 'w_in', 'rwkv_w2', 'rwkv_a2', 'rwkv_g2'):
        P[n] = raw[n].astype(BF16)
    P['w_in_sections'] = (a_proj, 2 * b_width, 2 * c_kwidth + 2 * c_width)
    for n in ('ffn1_pre_g', 'ffn1_post_g', 'mix_pre_g', 'mix_post_g', 'ffn2_pre_g', 'ffn2_post_g',
              'rwkv_mu', 'rwkv_w0', 'rwkv_a0', 'rwkv_k_k', 'rwkv_k_a', 'rwkv_ln_g', 'rwkv_ln_b',
              'lru_conv_b', 'lru_ba', 'lru_bx', 'lru_lam', 'hgrn_norm_g'):
        P[n] = raw[n][:, None, :]
    depth = raw['rwkv_r_k'].shape[0]
    P['rwkv_r_k'] = raw['rwkv_r_k'].reshape(depth, 1, -1)
    P['lru_conv_w'] = raw['lru_conv_w']
    P['lru_wa_bd'] = _block_diag(raw['lru_wa']).astype(BF16)
    P['lru_wx_bd'] = _block_diag(raw['lru_wx']).astype(BF16)
    lb_cum = jnp.cumsum(jax.nn.softmax(raw['hgrn_lb_logits'].astype(F32), axis=0), axis=0)
    P['hgrn_lb'] = (lb_cum - lb_cum[0])[:, None, :]
    return P


def kernel(x_prompt, x_sample, state_rwkv, state_shift, state_lru, state_conv, state_hgrn, ffn1_pre_g, ffn1_post_g, ffn1_wg, ffn1_wu, ffn1_wd, mix_pre_g, mix_post_g, w_in, rwkv_mu, rwkv_w0, rwkv_w2, rwkv_a0, rwkv_a2, rwkv_g2, rwkv_k_k, rwkv_k_a, rwkv_r_k, rwkv_ln_g, rwkv_ln_b, lru_conv_w, lru_conv_b, lru_wa, lru_ba, lru_wx, lru_bx, lru_lam, hgrn_lb_logits, hgrn_norm_g, proj_a, proj_b, proj_c, w_out, ffn2_pre_g, ffn2_post_g, ffn2_wg, ffn2_wu, ffn2_wd):
    raw = dict(
        ffn1_pre_g=ffn1_pre_g, ffn1_post_g=ffn1_post_g, ffn1_wg=ffn1_wg, ffn1_wu=ffn1_wu,
        ffn1_wd=ffn1_wd, mix_pre_g=mix_pre_g, mix_post_g=mix_post_g, w_in=w_in, rwkv_mu=rwkv_mu,
        rwkv_w0=rwkv_w0, rwkv_w2=rwkv_w2, rwkv_a0=rwkv_a0, rwkv_a2=rwkv_a2, rwkv_g2=rwkv_g2,
        rwkv_k_k=rwkv_k_k, rwkv_k_a=rwkv_k_a, rwkv_r_k=rwkv_r_k, rwkv_ln_g=rwkv_ln_g,
        rwkv_ln_b=rwkv_ln_b, lru_conv_w=lru_conv_w, lru_conv_b=lru_conv_b, lru_wa=lru_wa,
        lru_ba=lru_ba, lru_wx=lru_wx, lru_bx=lru_bx, lru_lam=lru_lam,
        hgrn_lb_logits=hgrn_lb_logits, hgrn_norm_g=hgrn_norm_g, proj_a=proj_a, proj_b=proj_b,
        proj_c=proj_c, w_out=w_out, ffn2_pre_g=ffn2_pre_g, ffn2_post_g=ffn2_post_g,
        ffn2_wg=ffn2_wg, ffn2_wu=ffn2_wu, ffn2_wd=ffn2_wd)
    depth, _, heads, hd, _ = state_rwkv.shape
    a_proj = state_shift.shape[-1]
    b_width = state_lru.shape[-1]
    _, _, c_heads, c_kd, c_vd = state_hgrn.shape
    P = _prepare_params(raw, a_proj, b_width, c_heads * c_kd, c_heads * c_vd)
    dt = state_rwkv.dtype
    bp = x_prompt.shape[0]
    y_prompt, p_st = _run_trunk(
        x_prompt,
        jnp.zeros((depth, bp, heads, hd, hd), dt),
        jnp.zeros((depth, bp, a_proj), dt),
        jnp.zeros((depth, bp, b_width), dt),
        jnp.zeros((depth, bp) + state_conv.shape[2:], dt),
        jnp.zeros((depth, bp, c_heads, c_kd, c_vd), dt),
        P)
    y_sample, s_st = _run_trunk(x_sample, state_rwkv, state_shift, state_lru, state_conv,
                                state_hgrn, P)
    return (y_prompt, y_sample) + tuple(t.astype(dt) for t in p_st) + tuple(t.astype(dt) for t in s_st)
```

```python
import functools

import jax
import jax.numpy as jnp
from jax import lax
from jax.experimental import pallas as pl
from jax.experimental.pallas import tpu as pltpu

F32 = jnp.float32
BF16 = jnp.bfloat16
RMS_EPS = 1e-6
GN_EPS = 64e-5
LRU_C = 8.0
KK_EPS = 1e-12

VMEM_LIMIT_BYTES = 48 * 1024 * 1024
SUBLANES = 8
LANES = 128
HGRN_SUB = 8
RWKV_CHUNK = 64
HGRN_CHUNK = 64
RWKV_SEQS = 4
HGRN_SEQS = 4
LRU_TILE = 512
LRU_SEQS = 16
IN_PROJ_ROWS = 512
MERGE_ROWS = 512
FFN_ROWS = 1024
FFN_CHUNK = 256


def _cparams(*sem):
    return pltpu.CompilerParams(dimension_semantics=sem, vmem_limit_bytes=VMEM_LIMIT_BYTES)


def _mm(a, b):
    return jnp.dot(a.astype(BF16), b.astype(BF16), preferred_element_type=F32)


def _mm_nt(a, b):
    return lax.dot_general(a.astype(BF16), b.astype(BF16), (((1,), (1,)), ((), ())),
                           preferred_element_type=F32)


def _mm_tn(a, b):
    return lax.dot_general(a.astype(BF16), b.astype(BF16), (((0,), (0,)), ((), ())),
                           preferred_element_type=F32)


def _split3(x):
    hi = x.astype(BF16)
    r1 = x - hi.astype(F32)
    mid = r1.astype(BF16)
    lo = (r1 - mid.astype(F32)).astype(BF16)
    return hi, mid, lo


def _cumsum_rows(x, seg=None):
    c = x.shape[0]
    row = lax.broadcasted_iota(jnp.int32, (c, c), 0)
    col = lax.broadcasted_iota(jnp.int32, (c, c), 1)
    tri = col <= row
    if seg is not None and seg < c:
        sh = seg.bit_length() - 1
        tri = tri & (lax.shift_right_logical(row, sh) == lax.shift_right_logical(col, sh))
    tri = tri.astype(BF16)
    hi, mid, lo = _split3(x)
    out = jnp.dot(tri, lo, preferred_element_type=F32)
    out = out + jnp.dot(tri, mid, preferred_element_type=F32)
    return out + jnp.dot(tri, hi, preferred_element_type=F32)


def _rms(x, g):
    return x * lax.rsqrt(jnp.mean(x * x, axis=-1, keepdims=True) + RMS_EPS) * g


def _softplus(x):
    return jnp.maximum(x, 0.0) + jnp.log1p(jnp.exp(-jnp.abs(x)))


def _sigmoid(x):
    return 0.5 * jnp.tanh(0.5 * x) + 0.5


def _silu(x):
    return x * _sigmoid(x)


def _lane_block(rows, width, size):
    lane = lax.broadcasted_iota(jnp.int32, (rows, width), 1)
    return lax.shift_right_logical(lane, size.bit_length() - 1)


def _head_sums(x, hd):
    rows, width = x.shape
    per = LANES // hd
    blk = _lane_block(rows, LANES, hd)
    out = []
    for g in range(width // LANES):
        xg = x[:, g * LANES:(g + 1) * LANES]
        acc = None
        for i in reversed(range(per)):
            col = jnp.sum(jnp.where(blk == i, xg, 0.0), axis=-1, keepdims=True)
            acc = jnp.broadcast_to(col, (rows, LANES)) if acc is None else jnp.where(blk == i, col, acc)
        out.append(acc)
    return out[0] if len(out) == 1 else jnp.concatenate(out, axis=1)


def _blockdiag(x, size):
    r, width = x.shape
    blk = _lane_block(r, width, size)
    return jnp.concatenate([jnp.where(blk == i, x, 0.0) for i in range(width // size)], axis=0)


def _run_streams(*gens):
    live = list(gens)
    while live:
        for g in list(live):
            try:
                next(g)
            except StopIteration:
                live.remove(g)


def _tile(n, pref):
    t = min(n, pref)
    while n % t:
        t //= 2
    return t


def _in_proj_kernel(sections, bw, kw, vw, x_ref, g_ref, w_ref, lb_ref, pa_ref, pb_ref, pc_ref):
    xn = _rms(x_ref[...], g_ref[...]).astype(BF16)

    def proj(start, width):
        return jnp.dot(xn, w_ref[:, start:start + width], preferred_element_type=F32)

    o_b = sections[0]
    o_c = o_b + sections[1]
    pa_ref[...] = proj(0, sections[0])
    pb_ref[:, :bw] = proj(o_b, bw)
    pb_ref[:, bw:] = jax.nn.gelu(proj(o_b + bw, bw))
    lb = lb_ref[...]
    pc_ref[:, :kw] = _silu(proj(o_c, kw))
    pc_ref[:, kw:2 * kw] = lb + (1.0 - lb) * _sigmoid(proj(o_c + kw, kw))
    pc_ref[:, 2 * kw:2 * kw + vw] = proj(o_c + 2 * kw, vw)
    pc_ref[:, 2 * kw + vw:] = _silu(proj(o_c + 2 * kw + vw, vw))


def _in_proj(x, g, w, lb, sections, bw, kw, vw, l):
    m, d = x.shape
    tm = _tile(m, IN_PROJ_ROWS)

    def resident(a, cols):
        return pl.BlockSpec((None, a.shape[1], cols), lambda i: (l, 0, 0),
                            pipeline_mode=pl.Buffered(1))

    return pl.pallas_call(
        functools.partial(_in_proj_kernel, sections, bw, kw, vw),
        grid=(m // tm,),
        in_specs=[pl.BlockSpec((tm, d), lambda i: (i, 0)), resident(g, d),
                  resident(w, sum(sections)), resident(lb, kw)],
        out_specs=[pl.BlockSpec((tm, n), lambda i: (i, 0)) for n in sections],
        out_shape=[jax.ShapeDtypeStruct((m, n), F32) for n in sections],
        compiler_params=_cparams("parallel"),
        name="in_proj",
    )(x, g, w, lb)


def _ffn_kernel(cw, x_ref, gpre_ref, wg_ref, wu_ref, wd_ref, gpost_ref, o_ref, h_ref):
    ff = wg_ref.shape[-1]
    xn = _rms(x_ref[...], gpre_ref[...]).astype(BF16)

    def gate_up(i):
        cs = slice(i * cw, (i + 1) * cw)
        return (jnp.dot(xn, wg_ref[:, cs], preferred_element_type=F32),
                jnp.dot(xn, wu_ref[:, cs], preferred_element_type=F32))

    nchunk = ff // cw
    cur = gate_up(0)
    for i in range(nchunk):
        nxt = gate_up(i + 1) if i + 1 < nchunk else None
        h_ref[:, i * cw:(i + 1) * cw] = (_silu(cur[0]) * cur[1]).astype(BF16)
        cur = nxt
    y = jnp.dot(h_ref[...], wd_ref[...], preferred_element_type=F32)
    o_ref[...] = x_ref[...] + 0.5 * _rms(y, gpost_ref[...])


def _ffn(x, gpre, wg, wu, wd, gpost, l):
    m, d = x.shape
    ff = wg.shape[-1]
    tm = _tile(m, FFN_ROWS)
    cw = FFN_CHUNK if ff % FFN_CHUNK == 0 else ff

    def resident(a):
        return pl.BlockSpec((None,) + a.shape[1:], lambda i: (l,) + (0,) * (a.ndim - 1),
                            pipeline_mode=pl.Buffered(1))

    return pl.pallas_call(
        functools.partial(_ffn_kernel, cw),
        grid=(m // tm,),
        in_specs=[pl.BlockSpec((tm, d), lambda i: (i, 0)), resident(gpre), resident(wg),
                  resident(wu), resident(wd), resident(gpost)],
        out_specs=pl.BlockSpec((tm, d), lambda i: (i, 0)),
        out_shape=jax.ShapeDtypeStruct((m, d), F32),
        scratch_shapes=[pltpu.VMEM((tm, ff), BF16)],
        compiler_params=_cparams("parallel"),
        name="ffn",
    )(x, gpre, wg, wu, wd, gpost)


def _merge_kernel(goff, x_ref, gpre_ref, win_ref, oa_ref, ob_ref, oc_ref, pa_ref, pb_ref, pc_ref,
                  wo_ref, g_ref, o_ref):
    d = x_ref.shape[-1]
    x = x_ref[...]
    xn = _rms(x, gpre_ref[...]).astype(BF16)
    merged = None
    for i, (o_b, p_b) in enumerate(((oa_ref, pa_ref), (ob_ref, pb_ref), (oc_ref, pc_ref))):
        gate = jnp.dot(xn, win_ref[:, goff + i * d:goff + (i + 1) * d], preferred_element_type=F32)
        term = _sigmoid(gate) * _mm(o_b[...], p_b[...])
        merged = term if merged is None else merged + term
    y = _mm(merged, wo_ref[...])
    o_ref[...] = x + _rms(y, g_ref[...])


def _merge(x, gpre, w_in, goff, oa, ob, oc, pa, pb, pc, wo, g, l):
    m, d = x.shape
    tm = _tile(m, MERGE_ROWS)

    def rows(w):
        return pl.BlockSpec((tm, w), lambda i: (i, 0))

    def whole(a):
        return pl.BlockSpec((None,) + a.shape[1:], lambda i: (l,) + (0,) * (a.ndim - 1),
                            pipeline_mode=pl.Buffered(1))

    return pl.pallas_call(
        functools.partial(_merge_kernel, goff),
        grid=(m // tm,),
        in_specs=[rows(d), whole(gpre), whole(w_in), rows(oa.shape[1]), rows(ob.shape[1]),
                  rows(oc.shape[1]), whole(pa), whole(pb), whole(pc), whole(wo), whole(g)],
        out_specs=rows(d),
        out_shape=jax.ShapeDtypeStruct((m, d), F32),
        compiler_params=_cparams("parallel"),
        name="merge_out",
    )(x, gpre, w_in, oa, ob, oc, pa, pb, pc, wo, g)


def _rwkv_prep(heads, hd, dw, da, nb, clen, p_ref, r0, prm, prev_ref, pf_ref, ops):
    (mu_ref, w0_ref, w2_ref, a0_ref, a2_ref, g2_ref, kk_ref, ka_ref, rk_ref) = prm
    l2_ref, bt_ref, kt_ref, h2_ref, v_ref, bv_ref, g_ref, egl_ref = ops
    aw = heads * hd
    seq = [slice(j * clen, (j + 1) * clen) for j in range(nb)]

    p = jnp.concatenate([p_ref[j, r0:r0 + clen, :] for j in range(nb)], axis=0)
    pf_ref[...] = pltpu.roll(p, 1, 0)
    for j in range(nb):
        pf_ref[j * clen:j * clen + 1, :] = prev_ref[j:j + 1, :]
        prev_ref[j:j + 1, :] = p_ref[j, r0 + clen - 1:r0 + clen, :]
    xs = p + mu_ref[...] * (pf_ref[...] - p)
    yield
    r = xs[:, :aw]
    k = xs[:, aw:2 * aw]
    v = xs[:, 2 * aw:3 * aw]
    w1 = xs[:, 3 * aw:3 * aw + dw]
    a1 = xs[:, 3 * aw + dw:3 * aw + dw + da]
    g1 = xs[:, 3 * aw + dw + da:]
    w = -_softplus(-(w0_ref[...] + _mm(jnp.tanh(w1), w2_ref[...]))) - 0.5
    logw = -jnp.exp(w)
    yield
    a_icl = _sigmoid(a0_ref[...] + _mm(a1, a2_ref[...]))
    gate = _mm(_sigmoid(g1), g2_ref[...])
    for j in range(nb):
        g_ref[j] = gate[seq[j]]
    yield
    kkraw = k * kk_ref[...]
    k2 = k * (1.0 + (a_icl - 1.0) * ka_ref[...])
    rkk = r * k2 * rk_ref[...]
    ginc = _cumsum_rows(logw, clen)
    yield
    eg = jnp.exp(ginc)
    egx = jnp.exp(ginc - logw)
    einv = jnp.exp(-ginc)
    yield
    scale = lax.rsqrt(jnp.maximum(_head_sums(kkraw * kkraw, hd), KK_EPS * KK_EPS))
    yield
    bonus = _head_sums(rkk, hd)
    kk = kkraw * scale
    bvec = kk * a_icl
    yield
    for j in range(nb):
        l2_ref[j, :clen, :] = -kk[seq[j]] * egx[seq[j]]
        l2_ref[j, clen:, :] = r[seq[j]] * eg[seq[j]]
    yield
    for j in range(nb):
        bt_ref[j] = bvec[seq[j]] * einv[seq[j]]
        kt_ref[j] = k2[seq[j]] * einv[seq[j]]
    yield
    for j in range(nb):
        glast = ginc[(j + 1) * clen - 1:(j + 1) * clen, :]
        ehat = jnp.exp(glast - ginc[seq[j]])
        h2_ref[j, :clen, :] = bvec[seq[j]] * ehat
        h2_ref[j, clen:, :] = k2[seq[j]] * ehat
        egl_ref[j] = jnp.exp(glast)
    yield
    for j in range(nb):
        v_ref[j] = v[seq[j]]
        bv_ref[j] = bonus[seq[j]] * v[seq[j]]
    yield


def _rwkv_chain(heads, hd, nb, clen, ops, st_ref, lng_ref, lnb_ref, o_ref, o_r0):
    l2_ref, bt_ref, kt_ref, h2_ref, v_ref, bv_ref, g_ref, egl_ref = ops
    per = LANES // hd
    cw = per * clen
    units = [(j, g) for j in range(nb) for g in range(heads // per)]

    def ld(ref):
        return [ref[j, :, g * LANES:(g + 1) * LANES] for j, g in units]

    ri = lax.broadcasted_iota(jnp.int32, (2 * clen, cw), 0)
    cpos = lax.broadcasted_iota(jnp.int32, (2 * clen, cw), 1) & (clen - 1)
    mask2 = cpos < jnp.where(ri < clen, ri, ri - (clen - 1))
    eye = ((lax.broadcasted_iota(jnp.int32, (clen, cw), 1) & (clen - 1))
           == lax.broadcasted_iota(jnp.int32, (clen, cw), 0)).astype(F32)
    blk = _lane_block(hd, LANES, hd)

    lhs2 = ld(l2_ref)
    ab2 = [jnp.where(mask2, _mm_nt(x, _blockdiag(y, hd)), 0.0) for x, y in zip(lhs2, ld(bt_ref))]
    yield
    ak2 = [jnp.where(mask2, _mm_nt(x, _blockdiag(y, hd)), 0.0) for x, y in zip(lhs2, ld(kt_ref))]
    yield
    s0 = [st_ref[j, g] for j, g in units]
    vs = ld(v_ref)
    xy = [_mm_nt(x, _blockdiag(s, hd)) + _mm(a, _blockdiag(vv, hd))
          for x, s, a, vv in zip(lhs2, s0, ak2, vs)]
    yield
    pw = [m[:clen] for m in ab2]
    tinv = [eye + m for m in pw]
    span = 2
    while span < clen:
        pwd = [_blockdiag(m, clen) for m in pw]
        pw = [_mm(m, md) for m, md in zip(pw, pwd)]
        yield
        pwd = [_blockdiag(m, clen) for m in pw]
        tinv = [t + _mm(t, md) for t, md in zip(tinv, pwd)]
        yield
        span *= 2
    u = [_mm(t, _blockdiag(m[:clen], hd)) for t, m in zip(tinv, xy)]
    yield
    y = [m[clen:] + _mm(a[clen:], _blockdiag(uu, hd)) for m, a, uu in zip(xy, ab2, u)]
    yield
    full = [_mm_tn(jnp.concatenate([uu, vv], axis=0), hh) for uu, vv, hh in zip(u, vs, ld(h2_ref))]
    for (j, g), s, f in zip(units, s0, full):
        own = f[(per - 1) * hd:, :]
        for i in reversed(range(per - 1)):
            own = jnp.where(blk == i, f[i * hd:(i + 1) * hd, :], own)
        st_ref[j, g] = s * egl_ref[j, :, g * LANES:(g + 1) * LANES] + own
    yield
    bvs = ld(bv_ref)
    gates = ld(g_ref)
    for i, (j, g) in enumerate(units):
        sl = slice(g * LANES, (g + 1) * LANES)
        yc = y[i] - _head_sums(y[i], hd) * (1.0 / hd)
        var = _head_sums(yc * yc, hd) * (1.0 / hd)
        yy = yc * lax.rsqrt(var + GN_EPS) * lng_ref[:, sl] + lnb_ref[:, sl]
        o_ref[j, o_r0:o_r0 + clen, sl] = (yy + bvs[i]) * gates[i]
        if i % (heads // per) == heads // per - 1:
            yield


def _rwkv_step_kernel(heads, hd, dw, da, nb, clen, nsub, first, *refs):
    refs = list(refs)
    p_ref = refs.pop(0)
    pn_ref = refs.pop(0) if nsub == 2 else None
    shift_ref, s0_ref = refs[:2]
    prm = refs[2:11]
    lng_ref, lnb_ref = refs[11:13]
    rest = refs[13:] if first else refs[14:]
    o_ref, nshift_ref, ns_ref, st_ref, prev_ref, pf_ref = rest[:6]
    ops_a = rest[6:14]
    ops_b = rest[14:22]
    c = pl.program_id(1)
    per = LANES // hd
    groups = [(j, g) for j in range(nb) for g in range(heads // per)]

    @pl.when(c == 0)
    def _():
        prev_ref[...] = shift_ref[...]
        for j, g in groups:
            st_ref[j, g] = jnp.concatenate([s0_ref[j, g * per + i] for i in range(per)], axis=1)

    def prep(src, r0, ops):
        return _rwkv_prep(heads, hd, dw, da, nb, clen, src, r0, prm, prev_ref, pf_ref, ops)

    def chain(ops, o_r0):
        return _rwkv_chain(heads, hd, nb, clen, ops, st_ref, lng_ref, lnb_ref, o_ref, o_r0)

    if nsub == 1:
        _run_streams(prep(p_ref, 0, ops_a))
        _run_streams(chain(ops_a, 0))
    else:
        @pl.when(c == 0)
        def _():
            _run_streams(prep(p_ref, 0, ops_a))

        _run_streams(chain(ops_a, 0), prep(p_ref, clen, ops_b))
        _run_streams(chain(ops_b, clen), prep(pn_ref, 0, ops_a))

    @pl.when(c == pl.num_programs(1) - 1)
    def _():
        nshift_ref[...] = prev_ref[...]
        if first:
            ns_ref[1:] = jnp.zeros((ns_ref.shape[0] - 1,) + ns_ref.shape[1:], F32)
        for j, g in groups:
            for i in range(per):
                own = st_ref[j, g, :, i * hd:(i + 1) * hd]
                if first:
                    ns_ref[0, j, g * per + i] = own
                else:
                    ns_ref[j, g * per + i] = own


def _rwkv_mix(pa, shift, s0, stacked, P, l, nseq, tlen, chunk, nb):
    ap = pa.shape[1]
    depth, _, heads, hd, _ = s0.shape
    aw = heads * hd
    dw = P['rwkv_w2'].shape[1]
    da = P['rwkv_a2'].shape[1]
    nch = tlen // chunk
    nsub = 2 if nch % 2 == 0 else 1
    assert nsub == 2 or nch == 1
    nstep = nch // nsub
    ngrp = nseq // nb
    shift = shift.reshape(depth, ngrp, nb, ap)
    pa = pa.reshape(nseq, tlen, ap)

    def vec(a):
        return pl.BlockSpec((None,) + a.shape[1:], lambda b, c: (l,) + (0,) * (a.ndim - 1))

    names = ['rwkv_mu', 'rwkv_w0', 'rwkv_w2', 'rwkv_a0', 'rwkv_a2', 'rwkv_g2', 'rwkv_k_k',
             'rwkv_k_a', 'rwkv_r_k', 'rwkv_ln_g', 'rwkv_ln_b']
    params = [P[n] for n in names]
    args = [pa]
    in_specs = [pl.BlockSpec((nb, nsub * chunk, ap), lambda b, c: (b, c, 0))]
    if nsub == 2:
        args.append(pa)
        in_specs.append(pl.BlockSpec(
            (nb, chunk, ap), lambda b, c: (b, jnp.minimum(2 * c + 2, nch - 1), 0)))
    args += [shift, s0] + params
    in_specs += [
        pl.BlockSpec((None, None, nb, ap), lambda b, c: (l, b, 0, 0)),
        pl.BlockSpec((None, nb, heads, hd, hd), lambda b, c: (l, b, 0, 0, 0)),
    ] + [vec(a) for a in params]
    first = stacked is None
    if first:
        aliases = {}
        state_spec = pl.BlockSpec((depth, nb, heads, hd, hd), lambda b, c: (0, b, 0, 0, 0))
    else:
        aliases = {len(args): 2}
        args.append(stacked)
        in_specs.append(pl.BlockSpec(memory_space=pl.ANY))
        state_spec = pl.BlockSpec((None, nb, heads, hd, hd), lambda b, c: (l, b, 0, 0, 0))

    def operand_set():
        f = lambda r: pltpu.VMEM((nb, r, aw), F32)
        return [f(2 * chunk), f(chunk), f(chunk), f(2 * chunk), f(chunk), f(chunk), f(chunk), f(1)]

    o, n_shift, stacked = pl.pallas_call(
        functools.partial(_rwkv_step_kernel, heads, hd, dw, da, nb, chunk, nsub, first),
        grid=(ngrp, nstep),
        in_specs=in_specs,
        out_specs=[
            pl.BlockSpec((nb, nsub * chunk, aw), lambda b, c: (b, c, 0)),
            pl.BlockSpec((None, nb, ap), lambda b, c: (b, 0, 0)),
            state_spec,
        ],
        out_shape=[
            jax.ShapeDtypeStruct((nseq, tlen, aw), F32),
            jax.ShapeDtypeStruct((ngrp, nb, ap), F32),
            jax.ShapeDtypeStruct((depth, nseq, heads, hd, hd), F32),
        ],
        scratch_shapes=[pltpu.VMEM((nb, aw // LANES, hd, LANES), F32), pltpu.VMEM((nb, ap), F32),
                        pltpu.VMEM((nb * chunk, ap), F32)] + operand_set() + operand_set(),
        input_output_aliases=aliases,
        compiler_params=_cparams("parallel", "arbitrary"),
        name="rwkv7",
    )(*args)
    return o.reshape(nseq * tlen, aw), n_shift.reshape(nseq, ap), stacked


def _lru_kernel(bw, seqs, p_ref, cs_ref, h0_ref, cw_ref, cb_ref, wa_ref, ba_ref, wx_ref, bx_ref,
                lam_ref, o_ref, nconv_ref, nh_ref, tail_ref, h_ref):
    c = pl.program_id(1)
    rows = p_ref.shape[0]
    ncv = cs_ref.shape[1]
    nseg = rows // SUBLANES

    if seqs == 1:
        @pl.when(c == 0)
        def _():
            tail_ref[...] = jnp.zeros_like(tail_ref)
            tail_ref[SUBLANES - ncv:, :] = cs_ref[0]
            h_ref[...] = h0_ref[0]
    else:
        tail_ref[...] = jnp.zeros_like(tail_ref)
        for j in range(seqs):
            tail_ref[(j + 1) * SUBLANES - ncv:(j + 1) * SUBLANES, :] = cs_ref[j]

    xb = p_ref[:, :bw]
    gb = p_ref[:, bw:]
    rowi = lax.broadcasted_iota(jnp.int32, (rows, 1), 0)
    pos = rowi & (SUBLANES - 1)

    def seg_roll(x, d):
        return pltpu.roll(x.reshape(nseg, SUBLANES, bw), d, 1).reshape(rows, bw)

    if seqs == 1 and rows > SUBLANES:
        before = jnp.concatenate([tail_ref[...], xb[:rows - SUBLANES, :]], axis=0)
    else:
        before = tail_ref[...]
    xc = cb_ref[...] + xb * cw_ref[ncv:ncv + 1, :]
    for d in range(1, ncv + 1):
        sh = jnp.where(pos >= d, seg_roll(xb, d), seg_roll(before, d))
        xc = xc + sh * cw_ref[ncv - d:ncv - d + 1, :]
    if seqs == 1:
        tail_ref[...] = xb[rows - SUBLANES:, :]

    rg = _sigmoid(_mm(xc, wa_ref[...]) + ba_ref[...])
    ig = _sigmoid(_mm(xc, wx_ref[...]) + bx_ref[...])
    log_a = -LRU_C * rg * _softplus(-lam_ref[...])
    a = jnp.exp(log_a)
    b = jnp.sqrt(-jnp.tanh(log_a) * (1.0 + a * a)) * ig * xc

    d = 1
    while d < SUBLANES:
        keep = pos >= d
        a_sh = jnp.where(keep, seg_roll(a, d), 1.0)
        b_sh = jnp.where(keep, seg_roll(b, d), 0.0)
        b = a * b_sh + b
        a = a_sh * a
        d *= 2
    pieces = []
    hc = h_ref[...] if seqs == 1 else None
    for s in range(nseg):
        rs = slice(s * SUBLANES, (s + 1) * SUBLANES)
        h_s = a[rs] * (hc if seqs == 1 else h0_ref[s]) + b[rs]
        hc = h_s[SUBLANES - 1:, :]
        pieces.append(h_s)
    h = pieces[0] if nseg == 1 else jnp.concatenate(pieces, axis=0)
    o_ref[...] = h * gb

    if seqs == 1:
        h_ref[...] = hc

        @pl.when(c == pl.num_programs(1) - 1)
        def _():
            nconv_ref[0] = xb[rows - ncv:, :]
            nh_ref[0] = hc
    else:
        for s in range(seqs):
            nconv_ref[s] = xb[(s + 1) * SUBLANES - ncv:(s + 1) * SUBLANES, :]
            nh_ref[s] = pieces[s][SUBLANES - 1:, :]


def _lru(pb, conv, h0, P, l, nseq, tlen, tile, seqs):
    bw = h0.shape[-1]
    ncv = conv.shape[2]
    nt = tlen // tile
    assert seqs == 1 or (nt == 1 and tile == SUBLANES)
    rows = seqs * tile

    def vec(a):
        return pl.BlockSpec((None,) + a.shape[1:], lambda b, c: (l,) + (0,) * (a.ndim - 1))

    names = ['lru_conv_w', 'lru_conv_b', 'lru_wa_bd', 'lru_ba', 'lru_wx_bd', 'lru_bx', 'lru_lam']
    params = [P[n] for n in names]
    return pl.pallas_call(
        functools.partial(_lru_kernel, bw, seqs),
        grid=(nseq // seqs, nt),
        in_specs=[
            pl.BlockSpec((rows, 2 * bw), lambda b, c: (b * nt + c, 0)),
            pl.BlockSpec((None, seqs, ncv, bw), lambda b, c: (l, b, 0, 0)),
            pl.BlockSpec((None, seqs, 1, bw), lambda b, c: (l, b, 0, 0)),
        ] + [vec(a) for a in params],
        out_specs=[
            pl.BlockSpec((rows, bw), lambda b, c: (b * nt + c, 0)),
            pl.BlockSpec((seqs, ncv, bw), lambda b, c: (b, 0, 0)),
            pl.BlockSpec((seqs, 1, bw), lambda b, c: (b, 0, 0)),
        ],
        out_shape=[
            jax.ShapeDtypeStruct((nseq * tlen, bw), F32),
            jax.ShapeDtypeStruct((nseq, ncv, bw), F32),
            jax.ShapeDtypeStruct((nseq, 1, bw), F32),
        ],
        scratch_shapes=[pltpu.VMEM((SUBLANES if seqs == 1 else rows, bw), F32),
                        pltpu.VMEM((1, bw), F32)],
        compiler_params=_cparams("parallel", "arbitrary"),
        name="rglru",
    )(pb, conv, h0, *params)


def _hgrn_kernel(heads, kd, vd, sub, nb, clen, first, p_ref, s0_ref, ng_ref, *rest):
    o_ref, ns_ref, st_ref = rest[-3:]
    c = pl.program_id(1)
    kw = heads * kd
    vw = heads * vd
    pairs = [(j, h) for j in range(nb) for h in range(heads)]

    @pl.when(c == 0)
    def _():
        for j, h in pairs:
            st_ref[j, h] = s0_ref[j, h].T

    p = p_ref[0] if nb == 1 else jnp.concatenate([p_ref[j] for j in range(nb)], axis=0)
    q = p[:, :kw]
    f = p[:, kw:2 * kw]
    logf = jnp.log(f)
    kf = 1.0 - f
    v = p[:, 2 * kw:2 * kw + vw]
    gzs = p[:, 2 * kw + vw:]

    g = _cumsum_rows(logf, clen)
    qg = q * jnp.exp(g)
    glast = [g[(j + 1) * clen - 1:(j + 1) * clen, :] for j in range(nb)]
    khat = [kf[j * clen:(j + 1) * clen, :] * jnp.exp(glast[j] - g[j * clen:(j + 1) * clen, :])
            for j in range(nb)]
    eglast = [jnp.exp(x) for x in glast]

    rows = nb * clen
    nblk = clen // sub
    rowi = lax.broadcasted_iota(jnp.int32, (rows, 1), 0)

    def usl(x, j, h, width):
        return x[j * clen:(j + 1) * clen, h * width:(h + 1) * width]

    st = [st_ref[j, h] for j, h in pairs]
    vs = [usl(v, j, h, vd) for j, h in pairs]
    o = [_mm_nt(usl(qg, j, h, kd), s) for (j, h), s in zip(pairs, st)]
    st_new = [s * eglast[j][:, h * kd:(h + 1) * kd] + _mm_tn(vv, khat[j][:, h * kd:(h + 1) * kd])
              for (j, h), s, vv in zip(pairs, st, vs)]
    for (j, h), s in zip(pairs, st_new):
        st_ref[j, h] = s

    ri = lax.broadcasted_iota(jnp.int32, (clen, clen), 0)
    ci = lax.broadcasted_iota(jnp.int32, (clen, clen), 1)
    att = None
    hs = sub
    while hs < clen:
        later = (rowi & hs) != 0
        gref = jnp.concatenate(
            [jnp.broadcast_to(g[r0 + hs - 1:r0 + hs, :], (2 * hs, kw)) for r0 in range(0, rows, 2 * hs)],
            axis=0)
        q_rel = q * jnp.exp(jnp.where(later, g - gref, -jnp.inf))
        k_rel = kf * jnp.exp(jnp.where(later, -jnp.inf, gref - g))
        sh = (2 * hs).bit_length() - 1
        same = lax.shift_right_logical(ri, sh) == lax.shift_right_logical(ci, sh)
        part = [jnp.where(same, _mm_nt(usl(q_rel, j, h, kd), usl(k_rel, j, h, kd)), 0.0)
                for j, h in pairs]
        att = part if att is None else [a + b for a, b in zip(att, part)]
        hs *= 2
    if att is not None:
        o = [oo + _mm(a, vv) for oo, a, vv in zip(o, att, vs)]

    def blocks3(x, j, h):
        return usl(x, j, h, kd).reshape(nblk, sub, kd)

    f3 = [blocks3(f, j, h) for j, h in pairs]
    q3 = [blocks3(q, j, h) for j, h in pairs]
    k3 = [blocks3(kf, j, h) for j, h in pairs]
    srow = lax.broadcasted_iota(jnp.int32, (nblk, sub, 1), 1)
    tcol = (lax.broadcasted_iota(jnp.int32, (nblk, sub, LANES), 2)
            - sub * lax.broadcasted_iota(jnp.int32, (nblk, sub, LANES), 0))
    att_t = [jnp.zeros((nblk, sub, LANES), F32) for _ in pairs]
    dec = [jnp.zeros((nblk, sub, kd), F32) for _ in pairs]
    for t in range(sub):
        dec = [jnp.where(srow == t, kk, d * ff[:, t:t + 1, :]) for d, ff, kk in zip(dec, f3, k3)]
        cols = [jnp.sum(qq[:, t:t + 1, :] * d, axis=-1, keepdims=True) for d, qq in zip(dec, q3)]
        att_t = [jnp.where(tcol == t, col, a) for col, a in zip(cols, att_t)]
    o = [oo + _mm_tn(a.reshape(clen, LANES)[:, :clen], vv) for oo, a, vv in zip(o, att_t, vs)]

    for (j, h), oo in zip(pairs, o):
        hv = slice(h * vd, (h + 1) * vd)
        oo = oo * lax.rsqrt(jnp.mean(oo * oo, axis=-1, keepdims=True) + RMS_EPS) * ng_ref[:, hv]
        o_ref[j, :, hv] = oo * gzs[j * clen:(j + 1) * clen, hv]

    @pl.when(c == pl.num_programs(1) - 1)
    def _():
        if first:
            ns_ref[1:] = jnp.zeros((ns_ref.shape[0] - 1,) + ns_ref.shape[1:], F32)
        for j, h in pairs:
            if first:
                ns_ref[0, j, h] = st_ref[j, h].T
            else:
                ns_ref[j, h] = st_ref[j, h].T


def _hgrn(pc, s0, stacked, ng, l, nseq, tlen, chunk, nb):
    depth, _, heads, kd, vd = s0.shape
    width = pc.shape[1]
    nch = tlen // chunk
    sub = min(HGRN_SUB, chunk)
    args = [pc.reshape(nseq, tlen, width), s0, ng]
    in_specs = [
        pl.BlockSpec((nb, chunk, width), lambda b, c: (b, c, 0)),
        pl.BlockSpec((None, nb, heads, kd, vd), lambda b, c: (l, b, 0, 0, 0)),
        pl.BlockSpec((None, 1, heads * vd), lambda b, c: (l, 0, 0)),
    ]
    first = stacked is None
    if first:
        aliases = {}
        state_spec = pl.BlockSpec((depth, nb, heads, kd, vd), lambda b, c: (0, b, 0, 0, 0))
    else:
        aliases = {len(args): 1}
        args.append(stacked)
        in_specs.append(pl.BlockSpec(memory_space=pl.ANY))
        state_spec = pl.BlockSpec((None, nb, heads, kd, vd), lambda b, c: (l, b, 0, 0, 0))
    o, stacked = pl.pallas_call(
        functools.partial(_hgrn_kernel, heads, kd, vd, sub, nb, chunk, first),
        grid=(nseq // nb, nch),
        in_specs=in_specs,
        out_specs=[pl.BlockSpec((nb, chunk, heads * vd), lambda b, c: (b, c, 0)), state_spec],
        out_shape=[
            jax.ShapeDtypeStruct((nseq, tlen, heads * vd), F32),
            jax.ShapeDtypeStruct((depth, nseq, heads, kd, vd), F32),
        ],
        scratch_shapes=[pltpu.VMEM((nb, heads, vd, kd), F32)],
        input_output_aliases=aliases,
        compiler_params=_cparams("parallel", "arbitrary"),
        name="hgrn2",
    )(*args)
    return o.reshape(nseq * tlen, heads * vd), stacked


def _run_trunk(x, s_rwkv, s_shift, s_lru, s_conv, s_hgrn, P):
    nseq, tlen, d = x.shape
    depth = s_rwkv.shape[0]
    x = x.reshape(nseq * tlen, d)
    lru4 = s_lru[:, :, None, :]
    rw_chunk = min(RWKV_CHUNK, tlen)
    rw_nb = _tile(nseq, max(RWKV_SEQS, RWKV_CHUNK // rw_chunk))
    hg_chunk = min(HGRN_CHUNK, tlen)
    hg_nb = _tile(nseq, max(HGRN_SEQS, HGRN_CHUNK // hg_chunk))
    lru_tile = min(LRU_TILE, tlen)
    lru_seqs = _tile(nseq, LRU_SEQS) if lru_tile == tlen == SUBLANES else 1
    outs = ([], [], [])
    n_rwkv = n_hgrn = None
    for l in range(depth):
        x = _ffn(x, P['ffn1_pre_g'], P['ffn1_wg'], P['ffn1_wu'], P['ffn1_wd'], P['ffn1_post_g'], l)
        pa, pb, pc = _in_proj(x, P['mix_pre_g'], P['w_in'], P['hgrn_lb'], P['w_in_sections'],
                              s_lru.shape[-1], s_hgrn.shape[2] * s_hgrn.shape[3],
                              s_hgrn.shape[2] * s_hgrn.shape[4], l)
        oa, n_shift, n_rwkv = _rwkv_mix(pa, s_shift, s_rwkv, n_rwkv, P, l, nseq, tlen, rw_chunk,
                                        rw_nb)
        ob, n_conv, n_lru = _lru(pb, s_conv, lru4, P, l, nseq, tlen, lru_tile, lru_seqs)
        oc, n_hgrn = _hgrn(pc, s_hgrn, n_hgrn, P['hgrn_norm_g'], l, nseq, tlen, hg_chunk, hg_nb)
        x = _merge(x, P['mix_pre_g'], P['w_in'], sum(P['w_in_sections']), oa, ob, oc, P['proj_a'],
                   P['proj_b'], P['proj_c'], P['w_out'], P['mix_post_g'], l)
        x = _ffn(x, P['ffn2_pre_g'], P['ffn2_wg'], P['ffn2_wu'], P['ffn2_wd'], P['ffn2_post_g'], l)
        for lst, t in zip(outs, (n_shift, n_lru[:, 0], n_conv)):
            lst.append(t)
    n_shift, n_lru, n_conv = (jnp.stack(lst, axis=0) for lst in outs)
    return x.reshape(nseq, tlen, d), (n_rwkv, n_shift, n_lru, n_conv, n_hgrn)


def _block_diag(w):
    depth, g, i, j = w.shape
    eye = jnp.eye(g, dtype=w.dtype)
    return jnp.einsum('lgij,gh->lgihj', w, eye).reshape(depth, g * i, g * j)


def _prepare_params(raw, a_proj, b_width, c_kwidth, c_width):
    P = {}
    for n in ('ffn1_wg', 'ffn1_wu', 'ffn1_wd', 'ffn2_wg', 'ffn2_wu', 'ffn2_wd', 'proj_a', 'proj_b',
              'proj_c', 'w_out', 'w_in', 'rwkv_w2', 'rwkv_a2', 'rwkv_g2'):
        P[n] = raw[n].astype(BF16)
    P['w_in_sections'] = (a_proj, 2 * b_width, 2 * c_kwidth + 2 * c_width)
    for n in ('ffn1_pre_g', 'ffn1_post_g', 'mix_pre_g', 'mix_post_g', 'ffn2_pre_g', 'ffn2_post_g',
              'rwkv_mu', 'rwkv_w0', 'rwkv_a0', 'rwkv_k_k', 'rwkv_k_a', 'rwkv_ln_g', 'rwkv_ln_b',
              'lru_conv_b', 'lru_ba', 'lru_bx', 'lru_lam', 'hgrn_norm_g'):
        P[n] = raw[n][:, None, :]
    depth = raw['rwkv_r_k'].shape[0]
    P['rwkv_r_k'] = raw['rwkv_r_k'].reshape(depth, 1, -1)
    P['lru_conv_w'] = raw['lru_conv_w']
    P['lru_wa_bd'] = _block_diag(raw['lru_wa']).astype(BF16)
    P['lru_wx_bd'] = _block_diag(raw['lru_wx']).astype(BF16)
    lb_cum = jnp.cumsum(jax.nn.softmax(raw['hgrn_lb_logits'].astype(F32), axis=0), axis=0)
    P['hgrn_lb'] = (lb_cum - lb_cum[0])[:, None, :]
    return P


def kernel(x_prompt, x_sample, state_rwkv, state_shift, state_lru, state_conv, state_hgrn, ffn1_pre_g, ffn1_post_g, ffn1_wg, ffn1_wu, ffn1_wd, mix_pre_g, mix_post_g, w_in, rwkv_mu, rwkv_w0, rwkv_w2, rwkv_a0, rwkv_a2, rwkv_g2, rwkv_k_k, rwkv_k_a, rwkv_r_k, rwkv_ln_g, rwkv_ln_b, lru_conv_w, lru_conv_b, lru_wa, lru_ba, lru_wx, lru_bx, lru_lam, hgrn_lb_logits, hgrn_norm_g, proj_a, proj_b, proj_c, w_out, ffn2_pre_g, ffn2_post_g, ffn2_wg, ffn2_wu, ffn2_wd):
    raw = dict(
        ffn1_pre_g=ffn1_pre_g, ffn1_post_g=ffn1_post_g, ffn1_wg=ffn1_wg, ffn1_wu=ffn1_wu,
        ffn1_wd=ffn1_wd, mix_pre_g=mix_pre_g, mix_post_g=mix_post_g, w_in=w_in, rwkv_mu=rwkv_mu,
        rwkv_w0=rwkv_w0, rwkv_w2=rwkv_w2, rwkv_a0=rwkv_a0, rwkv_a2=rwkv_a2, rwkv_g2=rwkv_g2,
        rwkv_k_k=rwkv_k_k, rwkv_k_a=rwkv_k_a, rwkv_r_k=rwkv_r_k, rwkv_ln_g=rwkv_ln_g,
        rwkv_ln_b=rwkv_ln_b, lru_conv_w=lru_conv_w, lru_conv_b=lru_conv_b, lru_wa=lru_wa,
        lru_ba=lru_ba, lru_wx=lru_wx, lru_bx=lru_bx, lru_lam=lru_lam,
        hgrn_lb_logits=hgrn_lb_logits, hgrn_norm_g=hgrn_norm_g, proj_a=proj_a, proj_b=proj_b,
        proj_c=proj_c, w_out=w_out, ffn2_pre_g=ffn2_pre_g, ffn2_post_g=ffn2_post_g,
        ffn2_wg=ffn2_wg, ffn2_wu=ffn2_wu, ffn2_wd=ffn2_wd)
    depth, _, heads, hd, _ = state_rwkv.shape
    a_proj = state_shift.shape[-1]
    b_width = state_lru.shape[-1]
    _, _, c_heads, c_kd, c_vd = state_hgrn.shape
    P = _prepare_params(raw, a_proj, b_width, c_heads * c_kd, c_heads * c_vd)
    dt = state_rwkv.dtype
    bp = x_prompt.shape[0]
    y_prompt, p_st = _run_trunk(
        x_prompt,
        jnp.zeros((depth, bp, heads, hd, hd), dt),
        jnp.zeros((depth, bp, a_proj), dt),
        jnp.zeros((depth, bp, b_width), dt),
        jnp.zeros((depth, bp) + state_conv.shape[2:], dt),
        jnp.zeros((depth, bp, c_heads, c_kd, c_vd), dt),
        P)
    y_sample, s_st = _run_trunk(x_sample, state_rwkv, state_shift, state_lru, state_conv,
                                state_hgrn, P)
    return (y_prompt, y_sample) + tuple(t.astype(dt) for t in p_st) + tuple(t.astype(dt) for t in s_st)
```
